```python
import math
import jax, jax.numpy as jnp
from jax import lax
import numpy as np

D_MODEL = 1024
BATCH = 4
SEQ = 4096
DEPTH = 4

D_MIX = D_MODEL
A_WIDTH = D_MIX // 2
A_GROUPS = 4
A_CH = A_WIDTH // A_GROUPS
CHUNK = 128
SB_WIDTH = D_MIX - A_WIDTH
SB_HEADS = 8
SB_HEAD_DIM = SB_WIDTH // SB_HEADS
QBLK = 128
IN_WIDTH = 2 * A_WIDTH + 3 * SB_WIDTH
MEM_LEN = 256
X_HEADS = 4
X_HEAD_DIM = D_MODEL // X_HEADS
N_GROUPS = 4
EXPERTS_PER_GROUP = 8
N_EXPERTS = N_GROUPS * EXPERTS_PER_GROUP
TOP_K = 2
D_EXPERT = D_MODEL // 2
MOE_BLOCK = 128
EPS = 1e-6

kernel_name = "hybrid_gmlp_stickbreak_hiermoe_trunk"


def rmsnorm(x, g):
    xf = x.astype(jnp.float32)
    y = xf * lax.rsqrt(jnp.mean(xf * xf, axis=-1, keepdims=True) + EPS)
    return (y * g.astype(jnp.float32)).astype(x.dtype)


def chunk_gmlp(u, v, v_gain, w_s, b_s):
    B, S = u.shape[0], u.shape[1]
    n_chunks = S // CHUNK
    u = jax.nn.gelu(u)
    v = rmsnorm(jax.nn.gelu(v), v_gain)
    vc = v.reshape(B, n_chunks, CHUNK, A_GROUPS, A_CH)
    mask = jnp.tril(jnp.ones((CHUNK, CHUNK), dtype=w_s.dtype))
    ws = w_s * mask[None]
    mixed = jnp.einsum('gts,bcsgd->bctgd', ws, vc) + b_s.T[None, None, :, :, None]
    return u * mixed.reshape(B, S, A_GROUPS, A_CH)


def stick_breaking_attention(q, k, v):
    B, S, H, Dh = q.shape
    nb = S // QBLK
    scale = 1.0 / math.sqrt(Dh)
    qb = q.reshape(B, nb, QBLK, H, Dh).transpose(1, 0, 3, 2, 4)
    kh = k.transpose(0, 2, 1, 3)
    vh = v.transpose(0, 2, 1, 3)
    key_pos = jnp.arange(S)

    def block(args):
        i, qi = args
        z = jnp.einsum('bhqd,bhkd->bhqk', qi, kh).astype(jnp.float32) * scale
        q_pos = i * QBLK + jnp.arange(QBLK)
        strict = key_pos[None, :] < q_pos[:, None]
        log_1mb = jnp.where(strict, jax.nn.log_sigmoid(-z), 0.0)
        after = lax.cumsum(log_1mb, axis=3, reverse=True) - log_1mb
        a = jnp.where(strict, jnp.exp(jax.nn.log_sigmoid(z) + after), 0.0)
        return jnp.einsum('bhqk,bhkd->bhqd', a.astype(vh.dtype), vh)

    out = lax.map(block, (jnp.arange(nb), qb))
    return out.transpose(1, 0, 3, 2, 4).reshape(B, S, H * Dh)


def memory_cross_attention(h, m, w_q, w_k, w_v, w_o):
    B, S, _ = h.shape
    M = m.shape[1]
    q = (h @ w_q).reshape(B, S, X_HEADS, X_HEAD_DIM)
    k = (m @ w_k).reshape(B, M, X_HEADS, X_HEAD_DIM)
    v = (m @ w_v).reshape(B, M, X_HEADS, X_HEAD_DIM)
    s = jnp.einsum('bshd,bmhd->bhsm', q, k).astype(jnp.float32) / math.sqrt(X_HEAD_DIM)
    p = jax.nn.softmax(s, axis=-1).astype(v.dtype)
    o = jnp.einsum('bhsm,bmhd->bshd', p, v).reshape(B, S, X_HEADS * X_HEAD_DIM)
    return o @ w_o


def hierarchical_moe(h, w_group, b_group, w_router, b_router, w1, w3, w2):
    B, S, D = h.shape
    t = h.reshape(-1, D)
    T = t.shape[0]
    g_logits = (t @ w_group).astype(jnp.float32) + b_group.astype(jnp.float32)
    p_group = jax.nn.softmax(g_logits, axis=-1)
    g_sel = jnp.argmax(g_logits, axis=-1)
    g_gate = jnp.take_along_axis(p_group, g_sel[:, None], axis=1)[:, 0]
    e_logits = ((t @ w_router).astype(jnp.float32) + b_router.astype(jnp.float32)).reshape(T, N_GROUPS, EXPERTS_PER_GROUP)
    e_sel_logits = jnp.take_along_axis(e_logits, g_sel[:, None, None], axis=1)[:, 0]
    top_v, top_i = lax.top_k(e_sel_logits, TOP_K)
    gates = g_gate[:, None] * jax.nn.softmax(top_v, axis=-1)
    eid = g_sel[:, None] * EXPERTS_PER_GROUP + top_i

    flat_e = eid.reshape(-1)
    N = T * TOP_K
    order = jnp.argsort(flat_e, stable=True)
    sorted_e = flat_e[order]
    counts = jnp.bincount(flat_e, length=N_EXPERTS)
    padded = ((counts + MOE_BLOCK - 1) // MOE_BLOCK) * MOE_BLOCK
    start = jnp.cumsum(counts) - counts
    pend = jnp.cumsum(padded)
    pstart = pend - padded
    dest = pstart[sorted_e] + (jnp.arange(N) - start[sorted_e])
    P = N + N_EXPERTS * MOE_BLOCK
    n_blocks = P // MOE_BLOCK
    buf = jnp.zeros((P, D), t.dtype).at[dest].set(t[order // TOP_K])
    blk_e = jnp.minimum(jnp.searchsorted(pend, jnp.arange(n_blocks) * MOE_BLOCK, side='right'), N_EXPERTS - 1)

    def expert_block(args):
        xb, e = args
        return (jax.nn.silu(xb @ w1[e]) * (xb @ w3[e])) @ w2[e]

    ybuf = lax.map(expert_block, (buf.reshape(n_blocks, MOE_BLOCK, D), blk_e)).reshape(P, D)
    y_assign = jnp.zeros((N, D), ybuf.dtype).at[order].set(ybuf[dest])
    y = jnp.einsum('tkd,tk->td', y_assign.reshape(T, TOP_K, D), gates.astype(ybuf.dtype))
    return y.reshape(B, S, D)


def setup_inputs(seed: int = 0) -> dict:
    key = jax.random.key(seed)
    ks = jax.random.split(key, 32)
    f32 = jnp.float32
    n = lambda k, shape, s: jax.random.normal(k, shape, f32) * s
    gain = lambda k, shape: 1.0 + 0.05 * jax.random.normal(k, shape, f32)
    L, D = DEPTH, D_MODEL
    return {
        "x": jax.random.normal(ks[0], (BATCH, SEQ, D), f32),
        "mem": jax.random.normal(ks[1], (BATCH, MEM_LEN, D), f32),
        "norm_mix": gain(ks[2], (L, D)),
        "w_in": n(ks[3], (L, D, IN_WIDTH), D ** -0.5),
        "v_norm": gain(ks[4], (L, A_GROUPS, A_CH)),
        "w_spatial": n(ks[5], (L, A_GROUPS, CHUNK, CHUNK), CHUNK ** -0.5),
        "b_spatial": gain(ks[6], (L, A_GROUPS, CHUNK)),
        "out_norm_a": gain(ks[7], (L, A_WIDTH)),
        "out_norm_b": gain(ks[8], (L, SB_WIDTH)),
        "w_out": n(ks[9], (L, D_MIX, D), D_MIX ** -0.5),
        "norm_cross": gain(ks[10], (L, D)),
        "norm_mem": gain(ks[11], (L, D)),
        "w_xq": n(ks[12], (L, D, D), D ** -0.5),
        "w_xk": n(ks[13], (L, D, D), D ** -0.5),
        "w_xv": n(ks[14], (L, D, D), D ** -0.5),
        "w_xo": n(ks[15], (L, D, D), D ** -0.5),
        "norm_moe": gain(ks[16], (L, D)),
        "w_group": n(ks[17], (L, D, N_GROUPS), D ** -0.5),
        "b_group": n(ks[18], (L, N_GROUPS), 0.01),
        "w_router": n(ks[19], (L, D, N_EXPERTS), D ** -0.5),
        "b_router": n(ks[20], (L, N_EXPERTS), 0.01),
        "w1": n(ks[21], (L, N_EXPERTS, D, D_EXPERT), D ** -0.5),
        "w3": n(ks[22], (L, N_EXPERTS, D, D_EXPERT), D ** -0.5),
        "w2": n(ks[23], (L, N_EXPERTS, D_EXPERT, D), D_EXPERT ** -0.5),
        "norm_final": gain(ks[24], (D,)),
    }


def reference(x, mem, norm_mix, w_in, v_norm, w_spatial, b_spatial, out_norm_a, out_norm_b, w_out,
              norm_cross, norm_mem, w_xq, w_xk, w_xv, w_xo, norm_moe, w_group, b_group, w_router,
              b_router, w1, w3, w2, norm_final):
    B, S, D = x.shape
    c1 = A_WIDTH
    c2 = 2 * A_WIDTH
    c3 = c2 + SB_WIDTH
    c4 = c3 + SB_WIDTH
    for l in range(DEPTH):
        h = rmsnorm(x, norm_mix[l])
        z = h @ w_in[l]
        u_a = z[..., :c1].reshape(B, S, A_GROUPS, A_CH)
        v_a = z[..., c1:c2].reshape(B, S, A_GROUPS, A_CH)
        q_b = z[..., c2:c3].reshape(B, S, SB_HEADS, SB_HEAD_DIM)
        k_b = z[..., c3:c4].reshape(B, S, SB_HEADS, SB_HEAD_DIM)
        v_b = z[..., c4:].reshape(B, S, SB_HEADS, SB_HEAD_DIM)
        y_a = chunk_gmlp(u_a, v_a, v_norm[l], w_spatial[l], b_spatial[l]).reshape(B, S, A_WIDTH)
        y_b = stick_breaking_attention(q_b, k_b, v_b)
        y = jnp.concatenate([rmsnorm(y_a, out_norm_a[l]), rmsnorm(y_b, out_norm_b[l])], axis=-1)
        x = x + y @ w_out[l]
        h = rmsnorm(x, norm_cross[l])
        m = rmsnorm(mem, norm_mem[l])
        x = x + memory_cross_attention(h, m, w_xq[l], w_xk[l], w_xv[l], w_xo[l])
        h = rmsnorm(x, norm_moe[l])
        x = x + hierarchical_moe(h, w_group[l], b_group[l], w_router[l], b_router[l], w1[l], w3[l], w2[l])
    return rmsnorm(x, norm_final)
```

```python
import functools

import jax
import jax.numpy as jnp
from jax import lax
from jax.experimental import pallas as pl
from jax.experimental.pallas import tpu as pltpu

F32 = jnp.float32
BF16 = jnp.bfloat16

D_MODEL = 1024
A_WIDTH = 512
A_GROUPS = 4
A_CH = 128
CHUNK = 128
SB_WIDTH = 512
SB_HEADS = 8
SB_HEAD_DIM = 64
SB_PAIRS = SB_HEADS // 2
IN_WIDTH = 2 * A_WIDTH + 3 * SB_WIDTH
X_HEADS = 4
X_HEAD_DIM = 256
N_GROUPS = 4
EXPERTS_PER_GROUP = 8
N_EXPERTS = 32
TOP_K = 2
D_EXPERT = 512
EPS = 1e-6

LANES = 128
SUBLANES = 8
ROW_TILES = D_MODEL // LANES

TM_IN = 512
TQ = 256
TK = 256
TM_POST = 256
TD = 512
BM = 256
TE = 256
ROUTE_BASE = N_GROUPS
NEG = -1e30

VMEM_LIMIT = 52 * 1024 * 1024


def _rms(x, g):
    return x * lax.rsqrt(jnp.mean(x * x, axis=-1, keepdims=True) + EPS) * g


def _dot(a, b):
    return jnp.dot(a, b, preferred_element_type=F32)


def _dot_nt(a, b):
    return lax.dot_general(a, b, (((1,), (1,)), ((), ())), preferred_element_type=F32)


def _split_bf16(x):
    hi = x.astype(BF16)
    lo = (x - hi.astype(F32)).astype(BF16)
    return hi, lo


def _mix_in_kernel(x_ref, g_ref, w_ref, vg_ref, ws_ref, bs_ref, ga_ref, q_ref, k_ref, v_ref, ya_ref):
    tm = x_ref.shape[0]
    h = _rms(x_ref[...], g_ref[...]).astype(BF16)
    z = _dot(h, w_ref[...])

    t_idx = lax.broadcasted_iota(jnp.int32, (CHUNK, CHUNK), 0)
    s_idx = lax.broadcasted_iota(jnp.int32, (CHUNK, CHUNK), 1)
    causal = s_idx <= t_idx
    parts = []
    ssq = jnp.zeros((tm, 1), F32)
    for g in range(A_GROUPS):
        lanes = slice(g * A_CH, (g + 1) * A_CH)
        u = jax.nn.gelu(z[:, g * A_CH:(g + 1) * A_CH])
        vg = jax.nn.gelu(z[:, A_WIDTH + g * A_CH:A_WIDTH + (g + 1) * A_CH])
        vn = _rms(vg, vg_ref[:, lanes]).astype(BF16)
        ws = jnp.where(causal, ws_ref[g], 0.0).astype(BF16)
        bias = bs_ref[:, g:g + 1]
        mixed = jnp.concatenate(
            [_dot(ws, vn[c * CHUNK:(c + 1) * CHUNK, :]) + bias for c in range(tm // CHUNK)], axis=0)
        ya = u * mixed
        parts.append(ya)
        ssq = ssq + jnp.sum(ya * ya, axis=-1, keepdims=True)
    inv = lax.rsqrt(ssq * (1.0 / A_WIDTH) + EPS)
    for g in range(A_GROUPS):
        lanes = slice(g * A_CH, (g + 1) * A_CH)
        ya_ref[:, lanes] = (parts[g] * inv * ga_ref[:, lanes]).astype(BF16)

    lane = lax.broadcasted_iota(jnp.int32, (1, LANES), 1)
    q0 = 2 * A_WIDTH
    k0 = q0 + SB_WIDTH
    v0 = k0 + SB_WIDTH
    scale = 1.0 / (SB_HEAD_DIM ** 0.5)
    for hd in range(SB_HEADS):
        pair = hd // 2
        keep = (lane < SB_HEAD_DIM) if hd % 2 == 0 else (lane >= SB_HEAD_DIM)
        qs = z[:, q0 + pair * LANES:q0 + (pair + 1) * LANES] * scale
        vs = z[:, v0 + pair * LANES:v0 + (pair + 1) * LANES]
        q_ref[:, hd * LANES:(hd + 1) * LANES] = jnp.where(keep, qs, 0.0).astype(BF16)
        v_ref[:, hd * LANES:(hd + 1) * LANES] = jnp.where(keep, vs, 0.0).astype(BF16)
    k_ref[...] = z[:, k0:v0].astype(BF16)


def _mix_in(x2d, g, w_in, vg, ws, bs_t, ga, l):
    T = x2d.shape[0]
    grid = (T // TM_IN,)
    row = lambda i: (i, 0)
    lay = lambda i: (l, 0, 0)
    return pl.pallas_call(
        _mix_in_kernel,
        grid=grid,
        in_specs=[
            pl.BlockSpec((TM_IN, D_MODEL), row),
            pl.BlockSpec((None, 1, D_MODEL), lay),
            pl.BlockSpec((None, D_MODEL, IN_WIDTH), lay),
            pl.BlockSpec((None, 1, A_WIDTH), lay),
            pl.BlockSpec((None, A_GROUPS, CHUNK, CHUNK), lambda i: (l, 0, 0, 0)),
            pl.BlockSpec((None, CHUNK, A_GROUPS), lay),
            pl.BlockSpec((None, 1, A_WIDTH), lay),
        ],
        out_specs=[
            pl.BlockSpec((TM_IN, SB_HEADS * LANES), row),
            pl.BlockSpec((TM_IN, SB_WIDTH), row),
            pl.BlockSpec((TM_IN, SB_HEADS * LANES), row),
            pl.BlockSpec((TM_IN, A_WIDTH), row),
        ],
        out_shape=[
            jax.ShapeDtypeStruct((T, SB_HEADS * LANES), BF16),
            jax.ShapeDtypeStruct((T, SB_WIDTH), BF16),
            jax.ShapeDtypeStruct((T, SB_HEADS * LANES), BF16),
            jax.ShapeDtypeStruct((T, A_WIDTH), BF16),
        ],
        compiler_params=pltpu.CompilerParams(dimension_semantics=("arbitrary",), vmem_limit_bytes=VMEM_LIMIT),
        name="mix_in",
    )(x2d, g, w_in, vg, ws, bs_t, ga)


def _stick_break_kernel(q_ref, k_ref, v_ref, o_ref):
    qi = pl.program_id(2)
    q = q_ref[...]
    r_idx = lax.broadcasted_iota(jnp.int32, (TQ, TK), 0)
    c_idx = lax.broadcasted_iota(jnp.int32, (TQ, TK), 1)
    strict = c_idx < r_idx
    later = jnp.where(c_idx < r_idx, -1.0, 0.0).astype(BF16)

    def block(j, carry, masked):
        acc, c0, c1 = carry
        start = pl.multiple_of(j * TK, TK)
        kb = k_ref[pl.ds(start, TK), :]
        vb = v_ref[pl.ds(start, TK), :]
        cs = (c0, c1)
        new_c = []
        for hd in range(2):
            z = _dot_nt(q[:, hd * LANES:(hd + 1) * LANES], kb)
            lg = jnp.log(1.0 + jnp.exp(-jnp.abs(z)))
            sp = jnp.maximum(z, 0.0) + lg
            lsz = jnp.minimum(z, 0.0) - lg
            if masked:
                sp = jnp.where(strict, sp, 0.0)
            sp_hi, sp_lo = _split_bf16(sp)
            after = _dot(sp_hi, later) + _dot(sp_lo, later)
            p = jnp.exp(lsz + after)
            if masked:
                p = jnp.where(strict, p, 0.0)
            ob = _dot(p.astype(BF16), vb[:, hd * LANES:(hd + 1) * LANES])
            acc = acc + ob * jnp.exp(cs[hd])
            new_c.append(cs[hd] - jnp.sum(sp, axis=-1, keepdims=True))
        return acc, new_c[0], new_c[1]

    carry = (jnp.zeros((TQ, LANES), F32), jnp.zeros((TQ, 1), F32), jnp.zeros((TQ, 1), F32))
    carry = block(qi, carry, True)
    carry = lax.fori_loop(0, qi, lambda jj, c: block(qi - 1 - jj, c, False), carry)
    o_ref[...] = carry[0]


def _stick_break(q, k, v, B, S):
    grid = (B, SB_PAIRS, S // TQ)
    return pl.pallas_call(
        _stick_break_kernel,
        grid=grid,
        in_specs=[
            pl.BlockSpec((None, TQ, 2 * LANES), lambda b, p, i: (b, i, p)),
            pl.BlockSpec((None, S, LANES), lambda b, p, i: (b, 0, p)),
            pl.BlockSpec((None, S, 2 * LANES), lambda b, p, i: (b, 0, p)),
        ],
        out_specs=pl.BlockSpec((None, TQ, LANES), lambda b, p, i: (b, i, p)),
        out_shape=jax.ShapeDtypeStruct((B, S, SB_WIDTH), F32),
        compiler_params=pltpu.CompilerParams(
            dimension_semantics=("arbitrary", "arbitrary", "arbitrary"), vmem_limit_bytes=VMEM_LIMIT),
        name="stick_break",
    )(q.reshape(B, S, SB_HEADS * LANES), k.reshape(B, S, SB_WIDTH), v.reshape(B, S, SB_HEADS * LANES))


def _mem_kv_kernel(m_ref, g_ref, wk_ref, wv_ref, k_ref, v_ref):
    m = _rms(m_ref[...], g_ref[...]).astype(BF16)
    k_ref[...] = _dot(m, wk_ref[...]).astype(BF16)
    v_ref[...] = _dot(m, wv_ref[...]).astype(BF16)


def _mem_kv(mem, g, wk, wv):
    B, M, _ = mem.shape
    L = wk.shape[0]
    kv_spec = pl.BlockSpec((None, None, M, D_MODEL), lambda l, b: (l, b, 0, 0))
    w_spec = pl.BlockSpec((None, D_MODEL, D_MODEL), lambda l, b: (l, 0, 0))
    return pl.pallas_call(
        _mem_kv_kernel,
        grid=(L, B),
        in_specs=[
            pl.BlockSpec((None, M, D_MODEL), lambda l, b: (b, 0, 0)),
            pl.BlockSpec((None, 1, D_MODEL), lambda l, b: (l, 0, 0)),
            w_spec, w_spec,
        ],
        out_specs=[kv_spec, kv_spec],
        out_shape=[jax.ShapeDtypeStruct((L, B, M, D_MODEL), BF16)] * 2,
        compiler_params=pltpu.CompilerParams(
            dimension_semantics=("arbitrary", "arbitrary"), vmem_limit_bytes=VMEM_LIMIT),
        name="mem_kv",
    )(mem, g, wk, wv)


def _post_kernel(x_ref, ya_ref, yb_ref, gb_ref, wo_ref, gc_ref, wq_ref, kx_ref, vx_ref, wxo_ref, gm_ref,
                 wrh_ref, wrl_ref, br_ref, x2_ref, hm_ref, info_ref, cnt_ref, run_ref):
    tm = x_ref.shape[0]

    @pl.when(pl.program_id(0) == 0)
    def _():
        run_ref[...] = jnp.zeros_like(run_ref)

    ybn = _rms(yb_ref[...], gb_ref[...]).astype(BF16)
    x1 = x_ref[...] + _dot(jnp.concatenate([ya_ref[...], ybn], axis=-1), wo_ref[...])

    h = _rms(x1, gc_ref[...]).astype(BF16)
    q = (_dot(h, wq_ref[...]) * (1.0 / (X_HEAD_DIM ** 0.5))).astype(BF16)
    heads = []
    for hd in range(X_HEADS):
        cols = slice(hd * X_HEAD_DIM, (hd + 1) * X_HEAD_DIM)
        s = _dot_nt(q[:, cols], kx_ref[:, cols])
        p = jnp.exp(s - jnp.max(s, axis=-1, keepdims=True))
        p = p / jnp.sum(p, axis=-1, keepdims=True)
        heads.append(_dot(p.astype(BF16), vx_ref[:, cols]).astype(BF16))
    x2 = x1 + _dot(jnp.concatenate(heads, axis=-1), wxo_ref[...])
    x2_ref[...] = x2

    hm = _rms(x2, gm_ref[...])
    for c in range(ROW_TILES):
        hm_ref[pl.ds(c, tm, stride=ROW_TILES), :] = hm[:, c * LANES:(c + 1) * LANES]

    hm_hi, hm_lo = _split_bf16(hm)
    logits = _dot(hm_hi, wrh_ref[...]) + _dot(hm_hi, wrl_ref[...]) + _dot(hm_lo, wrh_ref[...]) + br_ref[...]

    lane = lax.broadcasted_iota(jnp.int32, (tm, LANES), 1)
    lanef = lane.astype(F32)
    big = float(LANES)
    is_group = lane < N_GROUPS
    gl = jnp.where(is_group, logits, NEG)
    gmax = jnp.max(gl, axis=-1, keepdims=True)
    gsel = jnp.min(jnp.where(gl == gmax, lanef, big), axis=-1, keepdims=True)
    gden = jnp.sum(jnp.where(is_group, jnp.exp(gl - gmax), 0.0), axis=-1, keepdims=True)
    g_gate = 1.0 / gden
    lo = ROUTE_BASE + EXPERTS_PER_GROUP * gsel
    in_group = (lanef >= lo) & (lanef < lo + EXPERTS_PER_GROUP)
    el = jnp.where(in_group, logits, NEG)
    v1 = jnp.max(el, axis=-1, keepdims=True)
    i1 = jnp.min(jnp.where(el == v1, lanef, big), axis=-1, keepdims=True)
    el2 = jnp.where(lanef == i1, NEG, el)
    v2 = jnp.max(el2, axis=-1, keepdims=True)
    i2 = jnp.min(jnp.where(el2 == v2, lanef, big), axis=-1, keepdims=True)
    t = jnp.exp(v2 - v1)
    den = 1.0 + t
    gate1 = g_gate * (1.0 / den)
    gate2 = g_gate * (t / den)

    hit1 = lanef == i1
    hit2 = lanef == i2
    multi = jnp.where(hit1 | hit2, 1.0, 0.0)
    r_idx = lax.broadcasted_iota(jnp.int32, (tm, tm), 0)
    c_idx = lax.broadcasted_iota(jnp.int32, (tm, tm), 1)
    earlier = jnp.where(c_idx < r_idx, 1.0, 0.0).astype(BF16)
    before = _dot(earlier, multi.astype(BF16)) + run_ref[0:1, :]
    rank1 = jnp.sum(jnp.where(hit1, before, 0.0), axis=-1, keepdims=True)
    rank2 = jnp.sum(jnp.where(hit2, before, 0.0), axis=-1, keepdims=True)
    run_ref[...] = run_ref[...] + jnp.sum(multi, axis=0, keepdims=True)
    cnt_ref[...] = run_ref[...]

    info = jnp.where(lane == 0, i1 - ROUTE_BASE, 0.0)
    info = jnp.where(lane == 1, i2 - ROUTE_BASE, info)
    info = jnp.where(lane == 2, gate1, info)
    info = jnp.where(lane == 3, gate2, info)
    info = jnp.where(lane == 4, rank1, info)
    info = jnp.where(lane == 5, rank2, info)
    info_ref[...] = info


def _post(x2d, ya, yb, gb, wo, gc, wq, kx, vx, wxo, gm, wrh, wrl, br, l, S):
    T = x2d.shape[0]
    M = kx.shape[2]
    grid = (T // TM_POST,)
    row = lambda i: (i, 0)
    lay = lambda i: (l, 0, 0)
    batch = lambda i: (l, (i * TM_POST) // S, 0, 0)
    wspec = pl.BlockSpec((None, D_MODEL, D_MODEL), lay)
    gspec = pl.BlockSpec((None, 1, D_MODEL), lay)
    return pl.pallas_call(
        _post_kernel,
        grid=grid,
        in_specs=[
            pl.BlockSpec((TM_POST, D_MODEL), row),
            pl.BlockSpec((TM_POST, A_WIDTH), row),
            pl.BlockSpec((TM_POST, SB_WIDTH), row),
            pl.BlockSpec((None, 1, SB_WIDTH), lay),
            wspec, gspec, wspec,
            pl.BlockSpec((None, None, M, D_MODEL), batch),
            pl.BlockSpec((None, None, M, D_MODEL), batch),
            wspec, gspec,
            pl.BlockSpec((None, D_MODEL, LANES), lay),
            pl.BlockSpec((None, D_MODEL, LANES), lay),
            pl.BlockSpec((None, 1, LANES), lay),
        ],
        out_specs=[
            pl.BlockSpec((TM_POST, D_MODEL), row),
            pl.BlockSpec((TM_POST * ROW_TILES, LANES), row),
            pl.BlockSpec((TM_POST, LANES), row),
            pl.BlockSpec((SUBLANES, LANES), lambda i: (0, 0)),
        ],
        out_shape=[
            jax.ShapeDtypeStruct((T, D_MODEL), F32),
            jax.ShapeDtypeStruct((T * ROW_TILES, LANES), F32),
            jax.ShapeDtypeStruct((T, LANES), F32),
            jax.ShapeDtypeStruct((SUBLANES, LANES), F32),
        ],
        scratch_shapes=[pltpu.VMEM((SUBLANES, LANES), F32)],
        compiler_params=pltpu.CompilerParams(dimension_semantics=("arbitrary",), vmem_limit_bytes=VMEM_LIMIT),
        name="post",
    )(x2d, ya, yb, gb, wo, gc, wq, kx, vx, wxo, gm, wrh, wrl, br)


def _row_copy(src, src_row, dst, dst_row, sem):
    return pltpu.make_async_copy(
        src.at[pl.ds(pl.multiple_of(src_row * ROW_TILES, ROW_TILES), ROW_TILES), :],
        dst.at[pl.ds(pl.multiple_of(dst_row * ROW_TILES, ROW_TILES), ROW_TILES), :],
        sem)


def _dispatch_kernel(dest_ref, hm_ref, buf_in_ref, buf_ref, sem):
    del buf_in_ref
    base = pl.program_id(0) * (TD * TOP_K)

    def issue(r, c):
        for k in range(TOP_K):
            _row_copy(hm_ref, r, buf_ref, dest_ref[base + TOP_K * r + k], sem).start()
        return c

    lax.fori_loop(0, TD, issue, 0)

    def drain(r, c):
        _row_copy(hm_ref, 0, buf_ref, 0, sem).wait()
        return c

    lax.fori_loop(0, TD * TOP_K, drain, 0)


def _dispatch(dest, hm, buf0):
    T = hm.shape[0] // ROW_TILES
    return pl.pallas_call(
        _dispatch_kernel,
        grid_spec=pltpu.PrefetchScalarGridSpec(
            num_scalar_prefetch=1,
            grid=(T // TD,),
            in_specs=[
                pl.BlockSpec((TD * ROW_TILES, LANES), lambda i, d: (i, 0)),
                pl.BlockSpec(memory_space=pl.ANY),
            ],
            out_specs=pl.BlockSpec(memory_space=pl.ANY),
            scratch_shapes=[pltpu.SemaphoreType.DMA(())],
        ),
        out_shape=jax.ShapeDtypeStruct(buf0.shape, F32),
        input_output_aliases={2: 0},
        compiler_params=pltpu.CompilerParams(dimension_semantics=("arbitrary",), vmem_limit_bytes=VMEM_LIMIT),
        name="dispatch",
    )(dest, hm, buf0)


def _experts_kernel(blk_e_ref, n_used_ref, buf_ref, w1_ref, w3_ref, w2_ref, y_ref):
    del blk_e_ref
    i = pl.program_id(0)

    @pl.when(i < n_used_ref[0])
    def _():
        xb = jnp.concatenate(
            [buf_ref[pl.ds(c, BM, stride=ROW_TILES), :] for c in range(ROW_TILES)], axis=-1).astype(BF16)
        h1 = _dot(xb, w1_ref[...])
        h3 = _dot(xb, w3_ref[...])
        a = (h1 * jax.nn.sigmoid(h1) * h3).astype(BF16)
        y = _dot(a, w2_ref[...])
        for c in range(ROW_TILES):
            y_ref[pl.ds(c, BM, stride=ROW_TILES), :] = y[:, c * LANES:(c + 1) * LANES]

    @pl.when(i >= n_used_ref[0])
    def _():
        y_ref[...] = jnp.zeros_like(y_ref)


def _experts(blk_e, n_used, buf, w1, w3, w2, l):
    n_blocks = buf.shape[0] // (BM * ROW_TILES)
    rows = pl.BlockSpec((BM * ROW_TILES, LANES), lambda i, e, n: (i, 0))
    return pl.pallas_call(
        _experts_kernel,
        grid_spec=pltpu.PrefetchScalarGridSpec(
            num_scalar_prefetch=2,
            grid=(n_blocks,),
            in_specs=[
                rows,
                pl.BlockSpec((None, None, D_MODEL, D_EXPERT), lambda i, e, n: (l, e[i], 0, 0)),
                pl.BlockSpec((None, None, D_MODEL, D_EXPERT), lambda i, e, n: (l, e[i], 0, 0)),
                pl.BlockSpec((None, None, D_EXPERT, D_MODEL), lambda i, e, n: (l, e[i], 0, 0)),
            ],
            out_specs=rows,
        ),
        out_shape=jax.ShapeDtypeStruct(buf.shape, F32),
        compiler_params=pltpu.CompilerParams(dimension_semantics=("arbitrary",), vmem_limit_bytes=VMEM_LIMIT),
        name="experts",
    )(blk_e, n_used, buf, w1, w3, w2)


def _combine_kernel(dest_ref, x2_ref, info_ref, gf_ref, ybuf_ref, out_ref, rows_ref, sem, *, final):
    base = pl.program_id(0) * (TE * TOP_K)

    def issue(r, c):
        for k in range(TOP_K):
            _row_copy(ybuf_ref, dest_ref[base + TOP_K * r + k], rows_ref, k * TE + r, sem).start()
        return c

    lax.fori_loop(0, TE, issue, 0)

    def drain(r, c):
        _row_copy(ybuf_ref, 0, rows_ref, 0, sem).wait()
        return c

    lax.fori_loop(0, TE * TOP_K, drain, 0)

    info = info_ref[...]
    gate1 = info[:, 2:3]
    gate2 = info[:, 3:4]
    parts = []
    ssq = jnp.zeros((TE, 1), F32)
    for c in range(ROW_TILES):
        y1 = rows_ref[pl.ds(c, TE, stride=ROW_TILES), :]
        y2 = rows_ref[pl.ds(TE * ROW_TILES + c, TE, stride=ROW_TILES), :]
        xc = x2_ref[:, c * LANES:(c + 1) * LANES] + (gate1 * y1 + gate2 * y2)
        if final:
            parts.append(xc)
            ssq = ssq + jnp.sum(xc * xc, axis=-1, keepdims=True)
        else:
            out_ref[:, c * LANES:(c + 1) * LANES] = xc
    if final:
        inv = lax.rsqrt(ssq * (1.0 / D_MODEL) + EPS)
        for c in range(ROW_TILES):
            lanes = slice(c * LANES, (c + 1) * LANES)
            out_ref[:, lanes] = parts[c] * inv * gf_ref[:, lanes]


def _combine(dest, x2, info, gf, ybuf, final):
    T = x2.shape[0]
    row = lambda i, d: (i, 0)
    return pl.pallas_call(
        functools.partial(_combine_kernel, final=final),
        grid_spec=pltpu.PrefetchScalarGridSpec(
            num_scalar_prefetch=1,
            grid=(T // TE,),
            in_specs=[
                pl.BlockSpec((TE, D_MODEL), row),
                pl.BlockSpec((TE, LANES), row),
                pl.BlockSpec((1, D_MODEL), lambda i, d: (0, 0)),
                pl.BlockSpec(memory_space=pl.ANY),
            ],
            out_specs=pl.BlockSpec((TE, D_MODEL), row),
            scratch_shapes=[
                pltpu.VMEM((TOP_K * TE * ROW_TILES, LANES), F32),
                pltpu.SemaphoreType.DMA(()),
            ],
        ),
        out_shape=jax.ShapeDtypeStruct((T, D_MODEL), F32),
        compiler_params=pltpu.CompilerParams(dimension_semantics=("arbitrary",), vmem_limit_bytes=VMEM_LIMIT),
        name="combine_final" if final else "combine",
    )(dest, x2, info, gf, ybuf)


def kernel(x, mem, norm_mix, w_in, v_norm, w_spatial, b_spatial, out_norm_a, out_norm_b, w_out, norm_cross, norm_mem, w_xq, w_xk, w_xv, w_xo, norm_moe, w_group, b_group, w_router, b_router, w1, w3, w2, norm_final):
    B, S, D = x.shape
    L = w_in.shape[0]
    T = B * S
    assert D == D_MODEL and S % TQ == 0 and T % TM_IN == 0 and S % TM_POST == 0 and T % TD == 0 and T % TE == 0

    row3 = lambda a: a.reshape(L, 1, -1)
    w_in_b, w_out_b = w_in.astype(BF16), w_out.astype(BF16)
    w_xq_b, w_xk_b, w_xv_b, w_xo_b = (w.astype(BF16) for w in (w_xq, w_xk, w_xv, w_xo))
    w1_b, w3_b, w2_b = w1.astype(BF16), w3.astype(BF16), w2.astype(BF16)
    pad = jnp.zeros((L, D, LANES - N_GROUPS - N_EXPERTS), F32)
    w_route = jnp.concatenate([w_group, w_router, pad], axis=-1)
    w_route_hi = w_route.astype(BF16)
    w_route_lo = (w_route - w_route_hi.astype(F32)).astype(BF16)
    b_route = jnp.concatenate([b_group, b_router, pad[:, 0, :]], axis=-1).reshape(L, 1, LANES)
    bs_t = jnp.swapaxes(b_spatial, 1, 2)

    kx, vx = _mem_kv(mem, row3(norm_mem), w_xk_b, w_xv_b)

    n_slots = T * TOP_K + N_EXPERTS * BM
    n_blocks = n_slots // BM
    xs = x.reshape(T, D)
    for l in range(L):
        q, k, v, ya = _mix_in(xs, row3(norm_mix), w_in_b, row3(v_norm), w_spatial, bs_t, row3(out_norm_a), l)
        yb = _stick_break(q, k, v, B, S).reshape(T, SB_WIDTH)
        x2, hm, info, cnt = _post(xs, ya, yb, row3(out_norm_b), w_out_b, row3(norm_cross), w_xq_b, kx, vx, w_xo_b,
                                  row3(norm_moe), w_route_hi, w_route_lo, b_route, l, S)
        eid = info[:, 0:TOP_K].astype(jnp.int32)
        rank = info[:, 4:4 + TOP_K].astype(jnp.int32)
        counts = cnt[0, ROUTE_BASE:ROUTE_BASE + N_EXPERTS].astype(jnp.int32)
        padded = ((counts + BM - 1) // BM) * BM
        seg_end = jnp.cumsum(padded)
        seg_start = seg_end - padded
        dest = (seg_start[eid] + rank).reshape(-1)
        blk_row = jnp.arange(n_blocks, dtype=jnp.int32) * BM
        blk_e = jnp.minimum(
            jnp.sum((seg_end[None, :] <= blk_row[:, None]).astype(jnp.int32), axis=1), N_EXPERTS - 1)
        n_used = (seg_end[-1:] // BM).astype(jnp.int32)
        buf = _dispatch(dest, hm, jnp.zeros((n_slots * ROW_TILES, LANES), F32))
        ybuf = _experts(blk_e, n_used, buf, w1_b, w3_b, w2_b, l)
        xs = _combine(dest, x2, info, norm_final.reshape(1, D), ybuf, final=(l == L - 1))
    return xs.reshape(B, S, D)
```

```python
import functools

import jax
import jax.numpy as jnp
from jax import lax
from jax.experimental import pallas as pl
from jax.experimental.pallas import tpu as pltpu

F32 = jnp.float32
BF16 = jnp.bfloat16

D_MODEL = 1024
A_WIDTH = 512
A_GROUPS = 4
A_CH = 128
CHUNK = 128
SB_WIDTH = 512
SB_HEADS = 8
SB_HEAD_DIM = 64
SB_PAIRS = SB_HEADS // 2
IN_WIDTH = 2 * A_WIDTH + 3 * SB_WIDTH
X_HEADS = 4
X_HEAD_DIM = 256
N_GROUPS = 4
EXPERTS_PER_GROUP = 8
N_EXPERTS = 32
TOP_K = 2
D_EXPERT = 512
EPS = 1e-6

LANES = 128
SUBLANES = 8
ROW_TILES = D_MODEL // LANES

TM_IN = 512
TQ = 256
TK = 256
TM_POST = 256
TD = 512
BM = 256
TE = 256
ROUTE_BASE = N_GROUPS
NEG = -1e30

VMEM_LIMIT = 52 * 1024 * 1024


def _rms(x, g):
    return x * lax.rsqrt(jnp.mean(x * x, axis=-1, keepdims=True) + EPS) * g


def _dot(a, b):
    return jnp.dot(a, b, preferred_element_type=F32)


def _dot_nt(a, b):
    return lax.dot_general(a, b, (((1,), (1,)), ((), ())), preferred_element_type=F32)


def _split_bf16(x):
    hi = x.astype(BF16)
    lo = (x - hi.astype(F32)).astype(BF16)
    return hi, lo


def _mix_in_kernel(x_ref, g_ref, w_ref, vg_ref, ws_ref, bs_ref, ga_ref, q_ref, k_ref, v_ref, ya_ref):
    tm = x_ref.shape[0]
    h = _rms(x_ref[...], g_ref[...]).astype(BF16)
    z = _dot(h, w_ref[...])

    t_idx = lax.broadcasted_iota(jnp.int32, (CHUNK, CHUNK), 0)
    s_idx = lax.broadcasted_iota(jnp.int32, (CHUNK, CHUNK), 1)
    causal = s_idx <= t_idx
    parts = []
    ssq = jnp.zeros((tm, 1), F32)
    for g in range(A_GROUPS):
        lanes = slice(g * A_CH, (g + 1) * A_CH)
        u = jax.nn.gelu(z[:, g * A_CH:(g + 1) * A_CH])
        vg = jax.nn.gelu(z[:, A_WIDTH + g * A_CH:A_WIDTH + (g + 1) * A_CH])
        vn = _rms(vg, vg_ref[:, lanes]).astype(BF16)
        ws = jnp.where(causal, ws_ref[g], 0.0).astype(BF16)
        bias = bs_ref[:, g:g + 1]
        mixed = jnp.concatenate(
            [_dot(ws, vn[c * CHUNK:(c + 1) * CHUNK, :]) + bias for c in range(tm // CHUNK)], axis=0)
        ya = u * mixed
        parts.append(ya)
        ssq = ssq + jnp.sum(ya * ya, axis=-1, keepdims=True)
    inv = lax.rsqrt(ssq * (1.0 / A_WIDTH) + EPS)
    for g in range(A_GROUPS):
        lanes = slice(g * A_CH, (g + 1) * A_CH)
        ya_ref[:, lanes] = (parts[g] * inv * ga_ref[:, lanes]).astype(BF16)

    lane = lax.broadcasted_iota(jnp.int32, (1, LANES), 1)
    q0 = 2 * A_WIDTH
    k0 = q0 + SB_WIDTH
    v0 = k0 + SB_WIDTH
    scale = 1.0 / (SB_HEAD_DIM ** 0.5)
    for hd in range(SB_HEADS):
        pair = hd // 2
        keep = (lane < SB_HEAD_DIM) if hd % 2 == 0 else (lane >= SB_HEAD_DIM)
        qs = z[:, q0 + pair * LANES:q0 + (pair + 1) * LANES] * scale
        vs = z[:, v0 + pair * LANES:v0 + (pair + 1) * LANES]
        q_ref[:, hd * LANES:(hd + 1) * LANES] = jnp.where(keep, qs, 0.0).astype(BF16)
        v_ref[:, hd * LANES:(hd + 1) * LANES] = jnp.where(keep, vs, 0.0).astype(BF16)
    k_ref[...] = z[:, k0:v0].astype(BF16)


def _mix_in(x2d, g, w_in, vg, ws, bs_t, ga, l):
    T = x2d.shape[0]
    grid = (T // TM_IN,)
    row = lambda i: (i, 0)
    lay = lambda i: (l, 0, 0)
    return pl.pallas_call(
        _mix_in_kernel,
        grid=grid,
        in_specs=[
            pl.BlockSpec((TM_IN, D_MODEL), row),
            pl.BlockSpec((None, 1, D_MODEL), lay),
            pl.BlockSpec((None, D_MODEL, IN_WIDTH), lay),
            pl.BlockSpec((None, 1, A_WIDTH), lay),
            pl.BlockSpec((None, A_GROUPS, CHUNK, CHUNK), lambda i: (l, 0, 0, 0)),
            pl.BlockSpec((None, CHUNK, A_GROUPS), lay),
            pl.BlockSpec((None, 1, A_WIDTH), lay),
        ],
        out_specs=[
            pl.BlockSpec((TM_IN, SB_HEADS * LANES), row),
            pl.BlockSpec((TM_IN, SB_WIDTH), row),
            pl.BlockSpec((TM_IN, SB_HEADS * LANES), row),
            pl.BlockSpec((TM_IN, A_WIDTH), row),
        ],
        out_shape=[
            jax.ShapeDtypeStruct((T, SB_HEADS * LANES), BF16),
            jax.ShapeDtypeStruct((T, SB_WIDTH), BF16),
            jax.ShapeDtypeStruct((T, SB_HEADS * LANES), BF16),
            jax.ShapeDtypeStruct((T, A_WIDTH), BF16),
        ],
        compiler_params=pltpu.CompilerParams(dimension_semantics=("arbitrary",), vmem_limit_bytes=VMEM_LIMIT),
        name="mix_in",
    )(x2d, g, w_in, vg, ws, bs_t, ga)


def _stick_break_kernel(q_ref, k_ref, v_ref, o_ref):
    qi = pl.program_id(2)
    q = q_ref[...]
    r_idx = lax.broadcasted_iota(jnp.int32, (TQ, TK), 0)
    c_idx = lax.broadcasted_iota(jnp.int32, (TQ, TK), 1)
    strict = c_idx < r_idx
    later = jnp.where(c_idx < r_idx, -1.0, 0.0).astype(BF16)

    def block(j, carry, masked):
        acc, c0, c1 = carry
        start = pl.multiple_of(j * TK, TK)
        kb = k_ref[pl.ds(start, TK), :]
        vb = v_ref[pl.ds(start, TK), :]
        cs = (c0, c1)
        new_c = []
        for hd in range(2):
            z = _dot_nt(q[:, hd * LANES:(hd + 1) * LANES], kb)
            lg = jnp.log(1.0 + jnp.exp(-jnp.abs(z)))
            sp = jnp.maximum(z, 0.0) + lg
            lsz = jnp.minimum(z, 0.0) - lg
            if masked:
                sp = jnp.where(strict, sp, 0.0)
            sp_hi, sp_lo = _split_bf16(sp)
            after = _dot(sp_hi, later) + _dot(sp_lo, later)
            p = jnp.exp(lsz + after)
            if masked:
                p = jnp.where(strict, p, 0.0)
            ob = _dot(p.astype(BF16), vb[:, hd * LANES:(hd + 1) * LANES])
            acc = acc + ob * jnp.exp(cs[hd])
            new_c.append(cs[hd] - jnp.sum(sp, axis=-1, keepdims=True))
        return acc, new_c[0], new_c[1]

    def alive(c0, c1):
        return (jnp.max(jnp.maximum(jnp.exp(c0), jnp.exp(c1))) > 0.0).astype(jnp.int32)

    def cond(state):
        jj, live = state[0], state[1]
        return jnp.logical_and(jj < qi, live > 0)

    def body(state):
        jj, _, acc, c0, c1 = state
        acc, c0, c1 = block(qi - 1 - jj, (acc, c0, c1), False)
        return jj + 1, alive(c0, c1), acc, c0, c1

    carry = (jnp.zeros((TQ, LANES), F32), jnp.zeros((TQ, 1), F32), jnp.zeros((TQ, 1), F32))
    acc, c0, c1 = block(qi, carry, True)
    state = lax.while_loop(cond, body, (jnp.int32(0), alive(c0, c1), acc, c0, c1))
    o_ref[...] = state[2]


def _stick_break(q, k, v, B, S):
    grid = (B, SB_PAIRS, S // TQ)
    return pl.pallas_call(
        _stick_break_kernel,
        grid=grid,
        in_specs=[
            pl.BlockSpec((None, TQ, 2 * LANES), lambda b, p, i: (b, i, p)),
            pl.BlockSpec((None, S, LANES), lambda b, p, i: (b, 0, p)),
            pl.BlockSpec((None, S, 2 * LANES), lambda b, p, i: (b, 0, p)),
        ],
        out_specs=pl.BlockSpec((None, TQ, LANES), lambda b, p, i: (b, i, p)),
        out_shape=jax.ShapeDtypeStruct((B, S, SB_WIDTH), F32),
        compiler_params=pltpu.CompilerParams(
            dimension_semantics=("arbitrary", "arbitrary", "arbitrary"), vmem_limit_bytes=VMEM_LIMIT),
        name="stick_break",
    )(q.reshape(B, S, SB_HEADS * LANES), k.reshape(B, S, SB_WIDTH), v.reshape(B, S, SB_HEADS * LANES))


def _mem_kv_kernel(m_ref, g_ref, wk_ref, wv_ref, k_ref, v_ref):
    m = _rms(m_ref[...], g_ref[...]).astype(BF16)
    k_ref[...] = _dot(m, wk_ref[...]).astype(BF16)
    v_ref[...] = _dot(m, wv_ref[...]).astype(BF16)


def _mem_kv(mem, g, wk, wv):
    B, M, _ = mem.shape
    L = wk.shape[0]
    kv_spec = pl.BlockSpec((None, None, M, D_MODEL), lambda l, b: (l, b, 0, 0))
    w_spec = pl.BlockSpec((None, D_MODEL, D_MODEL), lambda l, b: (l, 0, 0))
    return pl.pallas_call(
        _mem_kv_kernel,
        grid=(L, B),
        in_specs=[
            pl.BlockSpec((None, M, D_MODEL), lambda l, b: (b, 0, 0)),
            pl.BlockSpec((None, 1, D_MODEL), lambda l, b: (l, 0, 0)),
            w_spec, w_spec,
        ],
        out_specs=[kv_spec, kv_spec],
        out_shape=[jax.ShapeDtypeStruct((L, B, M, D_MODEL), BF16)] * 2,
        compiler_params=pltpu.CompilerParams(
            dimension_semantics=("arbitrary", "arbitrary"), vmem_limit_bytes=VMEM_LIMIT),
        name="mem_kv",
    )(mem, g, wk, wv)


def _post_kernel(x_ref, ya_ref, yb_ref, gb_ref, wo_ref, gc_ref, wq_ref, kx_ref, vx_ref, wxo_ref, gm_ref,
                 wrh_ref, wrl_ref, br_ref, x2_ref, hm_ref, info_ref, cnt_ref, run_ref):
    tm = x_ref.shape[0]

    @pl.when(pl.program_id(0) == 0)
    def _():
        run_ref[...] = jnp.zeros_like(run_ref)

    ybn = _rms(yb_ref[...], gb_ref[...]).astype(BF16)
    x1 = x_ref[...] + _dot(jnp.concatenate([ya_ref[...], ybn], axis=-1), wo_ref[...])

    h = _rms(x1, gc_ref[...]).astype(BF16)
    q = (_dot(h, wq_ref[...]) * (1.0 / (X_HEAD_DIM ** 0.5))).astype(BF16)
    heads = []
    for hd in range(X_HEADS):
        cols = slice(hd * X_HEAD_DIM, (hd + 1) * X_HEAD_DIM)
        s = _dot_nt(q[:, cols], kx_ref[:, cols])
        p = jnp.exp(s - jnp.max(s, axis=-1, keepdims=True))
        p = p / jnp.sum(p, axis=-1, keepdims=True)
        heads.append(_dot(p.astype(BF16), vx_ref[:, cols]).astype(BF16))
    x2 = x1 + _dot(jnp.concatenate(heads, axis=-1), wxo_ref[...])
    x2_ref[...] = x2

    hm = _rms(x2, gm_ref[...])
    for c in range(ROW_TILES):
        hm_ref[pl.ds(c, tm, stride=ROW_TILES), :] = hm[:, c * LANES:(c + 1) * LANES]

    hm_hi, hm_lo = _split_bf16(hm)
    logits = _dot(hm_hi, wrh_ref[...]) + _dot(hm_hi, wrl_ref[...]) + _dot(hm_lo, wrh_ref[...]) + br_ref[...]

    lane = lax.broadcasted_iota(jnp.int32, (tm, LANES), 1)
    lanef = lane.astype(F32)
    big = float(LANES)
    is_group = lane < N_GROUPS
    gl = jnp.where(is_group, logits, NEG)
    gmax = jnp.max(gl, axis=-1, keepdims=True)
    gsel = jnp.min(jnp.where(gl == gmax, lanef, big), axis=-1, keepdims=True)
    gden = jnp.sum(jnp.where(is_group, jnp.exp(gl - gmax), 0.0), axis=-1, keepdims=True)
    g_gate = 1.0 / gden
    lo = ROUTE_BASE + EXPERTS_PER_GROUP * gsel
    in_group = (lanef >= lo) & (lanef < lo + EXPERTS_PER_GROUP)
    el = jnp.where(in_group, logits, NEG)
    v1 = jnp.max(el, axis=-1, keepdims=True)
    i1 = jnp.min(jnp.where(el == v1, lanef, big), axis=-1, keepdims=True)
    el2 = jnp.where(lanef == i1, NEG, el)
    v2 = jnp.max(el2, axis=-1, keepdims=True)
    i2 = jnp.min(jnp.where(el2 == v2, lanef, big), axis=-1, keepdims=True)
    t = jnp.exp(v2 - v1)
    den = 1.0 + t
    gate1 = g_gate * (1.0 / den)
    gate2 = g_gate * (t / den)

    hit1 = lanef == i1
    hit2 = lanef == i2
    multi = jnp.where(hit1 | hit2, 1.0, 0.0)
    r_idx = lax.broadcasted_iota(jnp.int32, (tm, tm), 0)
    c_idx = lax.broadcasted_iota(jnp.int32, (tm, tm), 1)
    earlier = jnp.where(c_idx < r_idx, 1.0, 0.0).astype(BF16)
    before = _dot(earlier, multi.astype(BF16)) + run_ref[0:1, :]
    rank1 = jnp.sum(jnp.where(hit1, before, 0.0), axis=-1, keepdims=True)
    rank2 = jnp.sum(jnp.where(hit2, before, 0.0), axis=-1, keepdims=True)
    run_ref[...] = run_ref[...] + jnp.sum(multi, axis=0, keepdims=True)
    cnt_ref[...] = run_ref[...]

    info = jnp.where(lane == 0, i1 - ROUTE_BASE, 0.0)
    info = jnp.where(lane == 1, i2 - ROUTE_BASE, info)
    info = jnp.where(lane == 2, gate1, info)
    info = jnp.where(lane == 3, gate2, info)
    info = jnp.where(lane == 4, rank1, info)
    info = jnp.where(lane == 5, rank2, info)
    info_ref[...] = info


def _post(x2d, ya, yb, gb, wo, gc, wq, kx, vx, wxo, gm, wrh, wrl, br, l, S):
    T = x2d.shape[0]
    M = kx.shape[2]
    grid = (T // TM_POST,)
    row = lambda i: (i, 0)
    lay = lambda i: (l, 0, 0)
    batch = lambda i: (l, (i * TM_POST) // S, 0, 0)
    wspec = pl.BlockSpec((None, D_MODEL, D_MODEL), lay)
    gspec = pl.BlockSpec((None, 1, D_MODEL), lay)
    return pl.pallas_call(
        _post_kernel,
        grid=grid,
        in_specs=[
            pl.BlockSpec((TM_POST, D_MODEL), row),
            pl.BlockSpec((TM_POST, A_WIDTH), row),
            pl.BlockSpec((TM_POST, SB_WIDTH), row),
            pl.BlockSpec((None, 1, SB_WIDTH), lay),
            wspec, gspec, wspec,
            pl.BlockSpec((None, None, M, D_MODEL), batch),
            pl.BlockSpec((None, None, M, D_MODEL), batch),
            wspec, gspec,
            pl.BlockSpec((None, D_MODEL, LANES), lay),
            pl.BlockSpec((None, D_MODEL, LANES), lay),
            pl.BlockSpec((None, 1, LANES), lay),
        ],
        out_specs=[
            pl.BlockSpec((TM_POST, D_MODEL), row),
            pl.BlockSpec((TM_POST * ROW_TILES, LANES), row),
            pl.BlockSpec((TM_POST, LANES), row),
            pl.BlockSpec((SUBLANES, LANES), lambda i: (0, 0)),
        ],
        out_shape=[
            jax.ShapeDtypeStruct((T, D_MODEL), F32),
            jax.ShapeDtypeStruct((T * ROW_TILES, LANES), F32),
            jax.ShapeDtypeStruct((T, LANES), F32),
            jax.ShapeDtypeStruct((SUBLANES, LANES), F32),
        ],
        scratch_shapes=[pltpu.VMEM((SUBLANES, LANES), F32)],
        compiler_params=pltpu.CompilerParams(dimension_semantics=("arbitrary",), vmem_limit_bytes=VMEM_LIMIT),
        name="post",
    )(x2d, ya, yb, gb, wo, gc, wq, kx, vx, wxo, gm, wrh, wrl, br)


def _row_copy(src, src_row, dst, dst_row, sem):
    return pltpu.make_async_copy(
        src.at[pl.ds(pl.multiple_of(src_row * ROW_TILES, ROW_TILES), ROW_TILES), :],
        dst.at[pl.ds(pl.multiple_of(dst_row * ROW_TILES, ROW_TILES), ROW_TILES), :],
        sem)


def _dispatch_kernel(dest_ref, hm_ref, buf_in_ref, buf_ref, sem):
    del buf_in_ref
    base = pl.program_id(0) * (TD * TOP_K)

    def issue(r, c):
        for k in range(TOP_K):
            _row_copy(hm_ref, r, buf_ref, dest_ref[base + TOP_K * r + k], sem).start(priority=k)
        return c

    lax.fori_loop(0, TD, issue, 0)
    for k in range(TOP_K):
        pltpu.make_async_copy(hm_ref, buf_ref.at[pl.ds(0, TD * ROW_TILES), :], sem).wait()


def _dispatch(dest, hm, buf0):
    T = hm.shape[0] // ROW_TILES
    return pl.pallas_call(
        _dispatch_kernel,
        grid_spec=pltpu.PrefetchScalarGridSpec(
            num_scalar_prefetch=1,
            grid=(T // TD,),
            in_specs=[
                pl.BlockSpec((TD * ROW_TILES, LANES), lambda i, d: (i, 0)),
                pl.BlockSpec(memory_space=pl.ANY),
            ],
            out_specs=pl.BlockSpec(memory_space=pl.ANY),
            scratch_shapes=[pltpu.SemaphoreType.DMA(())],
        ),
        out_shape=jax.ShapeDtypeStruct(buf0.shape, F32),
        input_output_aliases={2: 0},
        compiler_params=pltpu.CompilerParams(dimension_semantics=("arbitrary",), vmem_limit_bytes=VMEM_LIMIT),
        name="dispatch",
    )(dest, hm, buf0)


def _experts_kernel(blk_e_ref, n_used_ref, buf_ref, w1_ref, w3_ref, w2_ref, y_ref, w1b_ref, w3b_ref, w2b_ref):
    i = pl.program_id(0)
    expert = blk_e_ref[i]
    previous = blk_e_ref[jnp.maximum(i - 1, 0)]

    @pl.when(jnp.logical_or(i == 0, expert != previous))
    def _():
        w1b_ref[...] = w1_ref[...].astype(BF16)
        w3b_ref[...] = w3_ref[...].astype(BF16)
        w2b_ref[...] = w2_ref[...].astype(BF16)

    @pl.when(i < n_used_ref[0])
    def _():
        xb = jnp.concatenate(
            [buf_ref[pl.ds(c, BM, stride=ROW_TILES), :] for c in range(ROW_TILES)], axis=-1).astype(BF16)
        h1 = _dot(xb, w1b_ref[...])
        h3 = _dot(xb, w3b_ref[...])
        a = (h1 * jax.nn.sigmoid(h1) * h3).astype(BF16)
        y = _dot(a, w2b_ref[...])
        for c in range(ROW_TILES):
            y_ref[pl.ds(c, BM, stride=ROW_TILES), :] = y[:, c * LANES:(c + 1) * LANES]

    @pl.when(i >= n_used_ref[0])
    def _():
        y_ref[...] = jnp.zeros_like(y_ref)


def _experts(blk_e, n_used, buf, w1, w3, w2, l):
    n_blocks = buf.shape[0] // (BM * ROW_TILES)
    rows = pl.BlockSpec((BM * ROW_TILES, LANES), lambda i, e, n: (i, 0))
    return pl.pallas_call(
        _experts_kernel,
        grid_spec=pltpu.PrefetchScalarGridSpec(
            num_scalar_prefetch=2,
            grid=(n_blocks,),
            in_specs=[
                rows,
                pl.BlockSpec((None, None, D_MODEL, D_EXPERT), lambda i, e, n: (l, e[i], 0, 0)),
                pl.BlockSpec((None, None, D_MODEL, D_EXPERT), lambda i, e, n: (l, e[i], 0, 0)),
                pl.BlockSpec((None, None, D_EXPERT, D_MODEL), lambda i, e, n: (l, e[i], 0, 0)),
            ],
            out_specs=rows,
            scratch_shapes=[
                pltpu.VMEM((D_MODEL, D_EXPERT), BF16),
                pltpu.VMEM((D_MODEL, D_EXPERT), BF16),
                pltpu.VMEM((D_EXPERT, D_MODEL), BF16),
            ],
        ),
        out_shape=jax.ShapeDtypeStruct(buf.shape, F32),
        compiler_params=pltpu.CompilerParams(dimension_semantics=("arbitrary",), vmem_limit_bytes=VMEM_LIMIT),
        name="experts",
    )(blk_e, n_used, buf, w1, w3, w2)


def _combine_kernel(dest_ref, x2_ref, info_ref, gf_ref, ybuf_ref, out_ref, rows_ref, sem, *, final):
    base = pl.program_id(0) * (TE * TOP_K)

    def issue(r, c):
        for k in range(TOP_K):
            _row_copy(ybuf_ref, dest_ref[base + TOP_K * r + k], rows_ref, k * TE + r, sem).start(priority=k)
        return c

    lax.fori_loop(0, TE, issue, 0)
    pltpu.make_async_copy(ybuf_ref.at[pl.ds(0, TOP_K * TE * ROW_TILES), :], rows_ref, sem).wait()

    info = info_ref[...]
    gate1 = info[:, 2:3]
    gate2 = info[:, 3:4]
    parts = []
    ssq = jnp.zeros((TE, 1), F32)
    for c in range(ROW_TILES):
        y1 = rows_ref[pl.ds(c, TE, stride=ROW_TILES), :]
        y2 = rows_ref[pl.ds(TE * ROW_TILES + c, TE, stride=ROW_TILES), :]
        xc = x2_ref[:, c * LANES:(c + 1) * LANES] + (gate1 * y1 + gate2 * y2)
        if final:
            parts.append(xc)
            ssq = ssq + jnp.sum(xc * xc, axis=-1, keepdims=True)
        else:
            out_ref[:, c * LANES:(c + 1) * LANES] = xc
    if final:
        inv = lax.rsqrt(ssq * (1.0 / D_MODEL) + EPS)
        for c in range(ROW_TILES):
            lanes = slice(c * LANES, (c + 1) * LANES)
            out_ref[:, lanes] = parts[c] * inv * gf_ref[:, lanes]


def _combine(dest, x2, info, gf, ybuf, final):
    T = x2.shape[0]
    row = lambda i, d: (i, 0)
    return pl.pallas_call(
        functools.partial(_combine_kernel, final=final),
        grid_spec=pltpu.PrefetchScalarGridSpec(
            num_scalar_prefetch=1,
            grid=(T // TE,),
            in_specs=[
                pl.BlockSpec((TE, D_MODEL), row),
                pl.BlockSpec((TE, LANES), row),
                pl.BlockSpec((1, D_MODEL), lambda i, d: (0, 0)),
                pl.BlockSpec(memory_space=pl.ANY),
            ],
            out_specs=pl.BlockSpec((TE, D_MODEL), row),
            scratch_shapes=[
                pltpu.VMEM((TOP_K * TE * ROW_TILES, LANES), F32),
                pltpu.SemaphoreType.DMA(()),
            ],
        ),
        out_shape=jax.ShapeDtypeStruct((T, D_MODEL), F32),
        compiler_params=pltpu.CompilerParams(dimension_semantics=("arbitrary",), vmem_limit_bytes=VMEM_LIMIT),
        name="combine_final" if final else "combine",
    )(dest, x2, info, gf, ybuf)


def kernel(x, mem, norm_mix, w_in, v_norm, w_spatial, b_spatial, out_norm_a, out_norm_b, w_out, norm_cross, norm_mem, w_xq, w_xk, w_xv, w_xo, norm_moe, w_group, b_group, w_router, b_router, w1, w3, w2, norm_final):
    B, S, D = x.shape
    L = w_in.shape[0]
    T = B * S
    assert D == D_MODEL and S % TQ == 0 and T % TM_IN == 0 and S % TM_POST == 0 and T % TD == 0 and T % TE == 0

    row3 = lambda a: a.reshape(L, 1, -1)
    w_in_b, w_out_b = w_in.astype(BF16), w_out.astype(BF16)
    w_xq_b, w_xk_b, w_xv_b, w_xo_b = (w.astype(BF16) for w in (w_xq, w_xk, w_xv, w_xo))
    pad = jnp.zeros((L, D, LANES - N_GROUPS - N_EXPERTS), F32)
    w_route = jnp.concatenate([w_group, w_router, pad], axis=-1)
    w_route_hi = w_route.astype(BF16)
    w_route_lo = (w_route - w_route_hi.astype(F32)).astype(BF16)
    b_route = jnp.concatenate([b_group, b_router, pad[:, 0, :]], axis=-1).reshape(L, 1, LANES)
    bs_t = jnp.swapaxes(b_spatial, 1, 2)

    kx, vx = _mem_kv(mem, row3(norm_mem), w_xk_b, w_xv_b)

    n_slots = T * TOP_K + N_EXPERTS * BM
    n_blocks = n_slots // BM
    xs = x.reshape(T, D)
    for l in range(L):
        q, k, v, ya = _mix_in(xs, row3(norm_mix), w_in_b, row3(v_norm), w_spatial, bs_t, row3(out_norm_a), l)
        yb = _stick_break(q, k, v, B, S).reshape(T, SB_WIDTH)
        x2, hm, info, cnt = _post(xs, ya, yb, row3(out_norm_b), w_out_b, row3(norm_cross), w_xq_b, kx, vx, w_xo_b,
                                  row3(norm_moe), w_route_hi, w_route_lo, b_route, l, S)
        eid = info[:, 0:TOP_K].astype(jnp.int32)
        rank = info[:, 4:4 + TOP_K].astype(jnp.int32)
        counts = cnt[0, ROUTE_BASE:ROUTE_BASE + N_EXPERTS].astype(jnp.int32)
        padded = ((counts + BM - 1) // BM) * BM
        seg_end = jnp.cumsum(padded)
        seg_start = seg_end - padded
        expert_ids = jnp.arange(N_EXPERTS, dtype=jnp.int32)
        dest = (rank + jnp.sum(jnp.where(eid[..., None] == expert_ids, seg_start, 0), axis=-1)).reshape(-1)
        blk_row = jnp.arange(n_blocks, dtype=jnp.int32) * BM
        blk_e = jnp.minimum(
            jnp.sum((seg_end[None, :] <= blk_row[:, None]).astype(jnp.int32), axis=1), N_EXPERTS - 1)
        n_used = (seg_end[-1:] // BM).astype(jnp.int32)
        buf = _dispatch(dest, hm, jnp.zeros((n_slots * ROW_TILES, LANES), F32))
        ybuf = _experts(blk_e, n_used, buf, w1, w3, w2, l)
        xs = _combine(dest, x2, info, norm_final.reshape(1, D), ybuf, final=(l == L - 1))
    return xs.reshape(B, S, D)
```

```python
import jax
import jax.numpy as jnp
from jax import lax
from jax.experimental import pallas as pl
from jax.experimental.pallas import tpu as pltpu

F32 = jnp.float32
BF16 = jnp.bfloat16

D_MODEL = 1024
A_WIDTH = 512
A_GROUPS = 4
A_CH = 128
CHUNK = 128
SB_WIDTH = 512
SB_HEADS = 8
SB_HEAD_DIM = 64
SB_PAIRS = SB_HEADS // 2
IN_WIDTH = 2 * A_WIDTH + 3 * SB_WIDTH
X_HEADS = 4
X_HEAD_DIM = 256
N_GROUPS = 4
EXPERTS_PER_GROUP = 8
N_EXPERTS = 32
TOP_K = 2
D_EXPERT = 512
EPS = 1e-6

LANES = 128
SUBLANES = 8
ROW_TILES = D_MODEL // LANES

TM_IN = 512
TQ = 256
TK = 256
TM_POST = 512
TD = 512
BM = 256
TE = 256
ROUTE_BASE = N_GROUPS
NEG = -1e30

VMEM_LIMIT = 52 * 1024 * 1024


def _rms(x, g):
    return x * lax.rsqrt(jnp.mean(x * x, axis=-1, keepdims=True) + EPS) * g


def _dot(a, b):
    return jnp.dot(a, b, preferred_element_type=F32)


def _dot_nt(a, b):
    return lax.dot_general(a, b, (((1,), (1,)), ((), ())), preferred_element_type=F32)


def _split_bf16(x):
    hi = x.astype(BF16)
    lo = (x - hi.astype(F32)).astype(BF16)
    return hi, lo


def _row_copy(src, src_row, dst, dst_row, sem):
    return pltpu.make_async_copy(
        src.at[pl.ds(pl.multiple_of(src_row * ROW_TILES, ROW_TILES), ROW_TILES), :],
        dst.at[pl.ds(pl.multiple_of(dst_row * ROW_TILES, ROW_TILES), ROW_TILES), :],
        sem)


def _moe_output_chunks(dest_ref, x2_ref, info_ref, ybuf_ref, rows_ref, sems):
    tm = x2_ref.shape[0]
    i = pl.program_id(0)

    def start_gather(step, slot):
        base = step * (tm * TOP_K)

        def issue(r, c):
            for k in range(TOP_K):
                _row_copy(ybuf_ref, dest_ref[base + TOP_K * r + k], rows_ref.at[slot], k * tm + r,
                          sems.at[slot]).start(priority=k)
            return c

        lax.fori_loop(0, tm, issue, 0)

    @pl.when(i == 0)
    def _():
        start_gather(0, 0)

    @pl.when(i + 1 < pl.num_programs(0))
    def _():
        start_gather(i + 1, (i + 1) % 2)

    slot = i % 2
    rows = rows_ref.at[slot]
    pltpu.make_async_copy(ybuf_ref.at[pl.ds(0, TOP_K * tm * ROW_TILES), :], rows, sems.at[slot]).wait()

    info = info_ref[...]
    gate1 = info[:, 2:3]
    gate2 = info[:, 3:4]
    chunks = []
    for c in range(ROW_TILES):
        y1 = rows[pl.ds(c, tm, stride=ROW_TILES), :]
        y2 = rows[pl.ds(tm * ROW_TILES + c, tm, stride=ROW_TILES), :]
        chunks.append(x2_ref[:, c * LANES:(c + 1) * LANES] + (gate1 * y1 + gate2 * y2))
    return chunks


def _moe_gather_scratch(tm):
    return [pltpu.VMEM((2, TOP_K * tm * ROW_TILES, LANES), F32), pltpu.SemaphoreType.DMA((2,))]


def _mix_in_body(x, g_ref, w_ref, vg_ref, ws_ref, bs_ref, ga_ref, q_ref, k_ref, v_ref, ya_ref):
    tm = x.shape[0]
    h = _rms(x, g_ref[...]).astype(BF16)
    z = _dot(h, w_ref[...])

    t_idx = lax.broadcasted_iota(jnp.int32, (CHUNK, CHUNK), 0)
    s_idx = lax.broadcasted_iota(jnp.int32, (CHUNK, CHUNK), 1)
    causal = s_idx <= t_idx
    parts = []
    ssq = jnp.zeros((tm, 1), F32)
    for g in range(A_GROUPS):
        lanes = slice(g * A_CH, (g + 1) * A_CH)
        u = jax.nn.gelu(z[:, g * A_CH:(g + 1) * A_CH])
        vg = jax.nn.gelu(z[:, A_WIDTH + g * A_CH:A_WIDTH + (g + 1) * A_CH])
        vn = _rms(vg, vg_ref[:, lanes]).astype(BF16)
        ws = jnp.where(causal, ws_ref[g], 0.0).astype(BF16)
        bias = bs_ref[:, g:g + 1]
        mixed = jnp.concatenate(
            [_dot(ws, vn[c * CHUNK:(c + 1) * CHUNK, :]) + bias for c in range(tm // CHUNK)], axis=0)
        ya = u * mixed
        parts.append(ya)
        ssq = ssq + jnp.sum(ya * ya, axis=-1, keepdims=True)
    inv = lax.rsqrt(ssq * (1.0 / A_WIDTH) + EPS)
    for g in range(A_GROUPS):
        lanes = slice(g * A_CH, (g + 1) * A_CH)
        ya_ref[:, lanes] = (parts[g] * inv * ga_ref[:, lanes]).astype(BF16)

    lane = lax.broadcasted_iota(jnp.int32, (1, LANES), 1)
    q0 = 2 * A_WIDTH
    k0 = q0 + SB_WIDTH
    v0 = k0 + SB_WIDTH
    scale = 1.0 / (SB_HEAD_DIM ** 0.5)
    for hd in range(SB_HEADS):
        pair = hd // 2
        keep = (lane < SB_HEAD_DIM) if hd % 2 == 0 else (lane >= SB_HEAD_DIM)
        qs = z[:, q0 + pair * LANES:q0 + (pair + 1) * LANES] * scale
        vs = z[:, v0 + pair * LANES:v0 + (pair + 1) * LANES]
        q_ref[:, hd * LANES:(hd + 1) * LANES] = jnp.where(keep, qs, 0.0).astype(BF16)
        v_ref[:, hd * LANES:(hd + 1) * LANES] = jnp.where(keep, vs, 0.0).astype(BF16)
    k_ref[...] = z[:, k0:v0].astype(BF16)


def _mix_in_kernel(x_ref, *refs):
    _mix_in_body(x_ref[...], *refs)


def _moe_mix_in_kernel(dest_ref, x2_ref, info_ref, ybuf_ref, g_ref, w_ref, vg_ref, ws_ref, bs_ref, ga_ref,
                       x_ref, q_ref, k_ref, v_ref, ya_ref, rows_ref, sems):
    x = jnp.concatenate(_moe_output_chunks(dest_ref, x2_ref, info_ref, ybuf_ref, rows_ref, sems), axis=-1)
    x_ref[...] = x
    _mix_in_body(x, g_ref, w_ref, vg_ref, ws_ref, bs_ref, ga_ref, q_ref, k_ref, v_ref, ya_ref)


def _mix_in_specs(l, index):
    lay = lambda *a: (l, 0, 0)
    in_specs = [
        pl.BlockSpec((None, 1, D_MODEL), lay),
        pl.BlockSpec((None, D_MODEL, IN_WIDTH), lay),
        pl.BlockSpec((None, 1, A_WIDTH), lay),
        pl.BlockSpec((None, A_GROUPS, CHUNK, CHUNK), lambda *a: (l, 0, 0, 0)),
        pl.BlockSpec((None, CHUNK, A_GROUPS), lay),
        pl.BlockSpec((None, 1, A_WIDTH), lay),
    ]
    out_specs = [
        pl.BlockSpec((TM_IN, SB_HEADS * LANES), index),
        pl.BlockSpec((TM_IN, SB_WIDTH), index),
        pl.BlockSpec((TM_IN, SB_HEADS * LANES), index),
        pl.BlockSpec((TM_IN, A_WIDTH), index),
    ]
    return in_specs, out_specs


def _mix_in_out_shapes(T):
    return [
        jax.ShapeDtypeStruct((T, SB_HEADS * LANES), BF16),
        jax.ShapeDtypeStruct((T, SB_WIDTH), BF16),
        jax.ShapeDtypeStruct((T, SB_HEADS * LANES), BF16),
        jax.ShapeDtypeStruct((T, A_WIDTH), BF16),
    ]


def _mix_in(x2d, g, w_in, vg, ws, bs_t, ga, l):
    T = x2d.shape[0]
    row = lambda i: (i, 0)
    in_specs, out_specs = _mix_in_specs(l, row)
    return pl.pallas_call(
        _mix_in_kernel,
        grid=(T // TM_IN,),
        in_specs=[pl.BlockSpec((TM_IN, D_MODEL), row)] + in_specs,
        out_specs=out_specs,
        out_shape=_mix_in_out_shapes(T),
        compiler_params=pltpu.CompilerParams(dimension_semantics=("arbitrary",), vmem_limit_bytes=VMEM_LIMIT),
        name="mix_in",
    )(x2d, g, w_in, vg, ws, bs_t, ga)


def _moe_mix_in(dest, x2, info, ybuf, g, w_in, vg, ws, bs_t, ga, l):
    T = x2.shape[0]
    row = lambda i, d: (i, 0)
    in_specs, out_specs = _mix_in_specs(l, row)
    return pl.pallas_call(
        _moe_mix_in_kernel,
        grid_spec=pltpu.PrefetchScalarGridSpec(
            num_scalar_prefetch=1,
            grid=(T // TM_IN,),
            in_specs=[
                pl.BlockSpec((TM_IN, D_MODEL), row),
                pl.BlockSpec((TM_IN, LANES), row),
                pl.BlockSpec(memory_space=pl.ANY),
            ] + in_specs,
            out_specs=[pl.BlockSpec((TM_IN, D_MODEL), row)] + out_specs,
            scratch_shapes=_moe_gather_scratch(TM_IN),
        ),
        out_shape=[jax.ShapeDtypeStruct((T, D_MODEL), F32)] + _mix_in_out_shapes(T),
        compiler_params=pltpu.CompilerParams(dimension_semantics=("arbitrary",), vmem_limit_bytes=VMEM_LIMIT),
        name="moe_mix_in",
    )(dest, x2, info, ybuf, g, w_in, vg, ws, bs_t, ga)


def _stick_break_kernel(q_ref, k_ref, v_ref, o_ref):
    qi = pl.program_id(2)
    q = q_ref[...]
    r_idx = lax.broadcasted_iota(jnp.int32, (TQ, TK), 0)
    c_idx = lax.broadcasted_iota(jnp.int32, (TQ, TK), 1)
    strict = c_idx < r_idx
    later = jnp.where(c_idx < r_idx, -1.0, 0.0).astype(BF16)

    def block(j, carry, masked):
        acc, c0, c1 = carry
        start = pl.multiple_of(j * TK, TK)
        kb = k_ref[pl.ds(start, TK), :]
        vb = v_ref[pl.ds(start, TK), :]
        cs = (c0, c1)
        new_c = []
        for hd in range(2):
            z = _dot_nt(q[:, hd * LANES:(hd + 1) * LANES], kb)
            lg = jnp.log(1.0 + jnp.exp(-jnp.abs(z)))
            sp = jnp.maximum(z, 0.0) + lg
            lsz = jnp.minimum(z, 0.0) - lg
            if masked:
                sp = jnp.where(strict, sp, 0.0)
            sp_hi, sp_lo = _split_bf16(sp)
            after = _dot(sp_hi, later) + _dot(sp_lo, later)
            p = jnp.exp(lsz + after)
            if masked:
                p = jnp.where(strict, p, 0.0)
            ob = _dot(p.astype(BF16), vb[:, hd * LANES:(hd + 1) * LANES])
            acc = acc + ob * jnp.exp(cs[hd])
            new_c.append(cs[hd] - jnp.sum(sp, axis=-1, keepdims=True))
        return acc, new_c[0], new_c[1]

    def alive(c0, c1):
        return (jnp.max(jnp.maximum(jnp.exp(c0), jnp.exp(c1))) > 0.0).astype(jnp.int32)

    def cond(state):
        jj, live = state[0], state[1]
        return jnp.logical_and(jj < qi, live > 0)

    def body(state):
        jj, _, acc, c0, c1 = state
        acc, c0, c1 = block(qi - 1 - jj, (acc, c0, c1), False)
        return jj + 1, alive(c0, c1), acc, c0, c1

    carry = (jnp.zeros((TQ, LANES), F32), jnp.zeros((TQ, 1), F32), jnp.zeros((TQ, 1), F32))
    acc, c0, c1 = block(qi, carry, True)
    state = lax.while_loop(cond, body, (jnp.int32(0), alive(c0, c1), acc, c0, c1))
    o_ref[...] = state[2]


def _stick_break(q, k, v, B, S):
    grid = (B, SB_PAIRS, S // TQ)
    return pl.pallas_call(
        _stick_break_kernel,
        grid=grid,
        in_specs=[
            pl.BlockSpec((None, TQ, 2 * LANES), lambda b, p, i: (b, i, p)),
            pl.BlockSpec((None, S, LANES), lambda b, p, i: (b, 0, p)),
            pl.BlockSpec((None, S, 2 * LANES), lambda b, p, i: (b, 0, p)),
        ],
        out_specs=pl.BlockSpec((None, TQ, LANES), lambda b, p, i: (b, i, p)),
        out_shape=jax.ShapeDtypeStruct((B, S, SB_WIDTH), F32),
        compiler_params=pltpu.CompilerParams(
            dimension_semantics=("arbitrary", "arbitrary", "arbitrary"), vmem_limit_bytes=VMEM_LIMIT),
        name="stick_break",
    )(q.reshape(B, S, SB_HEADS * LANES), k.reshape(B, S, SB_WIDTH), v.reshape(B, S, SB_HEADS * LANES))


def _mem_kv_kernel(m_ref, g_ref, wk_ref, wv_ref, k_ref, v_ref):
    m = _rms(m_ref[...], g_ref[...]).astype(BF16)
    k_ref[...] = _dot(m, wk_ref[...]).astype(BF16)
    v_ref[...] = _dot(m, wv_ref[...]).astype(BF16)


def _mem_kv(mem, g, wk, wv):
    B, M, _ = mem.shape
    L = wk.shape[0]
    kv_spec = pl.BlockSpec((None, None, M, D_MODEL), lambda l, b: (l, b, 0, 0))
    w_spec = pl.BlockSpec((None, D_MODEL, D_MODEL), lambda l, b: (l, 0, 0))
    return pl.pallas_call(
        _mem_kv_kernel,
        grid=(L, B),
        in_specs=[
            pl.BlockSpec((None, M, D_MODEL), lambda l, b: (b, 0, 0)),
            pl.BlockSpec((None, 1, D_MODEL), lambda l, b: (l, 0, 0)),
            w_spec, w_spec,
        ],
        out_specs=[kv_spec, kv_spec],
        out_shape=[jax.ShapeDtypeStruct((L, B, M, D_MODEL), BF16)] * 2,
        compiler_params=pltpu.CompilerParams(
            dimension_semantics=("arbitrary", "arbitrary"), vmem_limit_bytes=VMEM_LIMIT),
        name="mem_kv",
    )(mem, g, wk, wv)


def _post_kernel(x_ref, ya_ref, yb_ref, gb_ref, wo_ref, gc_ref, wq_ref, kx_ref, vx_ref, wxo_ref, gm_ref,
                 wrh_ref, wrl_ref, br_ref, x2_ref, hm_ref, info_ref, cnt_ref, run_ref):
    tm = x_ref.shape[0]

    @pl.when(pl.program_id(0) == 0)
    def _():
        run_ref[...] = jnp.zeros_like(run_ref)

    ybn = _rms(yb_ref[...], gb_ref[...]).astype(BF16)
    x1 = x_ref[...] + _dot(jnp.concatenate([ya_ref[...], ybn], axis=-1), wo_ref[...])

    h = _rms(x1, gc_ref[...]).astype(BF16)
    q = (_dot(h, wq_ref[...]) * (1.0 / (X_HEAD_DIM ** 0.5))).astype(BF16)
    heads = []
    for hd in range(X_HEADS):
        cols = slice(hd * X_HEAD_DIM, (hd + 1) * X_HEAD_DIM)
        s = _dot_nt(q[:, cols], kx_ref[:, cols])
        p = jnp.exp(s - jnp.max(s, axis=-1, keepdims=True))
        p = p / jnp.sum(p, axis=-1, keepdims=True)
        heads.append(_dot(p.astype(BF16), vx_ref[:, cols]).astype(BF16))
    x2 = x1 + _dot(jnp.concatenate(heads, axis=-1), wxo_ref[...])
    x2_ref[...] = x2

    hm = _rms(x2, gm_ref[...])
    for c in range(ROW_TILES):
        hm_ref[pl.ds(c, tm, stride=ROW_TILES), :] = hm[:, c * LANES:(c + 1) * LANES]

    hm_hi, hm_lo = _split_bf16(hm)
    logits = _dot(hm_hi, wrh_ref[...]) + _dot(hm_hi, wrl_ref[...]) + _dot(hm_lo, wrh_ref[...]) + br_ref[...]

    lane = lax.broadcasted_iota(jnp.int32, (tm, LANES), 1)
    lanef = lane.astype(F32)
    big = float(LANES)
    is_group = lane < N_GROUPS
    gl = jnp.where(is_group, logits, NEG)
    gmax = jnp.max(gl, axis=-1, keepdims=True)
    gsel = jnp.min(jnp.where(gl == gmax, lanef, big), axis=-1, keepdims=True)
    gden = jnp.sum(jnp.where(is_group, jnp.exp(gl - gmax), 0.0), axis=-1, keepdims=True)
    g_gate = 1.0 / gden
    lo = ROUTE_BASE + EXPERTS_PER_GROUP * gsel
    in_group = (lanef >= lo) & (lanef < lo + EXPERTS_PER_GROUP)
    el = jnp.where(in_group, logits, NEG)
    v1 = jnp.max(el, axis=-1, keepdims=True)
    i1 = jnp.min(jnp.where(el == v1, lanef, big), axis=-1, keepdims=True)
    el2 = jnp.where(lanef == i1, NEG, el)
    v2 = jnp.max(el2, axis=-1, keepdims=True)
    i2 = jnp.min(jnp.where(el2 == v2, lanef, big), axis=-1, keepdims=True)
    t = jnp.exp(v2 - v1)
    den = 1.0 + t
    gate1 = g_gate * (1.0 / den)
    gate2 = g_gate * (t / den)

    hit1 = lanef == i1
    hit2 = lanef == i2
    multi = jnp.where(hit1 | hit2, 1.0, 0.0)
    r_idx = lax.broadcasted_iota(jnp.int32, (tm, tm), 0)
    c_idx = lax.broadcasted_iota(jnp.int32, (tm, tm), 1)
    earlier = jnp.where(c_idx < r_idx, 1.0, 0.0).astype(BF16)
    before = _dot(earlier, multi.astype(BF16)) + run_ref[0:1, :]
    rank1 = jnp.sum(jnp.where(hit1, before, 0.0), axis=-1, keepdims=True)
    rank2 = jnp.sum(jnp.where(hit2, before, 0.0), axis=-1, keepdims=True)
    run_ref[...] = run_ref[...] + jnp.sum(multi, axis=0, keepdims=True)
    cnt_ref[...] = run_ref[...]

    info = jnp.where(lane == 0, i1 - ROUTE_BASE, 0.0)
    info = jnp.where(lane == 1, i2 - ROUTE_BASE, info)
    info = jnp.where(lane == 2, gate1, info)
    info = jnp.where(lane == 3, gate2, info)
    info = jnp.where(lane == 4, rank1, info)
    info = jnp.where(lane == 5, rank2, info)
    info_ref[...] = info


def _post(x2d, ya, yb, gb, wo, gc, wq, kx, vx, wxo, gm, wrh, wrl, br, l, S):
    T = x2d.shape[0]
    M = kx.shape[2]
    grid = (T // TM_POST,)
    row = lambda i: (i, 0)
    lay = lambda i: (l, 0, 0)
    batch = lambda i: (l, (i * TM_POST) // S, 0, 0)
    wspec = pl.BlockSpec((None, D_MODEL, D_MODEL), lay)
    gspec = pl.BlockSpec((None, 1, D_MODEL), lay)
    return pl.pallas_call(
        _post_kernel,
        grid=grid,
        in_specs=[
            pl.BlockSpec((TM_POST, D_MODEL), row),
            pl.BlockSpec((TM_POST, A_WIDTH), row),
            pl.BlockSpec((TM_POST, SB_WIDTH), row),
            pl.BlockSpec((None, 1, SB_WIDTH), lay),
            wspec, gspec, wspec,
            pl.BlockSpec((None, None, M, D_MODEL), batch),
            pl.BlockSpec((None, None, M, D_MODEL), batch),
            wspec, gspec,
            pl.BlockSpec((None, D_MODEL, LANES), lay),
            pl.BlockSpec((None, D_MODEL, LANES), lay),
            pl.BlockSpec((None, 1, LANES), lay),
        ],
        out_specs=[
            pl.BlockSpec((TM_POST, D_MODEL), row),
            pl.BlockSpec((TM_POST * ROW_TILES, LANES), row),
            pl.BlockSpec((TM_POST, LANES), row),
            pl.BlockSpec((SUBLANES, LANES), lambda i: (0, 0)),
        ],
        out_shape=[
            jax.ShapeDtypeStruct((T, D_MODEL), F32),
            jax.ShapeDtypeStruct((T * ROW_TILES, LANES), F32),
            jax.ShapeDtypeStruct((T, LANES), F32),
            jax.ShapeDtypeStruct((SUBLANES, LANES), F32),
        ],
        scratch_shapes=[pltpu.VMEM((SUBLANES, LANES), F32)],
        compiler_params=pltpu.CompilerParams(dimension_semantics=("arbitrary",), vmem_limit_bytes=VMEM_LIMIT),
        name="post",
    )(x2d, ya, yb, gb, wo, gc, wq, kx, vx, wxo, gm, wrh, wrl, br)


def _dispatch_kernel(dest_ref, hm_ref, buf_in_ref, buf_ref, sem):
    del buf_in_ref
    base = pl.program_id(0) * (TD * TOP_K)

    def issue(r, c):
        for k in range(TOP_K):
            _row_copy(hm_ref, r, buf_ref, dest_ref[base + TOP_K * r + k], sem).start(priority=k)
        return c

    lax.fori_loop(0, TD, issue, 0)
    for k in range(TOP_K):
        pltpu.make_async_copy(hm_ref, buf_ref.at[pl.ds(0, TD * ROW_TILES), :], sem).wait()


def _dispatch(dest, hm, buf0):
    T = hm.shape[0] // ROW_TILES
    return pl.pallas_call(
        _dispatch_kernel,
        grid_spec=pltpu.PrefetchScalarGridSpec(
            num_scalar_prefetch=1,
            grid=(T // TD,),
            in_specs=[
                pl.BlockSpec((TD * ROW_TILES, LANES), lambda i, d: (i, 0)),
                pl.BlockSpec(memory_space=pl.ANY),
            ],
            out_specs=pl.BlockSpec(memory_space=pl.ANY),
            scratch_shapes=[pltpu.SemaphoreType.DMA(())],
        ),
        out_shape=jax.ShapeDtypeStruct(buf0.shape, F32),
        input_output_aliases={2: 0},
        compiler_params=pltpu.CompilerParams(dimension_semantics=("arbitrary",), vmem_limit_bytes=VMEM_LIMIT),
        name="dispatch",
    )(dest, hm, buf0)


def _experts_kernel(blk_e_ref, n_used_ref, buf_ref, w1_ref, w3_ref, w2_ref, y_ref, w1b_ref, w3b_ref, w2b_ref):
    i = pl.program_id(0)
    expert = blk_e_ref[i]
    previous = blk_e_ref[jnp.maximum(i - 1, 0)]

    @pl.when(jnp.logical_or(i == 0, expert != previous))
    def _():
        w1b_ref[...] = w1_ref[...].astype(BF16)
        w3b_ref[...] = w3_ref[...].astype(BF16)
        w2b_ref[...] = w2_ref[...].astype(BF16)

    @pl.when(i < n_used_ref[0])
    def _():
        xb = jnp.concatenate(
            [buf_ref[pl.ds(c, BM, stride=ROW_TILES), :] for c in range(ROW_TILES)], axis=-1).astype(BF16)
        h1 = _dot(xb, w1b_ref[...])
        h3 = _dot(xb, w3b_ref[...])
        a = (h1 * jax.nn.sigmoid(h1) * h3).astype(BF16)
        y = _dot(a, w2b_ref[...])
        for c in range(ROW_TILES):
            y_ref[pl.ds(c, BM, stride=ROW_TILES), :] = y[:, c * LANES:(c + 1) * LANES]

    @pl.when(i >= n_used_ref[0])
    def _():
        y_ref[...] = jnp.zeros_like(y_ref)


def _experts(blk_e, n_used, buf, w1, w3, w2, l):
    n_blocks = buf.shape[0] // (BM * ROW_TILES)
    rows = pl.BlockSpec((BM * ROW_TILES, LANES), lambda i, e, n: (i, 0))
    return pl.pallas_call(
        _experts_kernel,
        grid_spec=pltpu.PrefetchScalarGridSpec(
            num_scalar_prefetch=2,
            grid=(n_blocks,),
            in_specs=[
                rows,
                pl.BlockSpec((None, None, D_MODEL, D_EXPERT), lambda i, e, n: (l, e[i], 0, 0)),
                pl.BlockSpec((None, None, D_MODEL, D_EXPERT), lambda i, e, n: (l, e[i], 0, 0)),
                pl.BlockSpec((None, None, D_EXPERT, D_MODEL), lambda i, e, n: (l, e[i], 0, 0)),
            ],
            out_specs=rows,
            scratch_shapes=[
                pltpu.VMEM((D_MODEL, D_EXPERT), BF16),
                pltpu.VMEM((D_MODEL, D_EXPERT), BF16),
                pltpu.VMEM((D_EXPERT, D_MODEL), BF16),
            ],
        ),
        out_shape=jax.ShapeDtypeStruct(buf.shape, F32),
        compiler_params=pltpu.CompilerParams(dimension_semantics=("arbitrary",), vmem_limit_bytes=VMEM_LIMIT),
        name="experts",
    )(blk_e, n_used, buf, w1, w3, w2)


def _combine_final_kernel(dest_ref, x2_ref, info_ref, gf_ref, ybuf_ref, out_ref, rows_ref, sems):
    chunks = _moe_output_chunks(dest_ref, x2_ref, info_ref, ybuf_ref, rows_ref, sems)
    ssq = jnp.zeros((x2_ref.shape[0], 1), F32)
    for xc in chunks:
        ssq = ssq + jnp.sum(xc * xc, axis=-1, keepdims=True)
    inv = lax.rsqrt(ssq * (1.0 / D_MODEL) + EPS)
    for c, xc in enumerate(chunks):
        lanes = slice(c * LANES, (c + 1) * LANES)
        out_ref[:, lanes] = xc * inv * gf_ref[:, lanes]


def _combine_final(dest, x2, info, gf, ybuf):
    T = x2.shape[0]
    row = lambda i, d: (i, 0)
    return pl.pallas_call(
        _combine_final_kernel,
        grid_spec=pltpu.PrefetchScalarGridSpec(
            num_scalar_prefetch=1,
            grid=(T // TE,),
            in_specs=[
                pl.BlockSpec((TE, D_MODEL), row),
                pl.BlockSpec((TE, LANES), row),
                pl.BlockSpec((1, D_MODEL), lambda i, d: (0, 0)),
                pl.BlockSpec(memory_space=pl.ANY),
            ],
            out_specs=pl.BlockSpec((TE, D_MODEL), row),
            scratch_shapes=_moe_gather_scratch(TE),
        ),
        out_shape=jax.ShapeDtypeStruct((T, D_MODEL), F32),
        compiler_params=pltpu.CompilerParams(dimension_semantics=("arbitrary",), vmem_limit_bytes=VMEM_LIMIT),
        name="combine_final",
    )(dest, x2, info, gf, ybuf)


def kernel(x, mem, norm_mix, w_in, v_norm, w_spatial, b_spatial, out_norm_a, out_norm_b, w_out, norm_cross, norm_mem, w_xq, w_xk, w_xv, w_xo, norm_moe, w_group, b_group, w_router, b_router, w1, w3, w2, norm_final):
    B, S, D = x.shape
    L = w_in.shape[0]
    T = B * S
    assert D == D_MODEL and S % TQ == 0 and T % TM_IN == 0 and S % TM_POST == 0 and T % TD == 0 and T % TE == 0

    row3 = lambda a: a.reshape(L, 1, -1)
    w_in_b, w_out_b = w_in.astype(BF16), w_out.astype(BF16)
    w_xq_b, w_xk_b, w_xv_b, w_xo_b = (w.astype(BF16) for w in (w_xq, w_xk, w_xv, w_xo))
    pad = jnp.zeros((L, D, LANES - N_GROUPS - N_EXPERTS), F32)
    w_route = jnp.concatenate([w_group, w_router, pad], axis=-1)
    w_route_hi = w_route.astype(BF16)
    w_route_lo = (w_route - w_route_hi.astype(F32)).astype(BF16)
    b_route = jnp.concatenate([b_group, b_router, pad[:, 0, :]], axis=-1).reshape(L, 1, LANES)
    bs_t = jnp.swapaxes(b_spatial, 1, 2)

    kx, vx = _mem_kv(mem, row3(norm_mem), w_xk_b, w_xv_b)

    n_slots = T * TOP_K + N_EXPERTS * BM
    n_blocks = n_slots // BM
    xs = x.reshape(T, D)
    moe = None
    for l in range(L):
        mix_args = (row3(norm_mix), w_in_b, row3(v_norm), w_spatial, bs_t, row3(out_norm_a), l)
        if moe is None:
            q, k, v, ya = _mix_in(xs, *mix_args)
        else:
            xs, q, k, v, ya = _moe_mix_in(*moe, *mix_args)
        yb = _stick_break(q, k, v, B, S).reshape(T, SB_WIDTH)
        x2, hm, info, cnt = _post(xs, ya, yb, row3(out_norm_b), w_out_b, row3(norm_cross), w_xq_b, kx, vx, w_xo_b,
                                  row3(norm_moe), w_route_hi, w_route_lo, b_route, l, S)
        eid = info[:, 0:TOP_K].astype(jnp.int32)
        rank = info[:, 4:4 + TOP_K].astype(jnp.int32)
        counts = cnt[0, ROUTE_BASE:ROUTE_BASE + N_EXPERTS].astype(jnp.int32)
        padded = ((counts + BM - 1) // BM) * BM
        seg_end = jnp.cumsum(padded)
        seg_start = seg_end - padded
        expert_ids = jnp.arange(N_EXPERTS, dtype=jnp.int32)
        dest = (rank + jnp.sum(jnp.where(eid[..., None] == expert_ids, seg_start, 0), axis=-1)).reshape(-1)
        blk_row = jnp.arange(n_blocks, dtype=jnp.int32) * BM
        blk_e = jnp.minimum(
            jnp.sum((seg_end[None, :] <= blk_row[:, None]).astype(jnp.int32), axis=1), N_EXPERTS - 1)
        n_used = (seg_end[-1:] // BM).astype(jnp.int32)
        buf = _dispatch(dest, hm, jnp.zeros((n_slots * ROW_TILES, LANES), F32))
        ybuf = _experts(blk_e, n_used, buf, w1, w3, w2, l)
        moe = (dest, x2, info, ybuf)
    dest, x2, info, ybuf = moe
    return _combine_final(dest, x2, info, norm_final.reshape(1, D), ybuf).reshape(B, S, D)
```

```python
import jax
import jax.numpy as jnp
from jax import lax
from jax.experimental import pallas as pl
from jax.experimental.pallas import tpu as pltpu

F32 = jnp.float32
BF16 = jnp.bfloat16

D_MODEL = 1024
A_WIDTH = 512
A_GROUPS = 4
A_CH = 128
CHUNK = 128
SB_WIDTH = 512
SB_HEADS = 8
SB_HEAD_DIM = 64
SB_PAIRS = SB_HEADS // 2
IN_WIDTH = 2 * A_WIDTH + 3 * SB_WIDTH
X_HEADS = 4
X_HEAD_DIM = 256
N_GROUPS = 4
EXPERTS_PER_GROUP = 8
N_EXPERTS = 32
TOP_K = 2
D_EXPERT = 512
EPS = 1e-6

LANES = 128
SUBLANES = 8
ROW_TILES = D_MODEL // LANES

TM_IN = 512
TQ = 256
TK = 256
HALF = TQ // 2
TM_POST = 512
TD = 512
BM = 512
TE = 256
ROUTE_BASE = N_GROUPS
NEG = -1e30

VMEM_LIMIT = 52 * 1024 * 1024


def _rms(x, g):
    return x * lax.rsqrt(jnp.mean(x * x, axis=-1, keepdims=True) + EPS) * g


def _dot(a, b):
    return jnp.dot(a, b, preferred_element_type=F32)


def _dot_nt(a, b):
    return lax.dot_general(a, b, (((1,), (1,)), ((), ())), preferred_element_type=F32)


def _split_bf16(x):
    hi = x.astype(BF16)
    lo = (x - hi.astype(F32)).astype(BF16)
    return hi, lo


def _row_copy(src, src_row, dst, dst_row, sem):
    return pltpu.make_async_copy(
        src.at[pl.ds(pl.multiple_of(src_row * ROW_TILES, ROW_TILES), ROW_TILES), :],
        dst.at[pl.ds(pl.multiple_of(dst_row * ROW_TILES, ROW_TILES), ROW_TILES), :],
        sem)


def _moe_output_chunks(dest_ref, x2_ref, info_ref, ybuf_ref, rows_ref, sems):
    tm = x2_ref.shape[0]
    i = pl.program_id(0)

    def start_gather(step, slot):
        base = step * (tm * TOP_K)

        def issue(r, c):
            for k in range(TOP_K):
                _row_copy(ybuf_ref, dest_ref[base + TOP_K * r + k], rows_ref.at[slot], k * tm + r,
                          sems.at[slot]).start(priority=1)
            return c

        lax.fori_loop(0, tm, issue, 0)

    @pl.when(i == 0)
    def _():
        start_gather(0, 0)

    @pl.when(i + 1 < pl.num_programs(0))
    def _():
        start_gather(i + 1, (i + 1) % 2)

    slot = i % 2
    rows = rows_ref.at[slot]
    pltpu.make_async_copy(ybuf_ref.at[pl.ds(0, TOP_K * tm * ROW_TILES), :], rows, sems.at[slot]).wait()

    info = info_ref[...]
    gate1 = info[:, 2:3]
    gate2 = info[:, 3:4]
    chunks = []
    for c in range(ROW_TILES):
        y1 = rows[pl.ds(c, tm, stride=ROW_TILES), :]
        y2 = rows[pl.ds(tm * ROW_TILES + c, tm, stride=ROW_TILES), :]
        chunks.append(x2_ref[:, c * LANES:(c + 1) * LANES] + (gate1 * y1 + gate2 * y2))
    return chunks


def _moe_gather_scratch(tm):
    return [pltpu.VMEM((2, TOP_K * tm * ROW_TILES, LANES), F32), pltpu.SemaphoreType.DMA((2,))]


def _mix_in_body(x, g_ref, w_ref, vg_ref, ws_ref, bs_ref, ga_ref, q_ref, k_ref, v_ref, ya_ref):
    tm = x.shape[0]
    h = _rms(x, g_ref[...]).astype(BF16)
    z = _dot(h, w_ref[...])

    t_idx = lax.broadcasted_iota(jnp.int32, (CHUNK, CHUNK), 0)
    s_idx = lax.broadcasted_iota(jnp.int32, (CHUNK, CHUNK), 1)
    causal = s_idx <= t_idx
    parts = []
    ssq = jnp.zeros((tm, 1), F32)
    for g in range(A_GROUPS):
        lanes = slice(g * A_CH, (g + 1) * A_CH)
        u = jax.nn.gelu(z[:, g * A_CH:(g + 1) * A_CH])
        vg = jax.nn.gelu(z[:, A_WIDTH + g * A_CH:A_WIDTH + (g + 1) * A_CH])
        vn = _rms(vg, vg_ref[:, lanes]).astype(BF16)
        ws = jnp.where(causal, ws_ref[g], 0.0).astype(BF16)
        bias = bs_ref[:, g:g + 1]
        mixed = jnp.concatenate(
            [_dot(ws, vn[c * CHUNK:(c + 1) * CHUNK, :]) + bias for c in range(tm // CHUNK)], axis=0)
        ya = u * mixed
        parts.append(ya)
        ssq = ssq + jnp.sum(ya * ya, axis=-1, keepdims=True)
    inv = lax.rsqrt(ssq * (1.0 / A_WIDTH) + EPS)
    for g in range(A_GROUPS):
        lanes = slice(g * A_CH, (g + 1) * A_CH)
        ya_ref[:, lanes] = (parts[g] * inv * ga_ref[:, lanes]).astype(BF16)

    lane = lax.broadcasted_iota(jnp.int32, (1, LANES), 1)
    q0 = 2 * A_WIDTH
    k0 = q0 + SB_WIDTH
    v0 = k0 + SB_WIDTH
    scale = 1.0 / (SB_HEAD_DIM ** 0.5)
    for hd in range(SB_HEADS):
        pair = hd // 2
        keep = (lane < SB_HEAD_DIM) if hd % 2 == 0 else (lane >= SB_HEAD_DIM)
        qs = z[:, q0 + pair * LANES:q0 + (pair + 1) * LANES] * scale
        vs = z[:, v0 + pair * LANES:v0 + (pair + 1) * LANES]
        q_ref[:, hd * LANES:(hd + 1) * LANES] = jnp.where(keep, qs, 0.0).astype(BF16)
        v_ref[:, hd * LANES:(hd + 1) * LANES] = jnp.where(keep, vs, 0.0).astype(BF16)
    k_ref[...] = z[:, k0:v0].astype(BF16)


def _mix_in_kernel(x_ref, *refs):
    _mix_in_body(x_ref[...], *refs)


def _moe_mix_in_kernel(dest_ref, x2_ref, info_ref, ybuf_ref, g_ref, w_ref, vg_ref, ws_ref, bs_ref, ga_ref,
                       x_ref, q_ref, k_ref, v_ref, ya_ref, rows_ref, sems):
    x = jnp.concatenate(_moe_output_chunks(dest_ref, x2_ref, info_ref, ybuf_ref, rows_ref, sems), axis=-1)
    x_ref[...] = x
    _mix_in_body(x, g_ref, w_ref, vg_ref, ws_ref, bs_ref, ga_ref, q_ref, k_ref, v_ref, ya_ref)


def _mix_in_specs(l, index):
    lay = lambda *a: (l, 0, 0)
    in_specs = [
        pl.BlockSpec((None, 1, D_MODEL), lay),
        pl.BlockSpec((None, D_MODEL, IN_WIDTH), lay),
        pl.BlockSpec((None, 1, A_WIDTH), lay),
        pl.BlockSpec((None, A_GROUPS, CHUNK, CHUNK), lambda *a: (l, 0, 0, 0)),
        pl.BlockSpec((None, CHUNK, A_GROUPS), lay),
        pl.BlockSpec((None, 1, A_WIDTH), lay),
    ]
    out_specs = [
        pl.BlockSpec((TM_IN, SB_HEADS * LANES), index),
        pl.BlockSpec((TM_IN, SB_WIDTH), index),
        pl.BlockSpec((TM_IN, SB_HEADS * LANES), index),
        pl.BlockSpec((TM_IN, A_WIDTH), index),
    ]
    return in_specs, out_specs


def _mix_in_out_shapes(T):
    return [
        jax.ShapeDtypeStruct((T, SB_HEADS * LANES), BF16),
        jax.ShapeDtypeStruct((T, SB_WIDTH), BF16),
        jax.ShapeDtypeStruct((T, SB_HEADS * LANES), BF16),
        jax.ShapeDtypeStruct((T, A_WIDTH), BF16),
    ]


def _mix_in(x2d, g, w_in, vg, ws, bs_t, ga, l):
    T = x2d.shape[0]
    row = lambda i: (i, 0)
    in_specs, out_specs = _mix_in_specs(l, row)
    return pl.pallas_call(
        _mix_in_kernel,
        grid=(T // TM_IN,),
        in_specs=[pl.BlockSpec((TM_IN, D_MODEL), row)] + in_specs,
        out_specs=out_specs,
        out_shape=_mix_in_out_shapes(T),
        compiler_params=pltpu.CompilerParams(dimension_semantics=("arbitrary",), vmem_limit_bytes=VMEM_LIMIT),
        name="mix_in",
    )(x2d, g, w_in, vg, ws, bs_t, ga)


def _moe_mix_in(dest, x2, info, ybuf, g, w_in, vg, ws, bs_t, ga, l):
    T = x2.shape[0]
    row = lambda i, d: (i, 0)
    in_specs, out_specs = _mix_in_specs(l, row)
    return pl.pallas_call(
        _moe_mix_in_kernel,
        grid_spec=pltpu.PrefetchScalarGridSpec(
            num_scalar_prefetch=1,
            grid=(T // TM_IN,),
            in_specs=[
                pl.BlockSpec((TM_IN, D_MODEL), row),
                pl.BlockSpec((TM_IN, LANES), row),
                pl.BlockSpec(memory_space=pl.ANY),
            ] + in_specs,
            out_specs=[pl.BlockSpec((TM_IN, D_MODEL), row)] + out_specs,
            scratch_shapes=_moe_gather_scratch(TM_IN),
        ),
        out_shape=[jax.ShapeDtypeStruct((T, D_MODEL), F32)] + _mix_in_out_shapes(T),
        compiler_params=pltpu.CompilerParams(dimension_semantics=("arbitrary",), vmem_limit_bytes=VMEM_LIMIT),
        name="moe_mix_in",
    )(dest, x2, info, ybuf, g, w_in, vg, ws, bs_t, ga)


def _stick_break_kernel(q_ref, k_ref, v_ref, o_ref):
    qi = pl.program_id(1)
    r_idx = lax.broadcasted_iota(jnp.int32, (TK, TK), 0)
    c_idx = lax.broadcasted_iota(jnp.int32, (TK, TK), 1)
    later = jnp.where(c_idx < r_idx, -1.0, 0.0).astype(BF16)

    def chain(qh, kb, vbh, later_m, c, visible):
        z = _dot_nt(qh, kb)
        sp = jnp.maximum(z, 0.0) + jnp.log(1.0 + jnp.exp(-jnp.abs(z)))
        lsz = z - sp
        if visible is not None:
            sp = jnp.where(visible, sp, 0.0)
        after = _dot(sp.astype(BF16), later_m)
        p = jnp.exp(lsz + after)
        if visible is not None:
            p = jnp.where(visible, p, 0.0)
        ob = _dot(p.astype(BF16), vbh)
        return ob * jnp.exp(c), c - jnp.sum(sp, axis=-1, keepdims=True)

    def sweep(rows, key0, n_keys, later_m, half, visible):
        accs, cs, _ = half
        new_accs, new_cs = [], []
        top = None
        for pr in range(SB_PAIRS):
            kb = k_ref[pl.ds(key0, n_keys), pr * LANES:(pr + 1) * LANES]
            acc = accs[pr]
            for hd in (2 * pr, 2 * pr + 1):
                lanes = slice(hd * LANES, (hd + 1) * LANES)
                ob, c = chain(q_ref[rows, lanes], kb, v_ref[pl.ds(key0, n_keys), lanes], later_m, cs[hd], visible)
                acc = acc + ob
                new_cs.append(c)
                top = jnp.exp(c) if top is None else jnp.maximum(top, jnp.exp(c))
            new_accs.append(acc)
        live = (jnp.max(top) > 0.0).astype(jnp.int32)
        return tuple(new_accs), tuple(new_cs), live

    d0 = pl.multiple_of(qi * TK, TK)
    start = ((jnp.zeros((HALF, LANES), F32),) * SB_PAIRS, (jnp.zeros((HALF, 1), F32),) * SB_HEADS, jnp.int32(1))
    rows_a = lax.broadcasted_iota(jnp.int32, (HALF, HALF), 0)
    cols_a = lax.broadcasted_iota(jnp.int32, (HALF, HALF), 1)
    half_a = sweep(slice(0, HALF), d0, HALF, later[0:HALF, 0:HALF], start, cols_a < rows_a)
    rows_b = lax.broadcasted_iota(jnp.int32, (HALF, TK), 0)
    cols_b = lax.broadcasted_iota(jnp.int32, (HALF, TK), 1)
    half_b = sweep(slice(HALF, TQ), d0, TK, later, start, cols_b < rows_b + HALF)

    def cond(state):
        jj, half_a, half_b = state
        return jnp.logical_and(jj < qi, half_a[2] + half_b[2] > 0)

    def body(state):
        jj, half_a, half_b = state
        key0 = pl.multiple_of((qi - 1 - jj) * TK, TK)

        def step(rows, half):
            return lax.cond(half[2] > 0, lambda: sweep(rows, key0, TK, later, half, None), lambda: half)

        return jj + 1, step(slice(0, HALF), half_a), step(slice(HALF, TQ), half_b)

    _, half_a, half_b = lax.while_loop(cond, body, (jnp.int32(0), half_a, half_b))
    for pr in range(SB_PAIRS):
        o_ref[0:HALF, pr * LANES:(pr + 1) * LANES] = half_a[0][pr]
        o_ref[HALF:TQ, pr * LANES:(pr + 1) * LANES] = half_b[0][pr]


def _stick_break(q, k, v, B, S):
    return pl.pallas_call(
        _stick_break_kernel,
        grid=(B, S // TQ),
        in_specs=[
            pl.BlockSpec((None, TQ, SB_HEADS * LANES), lambda b, i: (b, i, 0)),
            pl.BlockSpec((None, S, SB_WIDTH), lambda b, i: (b, 0, 0)),
            pl.BlockSpec((None, S, SB_HEADS * LANES), lambda b, i: (b, 0, 0)),
        ],
        out_specs=pl.BlockSpec((None, TQ, SB_WIDTH), lambda b, i: (b, i, 0)),
        out_shape=jax.ShapeDtypeStruct((B, S, SB_WIDTH), F32),
        compiler_params=pltpu.CompilerParams(
            dimension_semantics=("arbitrary", "arbitrary"), vmem_limit_bytes=VMEM_LIMIT),
        name="stick_break",
    )(q.reshape(B, S, SB_HEADS * LANES), k.reshape(B, S, SB_WIDTH), v.reshape(B, S, SB_HEADS * LANES))


def _mem_kv_kernel(m_ref, g_ref, wk_ref, wv_ref, k_ref, v_ref):
    m = _rms(m_ref[...], g_ref[...]).astype(BF16)
    k_ref[...] = _dot(m, wk_ref[...]).astype(BF16)
    v_ref[...] = _dot(m, wv_ref[...]).astype(BF16)


def _mem_kv(mem, g, wk, wv):
    B, M, _ = mem.shape
    L = wk.shape[0]
    kv_spec = pl.BlockSpec((None, None, M, D_MODEL), lambda l, b: (l, b, 0, 0))
    w_spec = pl.BlockSpec((None, D_MODEL, D_MODEL), lambda l, b: (l, 0, 0))
    return pl.pallas_call(
        _mem_kv_kernel,
        grid=(L, B),
        in_specs=[
            pl.BlockSpec((None, M, D_MODEL), lambda l, b: (b, 0, 0)),
            pl.BlockSpec((None, 1, D_MODEL), lambda l, b: (l, 0, 0)),
            w_spec, w_spec,
        ],
        out_specs=[kv_spec, kv_spec],
        out_shape=[jax.ShapeDtypeStruct((L, B, M, D_MODEL), BF16)] * 2,
        compiler_params=pltpu.CompilerParams(
            dimension_semantics=("arbitrary", "arbitrary"), vmem_limit_bytes=VMEM_LIMIT),
        name="mem_kv",
    )(mem, g, wk, wv)


def _post_kernel(x_ref, ya_ref, yb_ref, gb_ref, wo_ref, gc_ref, wq_ref, kx_ref, vx_ref, wxo_ref, gm_ref,
                 wrh_ref, wrl_ref, br_ref, x2_ref, hm_ref, info_ref, cnt_ref, run_ref):
    tm = x_ref.shape[0]

    @pl.when(pl.program_id(0) == 0)
    def _():
        run_ref[...] = jnp.zeros_like(run_ref)

    ybn = _rms(yb_ref[...], gb_ref[...]).astype(BF16)
    x1 = x_ref[...] + _dot(jnp.concatenate([ya_ref[...], ybn], axis=-1), wo_ref[...])

    h = _rms(x1, gc_ref[...]).astype(BF16)
    q = (_dot(h, wq_ref[...]) * (1.0 / (X_HEAD_DIM ** 0.5))).astype(BF16)
    heads = []
    for hd in range(X_HEADS):
        cols = slice(hd * X_HEAD_DIM, (hd + 1) * X_HEAD_DIM)
        s = _dot_nt(q[:, cols], kx_ref[:, cols])
        p = jnp.exp(s - jnp.max(s, axis=-1, keepdims=True))
        p = p / jnp.sum(p, axis=-1, keepdims=True)
        heads.append(_dot(p.astype(BF16), vx_ref[:, cols]).astype(BF16))
    x2 = x1 + _dot(jnp.concatenate(heads, axis=-1), wxo_ref[...])
    x2_ref[...] = x2

    hm = _rms(x2, gm_ref[...])
    for c in range(ROW_TILES):
        hm_ref[pl.ds(c, tm, stride=ROW_TILES), :] = hm[:, c * LANES:(c + 1) * LANES]

    hm_hi, hm_lo = _split_bf16(hm)
    logits = _dot(hm_hi, wrh_ref[...]) + _dot(hm_hi, wrl_ref[...]) + _dot(hm_lo, wrh_ref[...]) + br_ref[...]

    lane = lax.broadcasted_iota(jnp.int32, (tm, LANES), 1)
    lanef = lane.astype(F32)
    big = float(LANES)
    is_group = lane < N_GROUPS
    gl = jnp.where(is_group, logits, NEG)
    gmax = jnp.max(gl, axis=-1, keepdims=True)
    gsel = jnp.min(jnp.where(gl == gmax, lanef, big), axis=-1, keepdims=True)
    gden = jnp.sum(jnp.where(is_group, jnp.exp(gl - gmax), 0.0), axis=-1, keepdims=True)
    g_gate = 1.0 / gden
    lo = ROUTE_BASE + EXPERTS_PER_GROUP * gsel
    in_group = (lanef >= lo) & (lanef < lo + EXPERTS_PER_GROUP)
    el = jnp.where(in_group, logits, NEG)
    v1 = jnp.max(el, axis=-1, keepdims=True)
    i1 = jnp.min(jnp.where(el == v1, lanef, big), axis=-1, keepdims=True)
    el2 = jnp.where(lanef == i1, NEG, el)
    v2 = jnp.max(el2, axis=-1, keepdims=True)
    i2 = jnp.min(jnp.where(el2 == v2, lanef, big), axis=-1, keepdims=True)
    t = jnp.exp(v2 - v1)
    den = 1.0 + t
    gate1 = g_gate * (1.0 / den)
    gate2 = g_gate * (t / den)

    hit1 = lanef == i1
    hit2 = lanef == i2
    multi = jnp.where(hit1 | hit2, 1.0, 0.0)
    r_idx = lax.broadcasted_iota(jnp.int32, (tm, tm), 0)
    c_idx = lax.broadcasted_iota(jnp.int32, (tm, tm), 1)
    earlier = jnp.where(c_idx < r_idx, 1.0, 0.0).astype(BF16)
    before = _dot(earlier, multi.astype(BF16)) + run_ref[0:1, :]
    rank1 = jnp.sum(jnp.where(hit1, before, 0.0), axis=-1, keepdims=True)
    rank2 = jnp.sum(jnp.where(hit2, before, 0.0), axis=-1, keepdims=True)
    run_ref[...] = run_ref[...] + jnp.sum(multi, axis=0, keepdims=True)
    cnt_ref[...] = run_ref[...]

    info = jnp.where(lane == 0, i1 - ROUTE_BASE, 0.0)
    info = jnp.where(lane == 1, i2 - ROUTE_BASE, info)
    info = jnp.where(lane == 2, gate1, info)
    info = jnp.where(lane == 3, gate2, info)
    info = jnp.where(lane == 4, rank1, info)
    info = jnp.where(lane == 5, rank2, info)
    info_ref[...] = info


def _post(x2d, ya, yb, gb, wo, gc, wq, kx, vx, wxo, gm, wrh, wrl, br, l, S):
    T = x2d.shape[0]
    M = kx.shape[2]
    grid = (T // TM_POST,)
    row = lambda i: (i, 0)
    lay = lambda i: (l, 0, 0)
    batch = lambda i: (l, (i * TM_POST) // S, 0, 0)
    wspec = pl.BlockSpec((None, D_MODEL, D_MODEL), lay)
    gspec = pl.BlockSpec((None, 1, D_MODEL), lay)
    return pl.pallas_call(
        _post_kernel,
        grid=grid,
        in_specs=[
            pl.BlockSpec((TM_POST, D_MODEL), row),
            pl.BlockSpec((TM_POST, A_WIDTH), row),
            pl.BlockSpec((TM_POST, SB_WIDTH), row),
            pl.BlockSpec((None, 1, SB_WIDTH), lay),
            wspec, gspec, wspec,
            pl.BlockSpec((None, None, M, D_MODEL), batch),
            pl.BlockSpec((None, None, M, D_MODEL), batch),
            wspec, gspec,
            pl.BlockSpec((None, D_MODEL, LANES), lay),
            pl.BlockSpec((None, D_MODEL, LANES), lay),
            pl.BlockSpec((None, 1, LANES), lay),
        ],
        out_specs=[
            pl.BlockSpec((TM_POST, D_MODEL), row),
            pl.BlockSpec((TM_POST * ROW_TILES, LANES), row),
            pl.BlockSpec((TM_POST, LANES), row),
            pl.BlockSpec((SUBLANES, LANES), lambda i: (0, 0)),
        ],
        out_shape=[
            jax.ShapeDtypeStruct((T, D_MODEL), F32),
            jax.ShapeDtypeStruct((T * ROW_TILES, LANES), F32),
            jax.ShapeDtypeStruct((T, LANES), F32),
            jax.ShapeDtypeStruct((SUBLANES, LANES), F32),
        ],
        scratch_shapes=[pltpu.VMEM((SUBLANES, LANES), F32)],
        compiler_params=pltpu.CompilerParams(dimension_semantics=("arbitrary",), vmem_limit_bytes=VMEM_LIMIT),
        name="post",
    )(x2d, ya, yb, gb, wo, gc, wq, kx, vx, wxo, gm, wrh, wrl, br)


def _dispatch_kernel(dest_ref, hm_ref, buf_in_ref, buf_ref, sem):
    del buf_in_ref
    base = pl.program_id(0) * (TD * TOP_K)

    def issue(r, c):
        for k in range(TOP_K):
            _row_copy(hm_ref, r, buf_ref, dest_ref[base + TOP_K * r + k], sem).start(priority=k)
        return c

    lax.fori_loop(0, TD, issue, 0)
    for k in range(TOP_K):
        pltpu.make_async_copy(hm_ref, buf_ref.at[pl.ds(0, TD * ROW_TILES), :], sem).wait()


def _dispatch(dest, hm, buf0):
    T = hm.shape[0] // ROW_TILES
    return pl.pallas_call(
        _dispatch_kernel,
        grid_spec=pltpu.PrefetchScalarGridSpec(
            num_scalar_prefetch=1,
            grid=(T // TD,),
            in_specs=[
                pl.BlockSpec((TD * ROW_TILES, LANES), lambda i, d: (i, 0)),
                pl.BlockSpec(memory_space=pl.ANY),
            ],
            out_specs=pl.BlockSpec(memory_space=pl.ANY),
            scratch_shapes=[pltpu.SemaphoreType.DMA(())],
        ),
        out_shape=jax.ShapeDtypeStruct(buf0.shape, F32),
        input_output_aliases={2: 0},
        compiler_params=pltpu.CompilerParams(dimension_semantics=("arbitrary",), vmem_limit_bytes=VMEM_LIMIT),
        name="dispatch",
    )(dest, hm, buf0)


def _experts_kernel(blk_e_ref, n_used_ref, buf_ref, w1_ref, w3_ref, w2_ref, y_ref, w1b_ref, w3b_ref, w2b_ref):
    i = pl.program_id(0)
    expert = blk_e_ref[i]
    previous = blk_e_ref[jnp.maximum(i - 1, 0)]

    @pl.when(jnp.logical_or(i == 0, expert != previous))
    def _():
        w1b_ref[...] = w1_ref[...].astype(BF16)
        w3b_ref[...] = w3_ref[...].astype(BF16)
        w2b_ref[...] = w2_ref[...].astype(BF16)

    @pl.when(i < n_used_ref[0])
    def _():
        xb = jnp.concatenate(
            [buf_ref[pl.ds(c, BM, stride=ROW_TILES), :] for c in range(ROW_TILES)], axis=-1).astype(BF16)
        h1 = _dot(xb, w1b_ref[...])
        h3 = _dot(xb, w3b_ref[...])
        a = (h1 * jax.nn.sigmoid(h1) * h3).astype(BF16)
        y = _dot(a, w2b_ref[...])
        for c in range(ROW_TILES):
            y_ref[pl.ds(c, BM, stride=ROW_TILES), :] = y[:, c * LANES:(c + 1) * LANES]

    @pl.when(i >= n_used_ref[0])
    def _():
        y_ref[...] = jnp.zeros_like(y_ref)


def _experts(blk_e, n_used, buf, w1, w3, w2, l):
    n_blocks = buf.shape[0] // (BM * ROW_TILES)
    rows = pl.BlockSpec((BM * ROW_TILES, LANES), lambda i, e, n: (i, 0))
    return pl.pallas_call(
        _experts_kernel,
        grid_spec=pltpu.PrefetchScalarGridSpec(
            num_scalar_prefetch=2,
            grid=(n_blocks,),
            in_specs=[
                rows,
                pl.BlockSpec((None, None, D_MODEL, D_EXPERT), lambda i, e, n: (l, e[i], 0, 0)),
                pl.BlockSpec((None, None, D_MODEL, D_EXPERT), lambda i, e, n: (l, e[i], 0, 0)),
                pl.BlockSpec((None, None, D_EXPERT, D_MODEL), lambda i, e, n: (l, e[i], 0, 0)),
            ],
            out_specs=rows,
            scratch_shapes=[
                pltpu.VMEM((D_MODEL, D_EXPERT), BF16),
                pltpu.VMEM((D_MODEL, D_EXPERT), BF16),
                pltpu.VMEM((D_EXPERT, D_MODEL), BF16),
            ],
        ),
        out_shape=jax.ShapeDtypeStruct(buf.shape, F32),
        compiler_params=pltpu.CompilerParams(dimension_semantics=("arbitrary",), vmem_limit_bytes=VMEM_LIMIT),
        name="experts",
    )(blk_e, n_used, buf, w1, w3, w2)


def _combine_final_kernel(dest_ref, x2_ref, info_ref, gf_ref, ybuf_ref, out_ref, rows_ref, sems):
    chunks = _moe_output_chunks(dest_ref, x2_ref, info_ref, ybuf_ref, rows_ref, sems)
    ssq = jnp.zeros((x2_ref.shape[0], 1), F32)
    for xc in chunks:
        ssq = ssq + jnp.sum(xc * xc, axis=-1, keepdims=True)
    inv = lax.rsqrt(ssq * (1.0 / D_MODEL) + EPS)
    for c, xc in enumerate(chunks):
        lanes = slice(c * LANES, (c + 1) * LANES)
        out_ref[:, lanes] = xc * inv * gf_ref[:, lanes]


def _combine_final(dest, x2, info, gf, ybuf):
    T = x2.shape[0]
    row = lambda i, d: (i, 0)
    return pl.pallas_call(
        _combine_final_kernel,
        grid_spec=pltpu.PrefetchScalarGridSpec(
            num_scalar_prefetch=1,
            grid=(T // TE,),
            in_specs=[
                pl.BlockSpec((TE, D_MODEL), row),
                pl.BlockSpec((TE, LANES), row),
                pl.BlockSpec((1, D_MODEL), lambda i, d: (0, 0)),
                pl.BlockSpec(memory_space=pl.ANY),
            ],
            out_specs=pl.BlockSpec((TE, D_MODEL), row),
            scratch_shapes=_moe_gather_scratch(TE),
        ),
        out_shape=jax.ShapeDtypeStruct((T, D_MODEL), F32),
        compiler_params=pltpu.CompilerParams(dimension_semantics=("arbitrary",), vmem_limit_bytes=VMEM_LIMIT),
        name="combine_final",
    )(dest, x2, info, gf, ybuf)


def kernel(x, mem, norm_mix, w_in, v_norm, w_spatial, b_spatial, out_norm_a, out_norm_b, w_out, norm_cross, norm_mem, w_xq, w_xk, w_xv, w_xo, norm_moe, w_group, b_group, w_router, b_router, w1, w3, w2, norm_final):
    B, S, D = x.shape
    L = w_in.shape[0]
    T = B * S
    assert D == D_MODEL and S % TQ == 0 and T % TM_IN == 0 and S % TM_POST == 0 and T % TD == 0 and T % TE == 0

    row3 = lambda a: a.reshape(L, 1, -1)
    w_in_b, w_out_b = w_in.astype(BF16), w_out.astype(BF16)
    w_xq_b, w_xk_b, w_xv_b, w_xo_b = (w.astype(BF16) for w in (w_xq, w_xk, w_xv, w_xo))
    pad = jnp.zeros((L, D, LANES - N_GROUPS - N_EXPERTS), F32)
    w_route = jnp.concatenate([w_group, w_router, pad], axis=-1)
    w_route_hi = w_route.astype(BF16)
    w_route_lo = (w_route - w_route_hi.astype(F32)).astype(BF16)
    b_route = jnp.concatenate([b_group, b_router, pad[:, 0, :]], axis=-1).reshape(L, 1, LANES)
    bs_t = jnp.swapaxes(b_spatial, 1, 2)

    kx, vx = _mem_kv(mem, row3(norm_mem), w_xk_b, w_xv_b)

    n_slots = T * TOP_K + N_EXPERTS * BM
    n_blocks = n_slots // BM
    xs = x.reshape(T, D)
    moe = None
    for l in range(L):
        mix_args = (row3(norm_mix), w_in_b, row3(v_norm), w_spatial, bs_t, row3(out_norm_a), l)
        if moe is None:
            q, k, v, ya = _mix_in(xs, *mix_args)
        else:
            xs, q, k, v, ya = _moe_mix_in(*moe, *mix_args)
        yb = _stick_break(q, k, v, B, S).reshape(T, SB_WIDTH)
        x2, hm, info, cnt = _post(xs, ya, yb, row3(out_norm_b), w_out_b, row3(norm_cross), w_xq_b, kx, vx, w_xo_b,
                                  row3(norm_moe), w_route_hi, w_route_lo, b_route, l, S)
        eid = info[:, 0:TOP_K].astype(jnp.int32)
        rank = info[:, 4:4 + TOP_K].astype(jnp.int32)
        counts = cnt[0, ROUTE_BASE:ROUTE_BASE + N_EXPERTS].astype(jnp.int32)
        padded = ((counts + BM - 1) // BM) * BM
        seg_end = jnp.cumsum(padded)
        seg_start = seg_end - padded
        expert_ids = jnp.arange(N_EXPERTS, dtype=jnp.int32)
        dest = (rank + jnp.sum(jnp.where(eid[..., None] == expert_ids, seg_start, 0), axis=-1)).reshape(-1)
        blk_row = jnp.arange(n_blocks, dtype=jnp.int32) * BM
        blk_e = jnp.minimum(
            jnp.sum((seg_end[None, :] <= blk_row[:, None]).astype(jnp.int32), axis=1), N_EXPERTS - 1)
        n_used = (seg_end[-1:] // BM).astype(jnp.int32)
        buf = _dispatch(dest, hm, jnp.zeros((n_slots * ROW_TILES, LANES), F32))
        ybuf = _experts(blk_e, n_used, buf, w1, w3, w2, l)
        moe = (dest, x2, info, ybuf)
    dest, x2, info, ybuf = moe
    return _combine_final(dest, x2, info, norm_final.reshape(1, D), ybuf).reshape(B, S, D)
```

```python
import jax
import jax.numpy as jnp
from jax import lax
from jax.experimental import pallas as pl
from jax.experimental.pallas import tpu as pltpu

F32 = jnp.float32
BF16 = jnp.bfloat16

D_MODEL = 1024
A_WIDTH = 512
A_GROUPS = 4
A_CH = 128
CHUNK = 128
SB_WIDTH = 512
SB_HEADS = 8
SB_HEAD_DIM = 64
SB_PAIRS = SB_HEADS // 2
IN_WIDTH = 2 * A_WIDTH + 3 * SB_WIDTH
X_HEADS = 4
X_HEAD_DIM = 256
N_GROUPS = 4
EXPERTS_PER_GROUP = 8
N_EXPERTS = 32
TOP_K = 2
D_EXPERT = 512
EPS = 1e-6

LANES = 128
SUBLANES = 8
ROW_TILES = D_MODEL // LANES

TM_IN = 512
TQ = 256
TK = 256
HALF = TQ // 2
TM_POST = 512
TD = 512
BM = 512
TE = 256
ROUTE_BASE = N_GROUPS
NEG = -1e30
LOG2E = 1.4426950408889634

VMEM_LIMIT = 52 * 1024 * 1024


def _rms(x, g):
    return x * lax.rsqrt(jnp.mean(x * x, axis=-1, keepdims=True) + EPS) * g


def _dot(a, b):
    return jnp.dot(a, b, preferred_element_type=F32)


def _dot_nt(a, b):
    return lax.dot_general(a, b, (((1,), (1,)), ((), ())), preferred_element_type=F32)


def _split_bf16(x):
    hi = x.astype(BF16)
    lo = (x - hi.astype(F32)).astype(BF16)
    return hi, lo


def _row_copy(src, src_row, dst, dst_row, sem):
    return pltpu.make_async_copy(
        src.at[pl.ds(pl.multiple_of(src_row * ROW_TILES, ROW_TILES), ROW_TILES), :],
        dst.at[pl.ds(pl.multiple_of(dst_row * ROW_TILES, ROW_TILES), ROW_TILES), :],
        sem)


def _moe_output_chunks(dest_ref, x2_ref, info_ref, ybuf_ref, rows_ref, sems):
    tm = x2_ref.shape[0]
    i = pl.program_id(0)

    def start_gather(step, slot):
        base = step * (tm * TOP_K)

        def issue(r, c):
            for k in range(TOP_K):
                _row_copy(ybuf_ref, dest_ref[base + TOP_K * r + k], rows_ref.at[slot], k * tm + r,
                          sems.at[slot]).start(priority=1)
            return c

        lax.fori_loop(0, tm, issue, 0)

    @pl.when(i == 0)
    def _():
        start_gather(0, 0)

    @pl.when(i + 1 < pl.num_programs(0))
    def _():
        start_gather(i + 1, (i + 1) % 2)

    slot = i % 2
    rows = rows_ref.at[slot]
    pltpu.make_async_copy(ybuf_ref.at[pl.ds(0, TOP_K * tm * ROW_TILES), :], rows, sems.at[slot]).wait()

    info = info_ref[...]
    gate1 = info[:, 2:3]
    gate2 = info[:, 3:4]
    chunks = []
    for c in range(ROW_TILES):
        y1 = rows[pl.ds(c, tm, stride=ROW_TILES), :]
        y2 = rows[pl.ds(tm * ROW_TILES + c, tm, stride=ROW_TILES), :]
        chunks.append(x2_ref[:, c * LANES:(c + 1) * LANES] + (gate1 * y1 + gate2 * y2))
    return chunks


def _moe_gather_scratch(tm):
    return [pltpu.VMEM((2, TOP_K * tm * ROW_TILES, LANES), F32), pltpu.SemaphoreType.DMA((2,))]


def _mix_in_body(x, g_ref, w_ref, vg_ref, ws_ref, bs_ref, ga_ref, q_ref, k_ref, v_ref, ya_ref):
    tm = x.shape[0]
    h = _rms(x, g_ref[...]).astype(BF16)
    z = _dot(h, w_ref[...])

    t_idx = lax.broadcasted_iota(jnp.int32, (CHUNK, CHUNK), 0)
    s_idx = lax.broadcasted_iota(jnp.int32, (CHUNK, CHUNK), 1)
    causal = s_idx <= t_idx
    parts = []
    ssq = jnp.zeros((tm, 1), F32)
    for g in range(A_GROUPS):
        lanes = slice(g * A_CH, (g + 1) * A_CH)
        u = jax.nn.gelu(z[:, g * A_CH:(g + 1) * A_CH])
        vg = jax.nn.gelu(z[:, A_WIDTH + g * A_CH:A_WIDTH + (g + 1) * A_CH])
        vn = _rms(vg, vg_ref[:, lanes]).astype(BF16)
        ws = jnp.where(causal, ws_ref[g], 0.0).astype(BF16)
        bias = bs_ref[:, g:g + 1]
        mixed = jnp.concatenate(
            [_dot(ws, vn[c * CHUNK:(c + 1) * CHUNK, :]) + bias for c in range(tm // CHUNK)], axis=0)
        ya = u * mixed
        parts.append(ya)
        ssq = ssq + jnp.sum(ya * ya, axis=-1, keepdims=True)
    inv = lax.rsqrt(ssq * (1.0 / A_WIDTH) + EPS)
    for g in range(A_GROUPS):
        lanes = slice(g * A_CH, (g + 1) * A_CH)
        ya_ref[:, lanes] = (parts[g] * inv * ga_ref[:, lanes]).astype(BF16)

    lane = lax.broadcasted_iota(jnp.int32, (1, LANES), 1)
    q0 = 2 * A_WIDTH
    k0 = q0 + SB_WIDTH
    v0 = k0 + SB_WIDTH
    scale = 1.0 / (SB_HEAD_DIM ** 0.5)
    for hd in range(SB_HEADS):
        pair = hd // 2
        keep = (lane < SB_HEAD_DIM) if hd % 2 == 0 else (lane >= SB_HEAD_DIM)
        qs = z[:, q0 + pair * LANES:q0 + (pair + 1) * LANES] * scale
        vs = z[:, v0 + pair * LANES:v0 + (pair + 1) * LANES]
        q_ref[:, hd * LANES:(hd + 1) * LANES] = jnp.where(keep, qs, 0.0).astype(BF16)
        v_ref[:, hd * LANES:(hd + 1) * LANES] = jnp.where(keep, vs, 0.0).astype(BF16)
    k_ref[...] = z[:, k0:v0].astype(BF16)


def _mix_in_kernel(x_ref, *refs):
    _mix_in_body(x_ref[...], *refs)


def _moe_mix_in_kernel(dest_ref, x2_ref, info_ref, ybuf_ref, g_ref, w_ref, vg_ref, ws_ref, bs_ref, ga_ref,
                       x_ref, q_ref, k_ref, v_ref, ya_ref, rows_ref, sems):
    x = jnp.concatenate(_moe_output_chunks(dest_ref, x2_ref, info_ref, ybuf_ref, rows_ref, sems), axis=-1)
    x_ref[...] = x
    _mix_in_body(x, g_ref, w_ref, vg_ref, ws_ref, bs_ref, ga_ref, q_ref, k_ref, v_ref, ya_ref)


def _mix_in_specs(l, index):
    lay = lambda *a: (l, 0, 0)
    in_specs = [
        pl.BlockSpec((None, 1, D_MODEL), lay),
        pl.BlockSpec((None, D_MODEL, IN_WIDTH), lay),
        pl.BlockSpec((None, 1, A_WIDTH), lay),
        pl.BlockSpec((None, A_GROUPS, CHUNK, CHUNK), lambda *a: (l, 0, 0, 0)),
        pl.BlockSpec((None, CHUNK, A_GROUPS), lay),
        pl.BlockSpec((None, 1, A_WIDTH), lay),
    ]
    out_specs = [
        pl.BlockSpec((TM_IN, SB_HEADS * LANES), index),
        pl.BlockSpec((TM_IN, SB_WIDTH), index),
        pl.BlockSpec((TM_IN, SB_HEADS * LANES), index),
        pl.BlockSpec((TM_IN, A_WIDTH), index),
    ]
    return in_specs, out_specs


def _mix_in_out_shapes(T):
    return [
        jax.ShapeDtypeStruct((T, SB_HEADS * LANES), BF16),
        jax.ShapeDtypeStruct((T, SB_WIDTH), BF16),
        jax.ShapeDtypeStruct((T, SB_HEADS * LANES), BF16),
        jax.ShapeDtypeStruct((T, A_WIDTH), BF16),
    ]


def _mix_in(x2d, g, w_in, vg, ws, bs_t, ga, l):
    T = x2d.shape[0]
    row = lambda i: (i, 0)
    in_specs, out_specs = _mix_in_specs(l, row)
    return pl.pallas_call(
        _mix_in_kernel,
        grid=(T // TM_IN,),
        in_specs=[pl.BlockSpec((TM_IN, D_MODEL), row)] + in_specs,
        out_specs=out_specs,
        out_shape=_mix_in_out_shapes(T),
        compiler_params=pltpu.CompilerParams(dimension_semantics=("arbitrary",), vmem_limit_bytes=VMEM_LIMIT),
        name="mix_in",
    )(x2d, g, w_in, vg, ws, bs_t, ga)


def _moe_mix_in(dest, x2, info, ybuf, g, w_in, vg, ws, bs_t, ga, l):
    T = x2.shape[0]
    row = lambda i, d: (i, 0)
    in_specs, out_specs = _mix_in_specs(l, row)
    return pl.pallas_call(
        _moe_mix_in_kernel,
        grid_spec=pltpu.PrefetchScalarGridSpec(
            num_scalar_prefetch=1,
            grid=(T // TM_IN,),
            in_specs=[
                pl.BlockSpec((TM_IN, D_MODEL), row),
                pl.BlockSpec((TM_IN, LANES), row),
                pl.BlockSpec(memory_space=pl.ANY),
            ] + in_specs,
            out_specs=[pl.BlockSpec((TM_IN, D_MODEL), row)] + out_specs,
            scratch_shapes=_moe_gather_scratch(TM_IN),
        ),
        out_shape=[jax.ShapeDtypeStruct((T, D_MODEL), F32)] + _mix_in_out_shapes(T),
        compiler_params=pltpu.CompilerParams(dimension_semantics=("arbitrary",), vmem_limit_bytes=VMEM_LIMIT),
        name="moe_mix_in",
    )(dest, x2, info, ybuf, g, w_in, vg, ws, bs_t, ga)


def _stick_break_kernel(q_ref, k_ref, v_ref, o_ref):
    qi = pl.program_id(1)
    r_idx = lax.broadcasted_iota(jnp.int32, (TK, TK), 0)
    c_idx = lax.broadcasted_iota(jnp.int32, (TK, TK), 1)
    later = jnp.where(c_idx < r_idx, -1.0, 0.0).astype(BF16)

    def sweep(rows, key0, n_keys, later_m, half, visible):
        accs, cs, _ = half
        n_rows = rows.stop - rows.start
        sps, lszs = [], []
        for hd in range(SB_HEADS):
            kb = k_ref[pl.ds(key0, n_keys), (hd // 2) * LANES:(hd // 2 + 1) * LANES]
            z = _dot_nt(q_ref[rows, hd * LANES:(hd + 1) * LANES], kb)
            sp = jnp.maximum(z, 0.0) + jnp.log(1.0 + jnp.exp2(jnp.abs(z) * (-LOG2E)))
            lszs.append(z - sp)
            sps.append(sp if visible is None else jnp.where(visible, sp, 0.0))
        after = _dot(jnp.concatenate([sp.astype(BF16) for sp in sps], axis=0), later_m)
        new_accs, new_cs = [], []
        top = None
        for pr in range(SB_PAIRS):
            acc = accs[pr]
            for hd in (2 * pr, 2 * pr + 1):
                p = jnp.exp(lszs[hd] + after[hd * n_rows:(hd + 1) * n_rows, :])
                if visible is not None:
                    p = jnp.where(visible, p, 0.0)
                ob = _dot(p.astype(BF16), v_ref[pl.ds(key0, n_keys), hd * LANES:(hd + 1) * LANES])
                acc = acc + ob * jnp.exp(cs[hd])
                c = cs[hd] - jnp.sum(sps[hd], axis=-1, keepdims=True)
                new_cs.append(c)
                top = jnp.exp(c) if top is None else jnp.maximum(top, jnp.exp(c))
            new_accs.append(acc)
        live = (jnp.max(top) > 0.0).astype(jnp.int32)
        return tuple(new_accs), tuple(new_cs), live

    d0 = pl.multiple_of(qi * TK, TK)
    start = ((jnp.zeros((HALF, LANES), F32),) * SB_PAIRS, (jnp.zeros((HALF, 1), F32),) * SB_HEADS, jnp.int32(1))
    rows_a = lax.broadcasted_iota(jnp.int32, (HALF, HALF), 0)
    cols_a = lax.broadcasted_iota(jnp.int32, (HALF, HALF), 1)
    half_a = sweep(slice(0, HALF), d0, HALF, later[0:HALF, 0:HALF], start, cols_a < rows_a)
    rows_b = lax.broadcasted_iota(jnp.int32, (HALF, TK), 0)
    cols_b = lax.broadcasted_iota(jnp.int32, (HALF, TK), 1)
    half_b = sweep(slice(HALF, TQ), d0, TK, later, start, cols_b < rows_b + HALF)

    def cond(state):
        jj, half_a, half_b = state
        return jnp.logical_and(jj < qi, half_a[2] + half_b[2] > 0)

    def body(state):
        jj, half_a, half_b = state
        key0 = pl.multiple_of((qi - 1 - jj) * TK, TK)

        def step(rows, half):
            return lax.cond(half[2] > 0, lambda: sweep(rows, key0, TK, later, half, None), lambda: half)

        return jj + 1, step(slice(0, HALF), half_a), step(slice(HALF, TQ), half_b)

    _, half_a, half_b = lax.while_loop(cond, body, (jnp.int32(0), half_a, half_b))
    for pr in range(SB_PAIRS):
        o_ref[0:HALF, pr * LANES:(pr + 1) * LANES] = half_a[0][pr]
        o_ref[HALF:TQ, pr * LANES:(pr + 1) * LANES] = half_b[0][pr]


def _stick_break(q, k, v, B, S):
    return pl.pallas_call(
        _stick_break_kernel,
        grid=(B, S // TQ),
        in_specs=[
            pl.BlockSpec((None, TQ, SB_HEADS * LANES), lambda b, i: (b, i, 0)),
            pl.BlockSpec((None, S, SB_WIDTH), lambda b, i: (b, 0, 0)),
            pl.BlockSpec((None, S, SB_HEADS * LANES), lambda b, i: (b, 0, 0)),
        ],
        out_specs=pl.BlockSpec((None, TQ, SB_WIDTH), lambda b, i: (b, i, 0)),
        out_shape=jax.ShapeDtypeStruct((B, S, SB_WIDTH), F32),
        compiler_params=pltpu.CompilerParams(
            dimension_semantics=("arbitrary", "arbitrary"), vmem_limit_bytes=VMEM_LIMIT),
        name="stick_break",
    )(q.reshape(B, S, SB_HEADS * LANES), k.reshape(B, S, SB_WIDTH), v.reshape(B, S, SB_HEADS * LANES))


def _mem_kv_kernel(m_ref, g_ref, wk_ref, wv_ref, k_ref, v_ref):
    m = _rms(m_ref[...], g_ref[...]).astype(BF16)
    k_ref[...] = _dot(m, wk_ref[...]).astype(BF16)
    v_ref[...] = _dot(m, wv_ref[...]).astype(BF16)


def _mem_kv(mem, g, wk, wv):
    B, M, _ = mem.shape
    L = wk.shape[0]
    kv_spec = pl.BlockSpec((None, None, M, D_MODEL), lambda l, b: (l, b, 0, 0))
    w_spec = pl.BlockSpec((None, D_MODEL, D_MODEL), lambda l, b: (l, 0, 0))
    return pl.pallas_call(
        _mem_kv_kernel,
        grid=(L, B),
        in_specs=[
            pl.BlockSpec((None, M, D_MODEL), lambda l, b: (b, 0, 0)),
            pl.BlockSpec((None, 1, D_MODEL), lambda l, b: (l, 0, 0)),
            w_spec, w_spec,
        ],
        out_specs=[kv_spec, kv_spec],
        out_shape=[jax.ShapeDtypeStruct((L, B, M, D_MODEL), BF16)] * 2,
        compiler_params=pltpu.CompilerParams(
            dimension_semantics=("arbitrary", "arbitrary"), vmem_limit_bytes=VMEM_LIMIT),
        name="mem_kv",
    )(mem, g, wk, wv)


def _post_kernel(x_ref, ya_ref, yb_ref, gb_ref, wo_ref, gc_ref, wq_ref, kx_ref, vx_ref, wxo_ref, gm_ref,
                 wr_ref, br_ref, x2_ref, hm_ref, info_ref, cnt_ref, run_ref):
    tm = x_ref.shape[0]

    @pl.when(pl.program_id(0) == 0)
    def _():
        run_ref[...] = jnp.zeros_like(run_ref)

    ybn = _rms(yb_ref[...], gb_ref[...]).astype(BF16)
    x1 = x_ref[...] + _dot(jnp.concatenate([ya_ref[...], ybn], axis=-1), wo_ref[...])

    h = _rms(x1, gc_ref[...]).astype(BF16)
    q = (_dot(h, wq_ref[...]) * (1.0 / (X_HEAD_DIM ** 0.5))).astype(BF16)
    heads = []
    for hd in range(X_HEADS):
        cols = slice(hd * X_HEAD_DIM, (hd + 1) * X_HEAD_DIM)
        s = _dot_nt(q[:, cols], kx_ref[:, cols])
        p = jnp.exp(s - jnp.max(s, axis=-1, keepdims=True))
        p = p / jnp.sum(p, axis=-1, keepdims=True)
        heads.append(_dot(p.astype(BF16), vx_ref[:, cols]).astype(BF16))
    x2 = x1 + _dot(jnp.concatenate(heads, axis=-1), wxo_ref[...])
    x2_ref[...] = x2

    hm = _rms(x2, gm_ref[...])
    for c in range(ROW_TILES):
        hm_ref[pl.ds(c, tm, stride=ROW_TILES), :] = hm[:, c * LANES:(c + 1) * LANES]

    hm_hi, hm_lo = _split_bf16(hm)
    both = _dot(hm_hi, wr_ref[...])
    logits = both[:, 0:LANES] + both[:, LANES:2 * LANES] + _dot(hm_lo, wr_ref[:, 0:LANES]) + br_ref[...]

    lane = lax.broadcasted_iota(jnp.int32, (tm, LANES), 1)
    lanef = lane.astype(F32)
    big = float(LANES)
    is_group = lane < N_GROUPS
    gl = jnp.where(is_group, logits, NEG)
    gmax = jnp.max(gl, axis=-1, keepdims=True)
    gsel = jnp.min(jnp.where(gl == gmax, lanef, big), axis=-1, keepdims=True)
    gden = jnp.sum(jnp.where(is_group, jnp.exp(gl - gmax), 0.0), axis=-1, keepdims=True)
    g_gate = 1.0 / gden
    lo = ROUTE_BASE + EXPERTS_PER_GROUP * gsel
    in_group = (lanef >= lo) & (lanef < lo + EXPERTS_PER_GROUP)
    el = jnp.where(in_group, logits, NEG)
    v1 = jnp.max(el, axis=-1, keepdims=True)
    i1 = jnp.min(jnp.where(el == v1, lanef, big), axis=-1, keepdims=True)
    el2 = jnp.where(lanef == i1, NEG, el)
    v2 = jnp.max(el2, axis=-1, keepdims=True)
    i2 = jnp.min(jnp.where(el2 == v2, lanef, big), axis=-1, keepdims=True)
    t = jnp.exp(v2 - v1)
    den = 1.0 + t
    gate1 = g_gate * (1.0 / den)
    gate2 = g_gate * (t / den)

    hit1 = lanef == i1
    hit2 = lanef == i2
    multi = jnp.where(hit1 | hit2, 1.0, 0.0)
    r_idx = lax.broadcasted_iota(jnp.int32, (tm, tm), 0)
    c_idx = lax.broadcasted_iota(jnp.int32, (tm, tm), 1)
    earlier = jnp.where(c_idx < r_idx, 1.0, 0.0).astype(BF16)
    before = _dot(earlier, multi.astype(BF16)) + run_ref[0:1, :]
    rank1 = jnp.sum(jnp.where(hit1, before, 0.0), axis=-1, keepdims=True)
    rank2 = jnp.sum(jnp.where(hit2, before, 0.0), axis=-1, keepdims=True)
    run_ref[...] = run_ref[...] + jnp.sum(multi, axis=0, keepdims=True)
    cnt_ref[...] = run_ref[...]

    info = jnp.where(lane == 0, i1 - ROUTE_BASE, 0.0)
    info = jnp.where(lane == 1, i2 - ROUTE_BASE, info)
    info = jnp.where(lane == 2, gate1, info)
    info = jnp.where(lane == 3, gate2, info)
    info = jnp.where(lane == 4, rank1, info)
    info = jnp.where(lane == 5, rank2, info)
    info_ref[...] = info


def _post(x2d, ya, yb, gb, wo, gc, wq, kx, vx, wxo, gm, wr, br, l, S):
    T = x2d.shape[0]
    M = kx.shape[2]
    grid = (T // TM_POST,)
    row = lambda i: (i, 0)
    lay = lambda i: (l, 0, 0)
    batch = lambda i: (l, (i * TM_POST) // S, 0, 0)
    wspec = pl.BlockSpec((None, D_MODEL, D_MODEL), lay)
    gspec = pl.BlockSpec((None, 1, D_MODEL), lay)
    return pl.pallas_call(
        _post_kernel,
        grid=grid,
        in_specs=[
            pl.BlockSpec((TM_POST, D_MODEL), row),
            pl.BlockSpec((TM_POST, A_WIDTH), row),
            pl.BlockSpec((TM_POST, SB_WIDTH), row),
            pl.BlockSpec((None, 1, SB_WIDTH), lay),
            wspec, gspec, wspec,
            pl.BlockSpec((None, None, M, D_MODEL), batch),
            pl.BlockSpec((None, None, M, D_MODEL), batch),
            wspec, gspec,
            pl.BlockSpec((None, D_MODEL, 2 * LANES), lay),
            pl.BlockSpec((None, 1, LANES), lay),
        ],
        out_specs=[
            pl.BlockSpec((TM_POST, D_MODEL), row),
            pl.BlockSpec((TM_POST * ROW_TILES, LANES), row),
            pl.BlockSpec((TM_POST, LANES), row),
            pl.BlockSpec((SUBLANES, LANES), lambda i: (0, 0)),
        ],
        out_shape=[
            jax.ShapeDtypeStruct((T, D_MODEL), F32),
            jax.ShapeDtypeStruct((T * ROW_TILES, LANES), F32),
            jax.ShapeDtypeStruct((T, LANES), F32),
            jax.ShapeDtypeStruct((SUBLANES, LANES), F32),
        ],
        scratch_shapes=[pltpu.VMEM((SUBLANES, LANES), F32)],
        compiler_params=pltpu.CompilerParams(dimension_semantics=("arbitrary",), vmem_limit_bytes=VMEM_LIMIT),
        name="post",
    )(x2d, ya, yb, gb, wo, gc, wq, kx, vx, wxo, gm, wr, br)


def _dispatch_kernel(dest_ref, hm_ref, buf_in_ref, buf_ref, sem):
    del buf_in_ref
    base = pl.program_id(0) * (TD * TOP_K)

    def issue(r, c):
        for k in range(TOP_K):
            _row_copy(hm_ref, r, buf_ref, dest_ref[base + TOP_K * r + k], sem).start(priority=k)
        return c

    lax.fori_loop(0, TD, issue, 0)
    for k in range(TOP_K):
        pltpu.make_async_copy(hm_ref, buf_ref.at[pl.ds(0, TD * ROW_TILES), :], sem).wait()


def _dispatch(dest, hm, buf0):
    T = hm.shape[0] // ROW_TILES
    return pl.pallas_call(
        _dispatch_kernel,
        grid_spec=pltpu.PrefetchScalarGridSpec(
            num_scalar_prefetch=1,
            grid=(T // TD,),
            in_specs=[
                pl.BlockSpec((TD * ROW_TILES, LANES), lambda i, d: (i, 0)),
                pl.BlockSpec(memory_space=pl.ANY),
            ],
            out_specs=pl.BlockSpec(memory_space=pl.ANY),
            scratch_shapes=[pltpu.SemaphoreType.DMA(())],
        ),
        out_shape=jax.ShapeDtypeStruct(buf0.shape, F32),
        input_output_aliases={2: 0},
        compiler_params=pltpu.CompilerParams(dimension_semantics=("arbitrary",), vmem_limit_bytes=VMEM_LIMIT),
        name="dispatch",
    )(dest, hm, buf0)


def _experts_kernel(blk_e_ref, n_used_ref, buf_ref, w1_ref, w3_ref, w2_ref, y_ref, w1b_ref, w3b_ref, w2b_ref):
    i = pl.program_id(0)
    expert = blk_e_ref[i]
    previous = blk_e_ref[jnp.maximum(i - 1, 0)]

    @pl.when(jnp.logical_or(i == 0, expert != previous))
    def _():
        w1b_ref[...] = w1_ref[...].astype(BF16)
        w3b_ref[...] = w3_ref[...].astype(BF16)
        w2b_ref[...] = w2_ref[...].astype(BF16)

    @pl.when(i < n_used_ref[0])
    def _():
        xb = jnp.concatenate(
            [buf_ref[pl.ds(c, BM, stride=ROW_TILES), :] for c in range(ROW_TILES)], axis=-1).astype(BF16)
        h1 = _dot(xb, w1b_ref[...])
        h3 = _dot(xb, w3b_ref[...])
        a = (h1 * jax.nn.sigmoid(h1) * h3).astype(BF16)
        y = _dot(a, w2b_ref[...])
        for c in range(ROW_TILES):
            y_ref[pl.ds(c, BM, stride=ROW_TILES), :] = y[:, c * LANES:(c + 1) * LANES]

    @pl.when(i >= n_used_ref[0])
    def _():
        y_ref[...] = jnp.zeros_like(y_ref)


def _experts(blk_e, n_used, buf, w1, w3, w2, l):
    n_blocks = buf.shape[0] // (BM * ROW_TILES)
    rows = pl.BlockSpec((BM * ROW_TILES, LANES), lambda i, e, n: (i, 0))
    return pl.pallas_call(
        _experts_kernel,
        grid_spec=pltpu.PrefetchScalarGridSpec(
            num_scalar_prefetch=2,
            grid=(n_blocks,),
            in_specs=[
                rows,
                pl.BlockSpec((None, None, D_MODEL, D_EXPERT), lambda i, e, n: (l, e[i], 0, 0)),
                pl.BlockSpec((None, None, D_MODEL, D_EXPERT), lambda i, e, n: (l, e[i], 0, 0)),
                pl.BlockSpec((None, None, D_EXPERT, D_MODEL), lambda i, e, n: (l, e[i], 0, 0)),
            ],
            out_specs=rows,
            scratch_shapes=[
                pltpu.VMEM((D_MODEL, D_EXPERT), BF16),
                pltpu.VMEM((D_MODEL, D_EXPERT), BF16),
                pltpu.VMEM((D_EXPERT, D_MODEL), BF16),
            ],
        ),
        out_shape=jax.ShapeDtypeStruct(buf.shape, F32),
        compiler_params=pltpu.CompilerParams(dimension_semantics=("arbitrary",), vmem_limit_bytes=VMEM_LIMIT),
        name="experts",
    )(blk_e, n_used, buf, w1, w3, w2)


def _combine_final_kernel(dest_ref, x2_ref, info_ref, gf_ref, ybuf_ref, out_ref, rows_ref, sems):
    chunks = _moe_output_chunks(dest_ref, x2_ref, info_ref, ybuf_ref, rows_ref, sems)
    ssq = jnp.zeros((x2_ref.shape[0], 1), F32)
    for xc in chunks:
        ssq = ssq + jnp.sum(xc * xc, axis=-1, keepdims=True)
    inv = lax.rsqrt(ssq * (1.0 / D_MODEL) + EPS)
    for c, xc in enumerate(chunks):
        lanes = slice(c * LANES, (c + 1) * LANES)
        out_ref[:, lanes] = xc * inv * gf_ref[:, lanes]


def _combine_final(dest, x2, info, gf, ybuf):
    T = x2.shape[0]
    row = lambda i, d: (i, 0)
    return pl.pallas_call(
        _combine_final_kernel,
        grid_spec=pltpu.PrefetchScalarGridSpec(
            num_scalar_prefetch=1,
            grid=(T // TE,),
            in_specs=[
                pl.BlockSpec((TE, D_MODEL), row),
                pl.BlockSpec((TE, LANES), row),
                pl.BlockSpec((1, D_MODEL), lambda i, d: (0, 0)),
                pl.BlockSpec(memory_space=pl.ANY),
            ],
            out_specs=pl.BlockSpec((TE, D_MODEL), row),
            scratch_shapes=_moe_gather_scratch(TE),
        ),
        out_shape=jax.ShapeDtypeStruct((T, D_MODEL), F32),
        compiler_params=pltpu.CompilerParams(dimension_semantics=("arbitrary",), vmem_limit_bytes=VMEM_LIMIT),
        name="combine_final",
    )(dest, x2, info, gf, ybuf)


def kernel(x, mem, norm_mix, w_in, v_norm, w_spatial, b_spatial, out_norm_a, out_norm_b, w_out, norm_cross, norm_mem, w_xq, w_xk, w_xv, w_xo, norm_moe, w_group, b_group, w_router, b_router, w1, w3, w2, norm_final):
    B, S, D = x.shape
    L = w_in.shape[0]
    T = B * S
    assert D == D_MODEL and S % TQ == 0 and T % TM_IN == 0 and S % TM_POST == 0 and T % TD == 0 and T % TE == 0

    row3 = lambda a: a.reshape(L, 1, -1)
    w_in_b, w_out_b = w_in.astype(BF16), w_out.astype(BF16)
    w_xq_b, w_xk_b, w_xv_b, w_xo_b = (w.astype(BF16) for w in (w_xq, w_xk, w_xv, w_xo))
    pad = jnp.zeros((L, D, LANES - N_GROUPS - N_EXPERTS), F32)
    w_route = jnp.concatenate([w_group, w_router, pad], axis=-1)
    w_route_hi = w_route.astype(BF16)
    w_route_lo = (w_route - w_route_hi.astype(F32)).astype(BF16)
    w_route_b = jnp.concatenate([w_route_hi, w_route_lo], axis=-1)
    b_route = jnp.concatenate([b_group, b_router, pad[:, 0, :]], axis=-1).reshape(L, 1, LANES)
    bs_t = jnp.swapaxes(b_spatial, 1, 2)

    kx, vx = _mem_kv(mem, row3(norm_mem), w_xk_b, w_xv_b)

    n_slots = T * TOP_K + N_EXPERTS * BM
    n_blocks = n_slots // BM
    xs = x.reshape(T, D)
    moe = None
    for l in range(L):
        mix_args = (row3(norm_mix), w_in_b, row3(v_norm), w_spatial, bs_t, row3(out_norm_a), l)
        if moe is None:
            q, k, v, ya = _mix_in(xs, *mix_args)
        else:
            xs, q, k, v, ya = _moe_mix_in(*moe, *mix_args)
        yb = _stick_break(q, k, v, B, S).reshape(T, SB_WIDTH)
        x2, hm, info, cnt = _post(xs, ya, yb, row3(out_norm_b), w_out_b, row3(norm_cross), w_xq_b, kx, vx, w_xo_b,
                                  row3(norm_moe), w_route_b, b_route, l, S)
        eid = info[:, 0:TOP_K].astype(jnp.int32)
        rank = info[:, 4:4 + TOP_K].astype(jnp.int32)
        counts = cnt[0, ROUTE_BASE:ROUTE_BASE + N_EXPERTS].astype(jnp.int32)
        padded = ((counts + BM - 1) // BM) * BM
        seg_end = jnp.cumsum(padded)
        seg_start = seg_end - padded
        expert_ids = jnp.arange(N_EXPERTS, dtype=jnp.int32)
        dest = (rank + jnp.sum(jnp.where(eid[..., None] == expert_ids, seg_start, 0), axis=-1)).reshape(-1)
        blk_row = jnp.arange(n_blocks, dtype=jnp.int32) * BM
        blk_e = jnp.minimum(
            jnp.sum((seg_end[None, :] <= blk_row[:, None]).astype(jnp.int32), axis=1), N_EXPERTS - 1)
        n_used = (seg_end[-1:] // BM).astype(jnp.int32)
        buf = _dispatch(dest, hm, jnp.zeros((n_slots * ROW_TILES, LANES), F32))
        ybuf = _experts(blk_e, n_used, buf, w1, w3, w2, l)
        moe = (dest, x2, info, ybuf)
    dest, x2, info, ybuf = moe
    return _combine_final(dest, x2, info, norm_final.reshape(1, D), ybuf).reshape(B, S, D)
```

```python
import jax
import jax.numpy as jnp
from jax import lax
from jax.experimental import pallas as pl
from jax.experimental.pallas import tpu as pltpu

F32 = jnp.float32
BF16 = jnp.bfloat16

D_MODEL = 1024
A_WIDTH = 512
A_GROUPS = 4
A_CH = 128
CHUNK = 128
SB_WIDTH = 512
SB_HEADS = 8
SB_HEAD_DIM = 64
SB_PAIRS = SB_HEADS // 2
IN_WIDTH = 2 * A_WIDTH + 3 * SB_WIDTH
X_HEADS = 4
X_HEAD_DIM = 256
N_GROUPS = 4
EXPERTS_PER_GROUP = 8
N_EXPERTS = 32
TOP_K = 2
D_EXPERT = 512
EPS = 1e-6

LANES = 128
SUBLANES = 8
ROW_TILES = D_MODEL // LANES

TM_IN = 512
TQ = 256
TK = 256
HALF = TQ // 2
TM_POST = 512
TD = 512
BM = 512
TE = 256
ROUTE_BASE = N_GROUPS
NEG = -1e30
LOG2E = 1.4426950408889634

VMEM_LIMIT = 52 * 1024 * 1024


def _rms(x, g):
    return x * lax.rsqrt(jnp.mean(x * x, axis=-1, keepdims=True) + EPS) * g


def _dot(a, b):
    return jnp.dot(a, b, preferred_element_type=F32)


def _dot_nt(a, b):
    return lax.dot_general(a, b, (((1,), (1,)), ((), ())), preferred_element_type=F32)


def _split_bf16(x):
    hi = x.astype(BF16)
    lo = (x - hi.astype(F32)).astype(BF16)
    return hi, lo


def _row_copy(src, src_row, dst, dst_row, sem):
    return pltpu.make_async_copy(
        src.at[pl.ds(pl.multiple_of(src_row * ROW_TILES, ROW_TILES), ROW_TILES), :],
        dst.at[pl.ds(pl.multiple_of(dst_row * ROW_TILES, ROW_TILES), ROW_TILES), :],
        sem)


def _moe_output_chunks(dest_ref, x2_ref, info_ref, ybuf_ref, rows_ref, sems):
    tm = x2_ref.shape[0]
    i = pl.program_id(0)

    def start_gather(step, slot):
        base = step * (tm * TOP_K)

        def issue(r, c):
            for k in range(TOP_K):
                _row_copy(ybuf_ref, dest_ref[base + TOP_K * r + k], rows_ref.at[slot], k * tm + r,
                          sems.at[slot]).start(priority=1)
            return c

        lax.fori_loop(0, tm, issue, 0)

    @pl.when(i == 0)
    def _():
        start_gather(0, 0)

    @pl.when(i + 1 < pl.num_programs(0))
    def _():
        start_gather(i + 1, (i + 1) % 2)

    slot = i % 2
    rows = rows_ref.at[slot]
    pltpu.make_async_copy(ybuf_ref.at[pl.ds(0, TOP_K * tm * ROW_TILES), :], rows, sems.at[slot]).wait()

    info = info_ref[...]
    gate1 = info[:, 2:3]
    gate2 = info[:, 3:4]
    chunks = []
    for c in range(ROW_TILES):
        y1 = rows[pl.ds(c, tm, stride=ROW_TILES), :]
        y2 = rows[pl.ds(tm * ROW_TILES + c, tm, stride=ROW_TILES), :]
        chunks.append(x2_ref[:, c * LANES:(c + 1) * LANES] + (gate1 * y1 + gate2 * y2))
    return chunks


def _moe_gather_scratch(tm):
    return [pltpu.VMEM((2, TOP_K * tm * ROW_TILES, LANES), F32), pltpu.SemaphoreType.DMA((2,))]


def _mix_in_body(x, g_ref, w_ref, vg_ref, ws_ref, bs_ref, ga_ref, q_ref, k_ref, v_ref, ya_ref):
    tm = x.shape[0]
    h = _rms(x, g_ref[...]).astype(BF16)
    z = _dot(h, w_ref[...])

    t_idx = lax.broadcasted_iota(jnp.int32, (CHUNK, CHUNK), 0)
    s_idx = lax.broadcasted_iota(jnp.int32, (CHUNK, CHUNK), 1)
    causal = s_idx <= t_idx
    parts = []
    ssq = jnp.zeros((tm, 1), F32)
    for g in range(A_GROUPS):
        lanes = slice(g * A_CH, (g + 1) * A_CH)
        u = jax.nn.gelu(z[:, g * A_CH:(g + 1) * A_CH])
        vg = jax.nn.gelu(z[:, A_WIDTH + g * A_CH:A_WIDTH + (g + 1) * A_CH])
        vn = _rms(vg, vg_ref[:, lanes]).astype(BF16)
        ws = jnp.where(causal, ws_ref[g], 0.0).astype(BF16)
        bias = bs_ref[:, g:g + 1]
        mixed = jnp.concatenate(
            [_dot(ws, vn[c * CHUNK:(c + 1) * CHUNK, :]) + bias for c in range(tm // CHUNK)], axis=0)
        ya = u * mixed
        parts.append(ya)
        ssq = ssq + jnp.sum(ya * ya, axis=-1, keepdims=True)
    inv = lax.rsqrt(ssq * (1.0 / A_WIDTH) + EPS)
    for g in range(A_GROUPS):
        lanes = slice(g * A_CH, (g + 1) * A_CH)
        ya_ref[:, lanes] = (parts[g] * inv * ga_ref[:, lanes]).astype(BF16)

    lane = lax.broadcasted_iota(jnp.int32, (1, LANES), 1)
    q0 = 2 * A_WIDTH
    k0 = q0 + SB_WIDTH
    v0 = k0 + SB_WIDTH
    scale = 1.0 / (SB_HEAD_DIM ** 0.5)
    for hd in range(SB_HEADS):
        pair = hd // 2
        keep = (lane < SB_HEAD_DIM) if hd % 2 == 0 else (lane >= SB_HEAD_DIM)
        qs = z[:, q0 + pair * LANES:q0 + (pair + 1) * LANES] * scale
        vs = z[:, v0 + pair * LANES:v0 + (pair + 1) * LANES]
        q_ref[:, hd * LANES:(hd + 1) * LANES] = jnp.where(keep, qs, 0.0).astype(BF16)
        v_ref[:, hd * LANES:(hd + 1) * LANES] = jnp.where(keep, vs, 0.0).astype(BF16)
    k_ref[...] = z[:, k0:v0].astype(BF16)


def _mix_in_kernel(x_ref, *refs):
    _mix_in_body(x_ref[...], *refs)


def _moe_mix_in_kernel(dest_ref, x2_ref, info_ref, ybuf_ref, g_ref, w_ref, vg_ref, ws_ref, bs_ref, ga_ref,
                       x_ref, q_ref, k_ref, v_ref, ya_ref, rows_ref, sems):
    x = jnp.concatenate(_moe_output_chunks(dest_ref, x2_ref, info_ref, ybuf_ref, rows_ref, sems), axis=-1)
    x_ref[...] = x
    _mix_in_body(x, g_ref, w_ref, vg_ref, ws_ref, bs_ref, ga_ref, q_ref, k_ref, v_ref, ya_ref)


def _mix_in_specs(l, index):
    lay = lambda *a: (l, 0, 0)
    in_specs = [
        pl.BlockSpec((None, 1, D_MODEL), lay),
        pl.BlockSpec((None, D_MODEL, IN_WIDTH), lay),
        pl.BlockSpec((None, 1, A_WIDTH), lay),
        pl.BlockSpec((None, A_GROUPS, CHUNK, CHUNK), lambda *a: (l, 0, 0, 0)),
        pl.BlockSpec((None, CHUNK, A_GROUPS), lay),
        pl.BlockSpec((None, 1, A_WIDTH), lay),
    ]
    out_specs = [
        pl.BlockSpec((TM_IN, SB_HEADS * LANES), index),
        pl.BlockSpec((TM_IN, SB_WIDTH), index),
        pl.BlockSpec((TM_IN, SB_HEADS * LANES), index),
        pl.BlockSpec((TM_IN, A_WIDTH), index),
    ]
    return in_specs, out_specs


def _mix_in_out_shapes(T):
    return [
        jax.ShapeDtypeStruct((T, SB_HEADS * LANES), BF16),
        jax.ShapeDtypeStruct((T, SB_WIDTH), BF16),
        jax.ShapeDtypeStruct((T, SB_HEADS * LANES), BF16),
        jax.ShapeDtypeStruct((T, A_WIDTH), BF16),
    ]


def _mix_in(x2d, g, w_in, vg, ws, bs_t, ga, l):
    T = x2d.shape[0]
    row = lambda i: (i, 0)
    in_specs, out_specs = _mix_in_specs(l, row)
    return pl.pallas_call(
        _mix_in_kernel,
        grid=(T // TM_IN,),
        in_specs=[pl.BlockSpec((TM_IN, D_MODEL), row)] + in_specs,
        out_specs=out_specs,
        out_shape=_mix_in_out_shapes(T),
        compiler_params=pltpu.CompilerParams(dimension_semantics=("arbitrary",), vmem_limit_bytes=VMEM_LIMIT),
        name="mix_in",
    )(x2d, g, w_in, vg, ws, bs_t, ga)


def _moe_mix_in(dest, x2, info, ybuf, g, w_in, vg, ws, bs_t, ga, l):
    T = x2.shape[0]
    row = lambda i, d: (i, 0)
    in_specs, out_specs = _mix_in_specs(l, row)
    return pl.pallas_call(
        _moe_mix_in_kernel,
        grid_spec=pltpu.PrefetchScalarGridSpec(
            num_scalar_prefetch=1,
            grid=(T // TM_IN,),
            in_specs=[
                pl.BlockSpec((TM_IN, D_MODEL), row),
                pl.BlockSpec((TM_IN, LANES), row),
                pl.BlockSpec(memory_space=pl.ANY),
            ] + in_specs,
            out_specs=[pl.BlockSpec((TM_IN, D_MODEL), row)] + out_specs,
            scratch_shapes=_moe_gather_scratch(TM_IN),
        ),
        out_shape=[jax.ShapeDtypeStruct((T, D_MODEL), F32)] + _mix_in_out_shapes(T),
        compiler_params=pltpu.CompilerParams(dimension_semantics=("arbitrary",), vmem_limit_bytes=VMEM_LIMIT),
        name="moe_mix_in",
    )(dest, x2, info, ybuf, g, w_in, vg, ws, bs_t, ga)


def _stick_break_kernel(q_ref, k_ref, v_ref, o_ref, c_ref, live_ref):
    qi = pl.program_id(1)
    r_idx = lax.broadcasted_iota(jnp.int32, (TK, TK), 0)
    c_idx = lax.broadcasted_iota(jnp.int32, (TK, TK), 1)
    later = jnp.where(c_idx < r_idx, -1.0, 0.0).astype(BF16)

    def sweep(half, key0, n_keys, later_m, visible):
        rows = slice(half * HALF, (half + 1) * HALF)
        first = visible is not None
        sps, lszs = [], []
        for pr in range(SB_PAIRS):
            kb = k_ref[pl.ds(key0, n_keys), pr * LANES:(pr + 1) * LANES]
            qq = jnp.concatenate([q_ref[rows, hd * LANES:(hd + 1) * LANES] for hd in (2 * pr, 2 * pr + 1)], axis=0)
            zz = _dot_nt(qq, kb)
            for z in (zz[0:HALF, :], zz[HALF:2 * HALF, :]):
                sp = jnp.maximum(z, 0.0) + jnp.log(1.0 + jnp.exp2(jnp.abs(z) * (-LOG2E)))
                lszs.append(z - sp)
                sps.append(jnp.where(visible, sp, 0.0) if first else sp)
        after = _dot(jnp.concatenate([sp.astype(BF16) for sp in sps], axis=0), later_m)
        top = None
        for pr in range(SB_PAIRS):
            acc = None
            for hd in (2 * pr, 2 * pr + 1):
                p = jnp.exp(lszs[hd] + after[hd * HALF:(hd + 1) * HALF, :])
                if first:
                    p = jnp.where(visible, p, 0.0)
                term = _dot(p.astype(BF16), v_ref[pl.ds(key0, n_keys), hd * LANES:(hd + 1) * LANES])
                c = -jnp.sum(sps[hd], axis=-1, keepdims=True)
                if not first:
                    term = term * jnp.exp(c_ref[hd, rows, :])
                    c = c_ref[hd, rows, :] + c
                c_ref[hd, rows, :] = c
                acc = term if acc is None else acc + term
                top = jnp.exp(c) if top is None else jnp.maximum(top, jnp.exp(c))
            lanes = slice(pr * LANES, (pr + 1) * LANES)
            o_ref[rows, lanes] = acc if first else o_ref[rows, lanes] + acc
        live_ref[half] = (jnp.max(top) > 0.0).astype(jnp.int32)

    d0 = pl.multiple_of(qi * TK, TK)
    rows_a = lax.broadcasted_iota(jnp.int32, (HALF, HALF), 0)
    cols_a = lax.broadcasted_iota(jnp.int32, (HALF, HALF), 1)
    sweep(0, d0, HALF, later[0:HALF, 0:HALF], cols_a < rows_a)
    rows_b = lax.broadcasted_iota(jnp.int32, (HALF, TK), 0)
    cols_b = lax.broadcasted_iota(jnp.int32, (HALF, TK), 1)
    sweep(1, d0, TK, later, cols_b < rows_b + HALF)

    def cond(jj):
        return jnp.logical_and(jj < qi, live_ref[0] + live_ref[1] > 0)

    def body(jj):
        key0 = pl.multiple_of((qi - 1 - jj) * TK, TK)
        for half in range(2):
            @pl.when(live_ref[half] > 0)
            def _():
                sweep(half, key0, TK, later, None)
        return jj + 1

    lax.while_loop(cond, body, jnp.int32(0))


def _stick_break(q, k, v, B, S):
    return pl.pallas_call(
        _stick_break_kernel,
        grid=(B, S // TQ),
        in_specs=[
            pl.BlockSpec((None, TQ, SB_HEADS * LANES), lambda b, i: (b, i, 0)),
            pl.BlockSpec((None, S, SB_WIDTH), lambda b, i: (b, 0, 0)),
            pl.BlockSpec((None, S, SB_HEADS * LANES), lambda b, i: (b, 0, 0)),
        ],
        out_specs=pl.BlockSpec((None, TQ, SB_WIDTH), lambda b, i: (b, i, 0)),
        out_shape=jax.ShapeDtypeStruct((B, S, SB_WIDTH), F32),
        scratch_shapes=[pltpu.VMEM((SB_HEADS, TQ, 1), F32), pltpu.SMEM((2,), jnp.int32)],
        compiler_params=pltpu.CompilerParams(
            dimension_semantics=("arbitrary", "arbitrary"), vmem_limit_bytes=VMEM_LIMIT),
        name="stick_break",
    )(q.reshape(B, S, SB_HEADS * LANES), k.reshape(B, S, SB_WIDTH), v.reshape(B, S, SB_HEADS * LANES))


def _mem_kv_kernel(m_ref, g_ref, wk_ref, wv_ref, k_ref, v_ref):
    m = _rms(m_ref[...], g_ref[...]).astype(BF16)
    k_ref[...] = _dot(m, wk_ref[...]).astype(BF16)
    v_ref[...] = _dot(m, wv_ref[...]).astype(BF16)


def _mem_kv(mem, g, wk, wv):
    B, M, _ = mem.shape
    L = wk.shape[0]
    kv_spec = pl.BlockSpec((None, None, M, D_MODEL), lambda l, b: (l, b, 0, 0))
    w_spec = pl.BlockSpec((None, D_MODEL, D_MODEL), lambda l, b: (l, 0, 0))
    return pl.pallas_call(
        _mem_kv_kernel,
        grid=(L, B),
        in_specs=[
            pl.BlockSpec((None, M, D_MODEL), lambda l, b: (b, 0, 0)),
            pl.BlockSpec((None, 1, D_MODEL), lambda l, b: (l, 0, 0)),
            w_spec, w_spec,
        ],
        out_specs=[kv_spec, kv_spec],
        out_shape=[jax.ShapeDtypeStruct((L, B, M, D_MODEL), BF16)] * 2,
        compiler_params=pltpu.CompilerParams(
            dimension_semantics=("arbitrary", "arbitrary"), vmem_limit_bytes=VMEM_LIMIT),
        name="mem_kv",
    )(mem, g, wk, wv)


def _post_kernel(x_ref, ya_ref, yb_ref, gb_ref, wo_ref, gc_ref, wq_ref, kx_ref, vx_ref, wxo_ref, gm_ref,
                 wr_ref, br_ref, x2_ref, hm_ref, info_ref, cnt_ref, run_ref):
    tm = x_ref.shape[0]

    @pl.when(pl.program_id(0) == 0)
    def _():
        run_ref[...] = jnp.zeros_like(run_ref)

    ybn = _rms(yb_ref[...], gb_ref[...]).astype(BF16)
    x1 = x_ref[...] + _dot(jnp.concatenate([ya_ref[...], ybn], axis=-1), wo_ref[...])

    h = _rms(x1, gc_ref[...]).astype(BF16)
    q = (_dot(h, wq_ref[...]) * (1.0 / (X_HEAD_DIM ** 0.5))).astype(BF16)
    heads = []
    for hd in range(X_HEADS):
        cols = slice(hd * X_HEAD_DIM, (hd + 1) * X_HEAD_DIM)
        s = _dot_nt(q[:, cols], kx_ref[:, cols])
        p = jnp.exp(s - jnp.max(s, axis=-1, keepdims=True))
        p = p / jnp.sum(p, axis=-1, keepdims=True)
        heads.append(_dot(p.astype(BF16), vx_ref[:, cols]).astype(BF16))
    x2 = x1 + _dot(jnp.concatenate(heads, axis=-1), wxo_ref[...])
    x2_ref[...] = x2

    hm = _rms(x2, gm_ref[...])
    for c in range(ROW_TILES):
        hm_ref[pl.ds(c, tm, stride=ROW_TILES), :] = hm[:, c * LANES:(c + 1) * LANES]

    hm_hi, hm_lo = _split_bf16(hm)
    both = _dot(hm_hi, wr_ref[...])
    logits = both[:, 0:LANES] + both[:, LANES:2 * LANES] + _dot(hm_lo, wr_ref[:, 0:LANES]) + br_ref[...]

    lane = lax.broadcasted_iota(jnp.int32, (tm, LANES), 1)
    lanef = lane.astype(F32)
    big = float(LANES)
    is_group = lane < N_GROUPS
    gl = jnp.where(is_group, logits, NEG)
    gmax = jnp.max(gl, axis=-1, keepdims=True)
    gsel = jnp.min(jnp.where(gl == gmax, lanef, big), axis=-1, keepdims=True)
    gden = jnp.sum(jnp.where(is_group, jnp.exp(gl - gmax), 0.0), axis=-1, keepdims=True)
    g_gate = 1.0 / gden
    lo = ROUTE_BASE + EXPERTS_PER_GROUP * gsel
    in_group = (lanef >= lo) & (lanef < lo + EXPERTS_PER_GROUP)
    el = jnp.where(in_group, logits, NEG)
    v1 = jnp.max(el, axis=-1, keepdims=True)
    i1 = jnp.min(jnp.where(el == v1, lanef, big), axis=-1, keepdims=True)
    el2 = jnp.where(lanef == i1, NEG, el)
    v2 = jnp.max(el2, axis=-1, keepdims=True)
    i2 = jnp.min(jnp.where(el2 == v2, lanef, big), axis=-1, keepdims=True)
    t = jnp.exp(v2 - v1)
    den = 1.0 + t
    gate1 = g_gate * (1.0 / den)
    gate2 = g_gate * (t / den)

    hit1 = lanef == i1
    hit2 = lanef == i2
    multi = jnp.where(hit1 | hit2, 1.0, 0.0)
    r_idx = lax.broadcasted_iota(jnp.int32, (tm, tm), 0)
    c_idx = lax.broadcasted_iota(jnp.int32, (tm, tm), 1)
    earlier = jnp.where(c_idx < r_idx, 1.0, 0.0).astype(BF16)
    before = _dot(earlier, multi.astype(BF16)) + run_ref[0:1, :]
    rank1 = jnp.sum(jnp.where(hit1, before, 0.0), axis=-1, keepdims=True)
    rank2 = jnp.sum(jnp.where(hit2, before, 0.0), axis=-1, keepdims=True)
    run_ref[...] = run_ref[...] + jnp.sum(multi, axis=0, keepdims=True)
    cnt_ref[...] = run_ref[...]

    info = jnp.where(lane == 0, i1 - ROUTE_BASE, 0.0)
    info = jnp.where(lane == 1, i2 - ROUTE_BASE, info)
    info = jnp.where(lane == 2, gate1, info)
    info = jnp.where(lane == 3, gate2, info)
    info = jnp.where(lane == 4, rank1, info)
    info = jnp.where(lane == 5, rank2, info)
    info_ref[...] = info


def _post(x2d, ya, yb, gb, wo, gc, wq, kx, vx, wxo, gm, wr, br, l, S):
    T = x2d.shape[0]
    M = kx.shape[2]
    grid = (T // TM_POST,)
    row = lambda i: (i, 0)
    lay = lambda i: (l, 0, 0)
    batch = lambda i: (l, (i * TM_POST) // S, 0, 0)
    wspec = pl.BlockSpec((None, D_MODEL, D_MODEL), lay)
    gspec = pl.BlockSpec((None, 1, D_MODEL), lay)
    return pl.pallas_call(
        _post_kernel,
        grid=grid,
        in_specs=[
            pl.BlockSpec((TM_POST, D_MODEL), row),
            pl.BlockSpec((TM_POST, A_WIDTH), row),
            pl.BlockSpec((TM_POST, SB_WIDTH), row),
            pl.BlockSpec((None, 1, SB_WIDTH), lay),
            wspec, gspec, wspec,
            pl.BlockSpec((None, None, M, D_MODEL), batch),
            pl.BlockSpec((None, None, M, D_MODEL), batch),
            wspec, gspec,
            pl.BlockSpec((None, D_MODEL, 2 * LANES), lay),
            pl.BlockSpec((None, 1, LANES), lay),
        ],
        out_specs=[
            pl.BlockSpec((TM_POST, D_MODEL), row),
            pl.BlockSpec((TM_POST * ROW_TILES, LANES), row),
            pl.BlockSpec((TM_POST, LANES), row),
            pl.BlockSpec((SUBLANES, LANES), lambda i: (0, 0)),
        ],
        out_shape=[
            jax.ShapeDtypeStruct((T, D_MODEL), F32),
            jax.ShapeDtypeStruct((T * ROW_TILES, LANES), F32),
            jax.ShapeDtypeStruct((T, LANES), F32),
            jax.ShapeDtypeStruct((SUBLANES, LANES), F32),
        ],
        scratch_shapes=[pltpu.VMEM((SUBLANES, LANES), F32)],
        compiler_params=pltpu.CompilerParams(dimension_semantics=("arbitrary",), vmem_limit_bytes=VMEM_LIMIT),
        name="post",
    )(x2d, ya, yb, gb, wo, gc, wq, kx, vx, wxo, gm, wr, br)


def _dispatch_kernel(dest_ref, hm_ref, buf_in_ref, buf_ref, sem):
    del buf_in_ref
    base = pl.program_id(0) * (TD * TOP_K)

    def issue(r, c):
        for k in range(TOP_K):
            _row_copy(hm_ref, r, buf_ref, dest_ref[base + TOP_K * r + k], sem).start(priority=k)
        return c

    lax.fori_loop(0, TD, issue, 0)
    for k in range(TOP_K):
        pltpu.make_async_copy(hm_ref, buf_ref.at[pl.ds(0, TD * ROW_TILES), :], sem).wait()


def _dispatch(dest, hm, buf0):
    T = hm.shape[0] // ROW_TILES
    return pl.pallas_call(
        _dispatch_kernel,
        grid_spec=pltpu.PrefetchScalarGridSpec(
            num_scalar_prefetch=1,
            grid=(T // TD,),
            in_specs=[
                pl.BlockSpec((TD * ROW_TILES, LANES), lambda i, d: (i, 0)),
                pl.BlockSpec(memory_space=pl.ANY),
            ],
            out_specs=pl.BlockSpec(memory_space=pl.ANY),
            scratch_shapes=[pltpu.SemaphoreType.DMA(())],
        ),
        out_shape=jax.ShapeDtypeStruct(buf0.shape, F32),
        input_output_aliases={2: 0},
        compiler_params=pltpu.CompilerParams(dimension_semantics=("arbitrary",), vmem_limit_bytes=VMEM_LIMIT),
        name="dispatch",
    )(dest, hm, buf0)


def _experts_kernel(blk_e_ref, n_used_ref, buf_ref, w1_ref, w3_ref, w2_ref, y_ref, w1b_ref, w3b_ref, w2b_ref):
    i = pl.program_id(0)
    expert = blk_e_ref[i]
    previous = blk_e_ref[jnp.maximum(i - 1, 0)]

    @pl.when(jnp.logical_or(i == 0, expert != previous))
    def _():
        w1b_ref[...] = w1_ref[...].astype(BF16)
        w3b_ref[...] = w3_ref[...].astype(BF16)
        w2b_ref[...] = w2_ref[...].astype(BF16)

    @pl.when(i < n_used_ref[0])
    def _():
        xb = jnp.concatenate(
            [buf_ref[pl.ds(c, BM, stride=ROW_TILES), :] for c in range(ROW_TILES)], axis=-1).astype(BF16)
        h1 = _dot(xb, w1b_ref[...])
        h3 = _dot(xb, w3b_ref[...])
        a = (h1 * jax.nn.sigmoid(h1) * h3).astype(BF16)
        y = _dot(a, w2b_ref[...])
        for c in range(ROW_TILES):
            y_ref[pl.ds(c, BM, stride=ROW_TILES), :] = y[:, c * LANES:(c + 1) * LANES]

    @pl.when(i >= n_used_ref[0])
    def _():
        y_ref[...] = jnp.zeros_like(y_ref)


def _experts(blk_e, n_used, buf, w1, w3, w2, l):
    n_blocks = buf.shape[0] // (BM * ROW_TILES)
    rows = pl.BlockSpec((BM * ROW_TILES, LANES), lambda i, e, n: (i, 0))
    return pl.pallas_call(
        _experts_kernel,
        grid_spec=pltpu.PrefetchScalarGridSpec(
            num_scalar_prefetch=2,
            grid=(n_blocks,),
            in_specs=[
                rows,
                pl.BlockSpec((None, None, D_MODEL, D_EXPERT), lambda i, e, n: (l, e[i], 0, 0)),
                pl.BlockSpec((None, None, D_MODEL, D_EXPERT), lambda i, e, n: (l, e[i], 0, 0)),
                pl.BlockSpec((None, None, D_EXPERT, D_MODEL), lambda i, e, n: (l, e[i], 0, 0)),
            ],
            out_specs=rows,
            scratch_shapes=[
                pltpu.VMEM((D_MODEL, D_EXPERT), BF16),
                pltpu.VMEM((D_MODEL, D_EXPERT), BF16),
                pltpu.VMEM((D_EXPERT, D_MODEL), BF16),
            ],
        ),
        out_shape=jax.ShapeDtypeStruct(buf.shape, F32),
        compiler_params=pltpu.CompilerParams(dimension_semantics=("arbitrary",), vmem_limit_bytes=VMEM_LIMIT),
        name="experts",
    )(blk_e, n_used, buf, w1, w3, w2)


def _combine_final_kernel(dest_ref, x2_ref, info_ref, gf_ref, ybuf_ref, out_ref, rows_ref, sems):
    chunks = _moe_output_chunks(dest_ref, x2_ref, info_ref, ybuf_ref, rows_ref, sems)
    ssq = jnp.zeros((x2_ref.shape[0], 1), F32)
    for xc in chunks:
        ssq = ssq + jnp.sum(xc * xc, axis=-1, keepdims=True)
    inv = lax.rsqrt(ssq * (1.0 / D_MODEL) + EPS)
    for c, xc in enumerate(chunks):
        lanes = slice(c * LANES, (c + 1) * LANES)
        out_ref[:, lanes] = xc * inv * gf_ref[:, lanes]


def _combine_final(dest, x2, info, gf, ybuf):
    T = x2.shape[0]
    row = lambda i, d: (i, 0)
    return pl.pallas_call(
        _combine_final_kernel,
        grid_spec=pltpu.PrefetchScalarGridSpec(
            num_scalar_prefetch=1,
            grid=(T // TE,),
            in_specs=[
                pl.BlockSpec((TE, D_MODEL), row),
                pl.BlockSpec((TE, LANES), row),
                pl.BlockSpec((1, D_MODEL), lambda i, d: (0, 0)),
                pl.BlockSpec(memory_space=pl.ANY),
            ],
            out_specs=pl.BlockSpec((TE, D_MODEL), row),
            scratch_shapes=_moe_gather_scratch(TE),
        ),
        out_shape=jax.ShapeDtypeStruct((T, D_MODEL), F32),
        compiler_params=pltpu.CompilerParams(dimension_semantics=("arbitrary",), vmem_limit_bytes=VMEM_LIMIT),
        name="combine_final",
    )(dest, x2, info, gf, ybuf)


def kernel(x, mem, norm_mix, w_in, v_norm, w_spatial, b_spatial, out_norm_a, out_norm_b, w_out, norm_cross, norm_mem, w_xq, w_xk, w_xv, w_xo, norm_moe, w_group, b_group, w_router, b_router, w1, w3, w2, norm_final):
    B, S, D = x.shape
    L = w_in.shape[0]
    T = B * S
    assert D == D_MODEL and S % TQ == 0 and T % TM_IN == 0 and S % TM_POST == 0 and T % TD == 0 and T % TE == 0

    row3 = lambda a: a.reshape(L, 1, -1)
    w_in_b, w_out_b = w_in.astype(BF16), w_out.astype(BF16)
    w_xq_b, w_xk_b, w_xv_b, w_xo_b = (w.astype(BF16) for w in (w_xq, w_xk, w_xv, w_xo))
    pad = jnp.zeros((L, D, LANES - N_GROUPS - N_EXPERTS), F32)
    w_route = jnp.concatenate([w_group, w_router, pad], axis=-1)
    w_route_hi = w_route.astype(BF16)
    w_route_lo = (w_route - w_route_hi.astype(F32)).astype(BF16)
    w_route_b = jnp.concatenate([w_route_hi, w_route_lo], axis=-1)
    b_route = jnp.concatenate([b_group, b_router, pad[:, 0, :]], axis=-1).reshape(L, 1, LANES)
    bs_t = jnp.swapaxes(b_spatial, 1, 2)

    kx, vx = _mem_kv(mem, row3(norm_mem), w_xk_b, w_xv_b)

    n_slots = T * TOP_K + N_EXPERTS * BM
    n_blocks = n_slots // BM
    xs = x.reshape(T, D)
    moe = None
    for l in range(L):
        mix_args = (row3(norm_mix), w_in_b, row3(v_norm), w_spatial, bs_t, row3(out_norm_a), l)
        if moe is None:
            q, k, v, ya = _mix_in(xs, *mix_args)
        else:
            xs, q, k, v, ya = _moe_mix_in(*moe, *mix_args)
        yb = _stick_break(q, k, v, B, S).reshape(T, SB_WIDTH)
        x2, hm, info, cnt = _post(xs, ya, yb, row3(out_norm_b), w_out_b, row3(norm_cross), w_xq_b, kx, vx, w_xo_b,
                                  row3(norm_moe), w_route_b, b_route, l, S)
        eid = info[:, 0:TOP_K].astype(jnp.int32)
        rank = info[:, 4:4 + TOP_K].astype(jnp.int32)
        counts = cnt[0, ROUTE_BASE:ROUTE_BASE + N_EXPERTS].astype(jnp.int32)
        padded = ((counts + BM - 1) // BM) * BM
        seg_end = jnp.cumsum(padded)
        seg_start = seg_end - padded
        expert_ids = jnp.arange(N_EXPERTS, dtype=jnp.int32)
        dest = (rank + jnp.sum(jnp.where(eid[..., None] == expert_ids, seg_start, 0), axis=-1)).reshape(-1)
        blk_row = jnp.arange(n_blocks, dtype=jnp.int32) * BM
        blk_e = jnp.minimum(
            jnp.sum((seg_end[None, :] <= blk_row[:, None]).astype(jnp.int32), axis=1), N_EXPERTS - 1)
        n_used = (seg_end[-1:] // BM).astype(jnp.int32)
        buf = _dispatch(dest, hm, jnp.zeros((n_slots * ROW_TILES, LANES), F32))
        ybuf = _experts(blk_e, n_used, buf, w1, w3, w2, l)
        moe = (dest, x2, info, ybuf)
    dest, x2, info, ybuf = moe
    return _combine_final(dest, x2, info, norm_final.reshape(1, D), ybuf).reshape(B, S, D)
```

```python
import jax
import jax.numpy as jnp
from jax import lax
from jax.experimental import pallas as pl
from jax.experimental.pallas import tpu as pltpu

F32 = jnp.float32
BF16 = jnp.bfloat16
U32 = jnp.uint32

D_MODEL = 1024
A_WIDTH = 512
A_GROUPS = 4
A_CH = 128
CHUNK = 128
SB_WIDTH = 512
SB_HEADS = 8
SB_HEAD_DIM = 64
SB_PAIRS = SB_HEADS // 2
IN_WIDTH = 2 * A_WIDTH + 3 * SB_WIDTH
X_HEADS = 4
X_HEAD_DIM = 256
N_GROUPS = 4
EXPERTS_PER_GROUP = 8
N_EXPERTS = 32
TOP_K = 2
D_EXPERT = 512
EPS = 1e-6

LANES = 128
SUBLANES = 8
ROW_TILES = D_MODEL // LANES
PACK_TILES = ROW_TILES // 2

TM_IN = 512
TQ = 256
TK = 256
HALF = TQ // 2
TM_POST = 512
TD = 512
BM = 512
TE = 256
ROUTE_BASE = N_GROUPS
NEG = -1e30
LOG2E = 1.4426950408889634

VMEM_LIMIT = 52 * 1024 * 1024


def _rms(x, g):
    return x * lax.rsqrt(jnp.mean(x * x, axis=-1, keepdims=True) + EPS) * g


def _dot(a, b):
    return jnp.dot(a, b, preferred_element_type=F32)


def _dot_nt(a, b):
    return lax.dot_general(a, b, (((1,), (1,)), ((), ())), preferred_element_type=F32)


def _split_bf16(x):
    hi = x.astype(BF16)
    lo = (x - hi.astype(F32)).astype(BF16)
    return hi, lo


def _row_copy(src, src_row, dst, dst_row, sem):
    return pltpu.make_async_copy(
        src.at[pl.ds(pl.multiple_of(src_row * PACK_TILES, PACK_TILES), PACK_TILES), :],
        dst.at[pl.ds(pl.multiple_of(dst_row * PACK_TILES, PACK_TILES), PACK_TILES), :],
        sem)


def _store_packed(x, ref):
    half = D_MODEL // 2
    hi = pltpu.bitcast(x[:, 0:half].astype(BF16).astype(F32), U32)
    lo = pltpu.bitcast(x[:, half:D_MODEL].astype(BF16).astype(F32), U32)
    words = hi | (lo >> 16)
    for c in range(PACK_TILES):
        ref[pl.ds(c, x.shape[0], stride=PACK_TILES), :] = words[:, c * LANES:(c + 1) * LANES]


def _load_packed_chunks(ref, row0, rows):
    words = [ref[pl.ds(row0 * PACK_TILES + c, rows, stride=PACK_TILES), :] for c in range(PACK_TILES)]
    return ([pltpu.bitcast(w & jnp.uint32(0xFFFF0000), F32) for w in words]
            + [pltpu.bitcast(w << 16, F32) for w in words])


def _moe_output_chunks(dest_ref, x2_ref, info_ref, ybuf_ref, rows_ref, sems):
    tm = x2_ref.shape[0]
    i = pl.program_id(0)

    def start_gather(step, slot):
        base = step * (tm * TOP_K)

        def issue(r, c):
            for k in range(TOP_K):
                _row_copy(ybuf_ref, dest_ref[base + TOP_K * r + k], rows_ref.at[slot], k * tm + r,
                          sems.at[slot]).start(priority=1)
            return c

        lax.fori_loop(0, tm, issue, 0)

    @pl.when(i == 0)
    def _():
        start_gather(0, 0)

    @pl.when(i + 1 < pl.num_programs(0))
    def _():
        start_gather(i + 1, (i + 1) % 2)

    slot = i % 2
    rows = rows_ref.at[slot]
    pltpu.make_async_copy(ybuf_ref.at[pl.ds(0, TOP_K * tm * PACK_TILES), :], rows, sems.at[slot]).wait()

    info = info_ref[...]
    gate1 = info[:, 2:3]
    gate2 = info[:, 3:4]
    y1 = _load_packed_chunks(rows, 0, tm)
    y2 = _load_packed_chunks(rows, tm, tm)
    return [x2_ref[:, c * LANES:(c + 1) * LANES] + (gate1 * y1[c] + gate2 * y2[c]) for c in range(ROW_TILES)]


def _moe_gather_scratch(tm):
    return [pltpu.VMEM((2, TOP_K * tm * PACK_TILES, LANES), U32), pltpu.SemaphoreType.DMA((2,))]


def _mix_in_body(x, g_ref, w_ref, vg_ref, ws_ref, bs_ref, ga_ref, q_ref, k_ref, v_ref, ya_ref):
    tm = x.shape[0]
    h = _rms(x, g_ref[...]).astype(BF16)
    z = _dot(h, w_ref[...])

    t_idx = lax.broadcasted_iota(jnp.int32, (CHUNK, CHUNK), 0)
    s_idx = lax.broadcasted_iota(jnp.int32, (CHUNK, CHUNK), 1)
    causal = s_idx <= t_idx
    parts = []
    ssq = jnp.zeros((tm, 1), F32)
    for g in range(A_GROUPS):
        lanes = slice(g * A_CH, (g + 1) * A_CH)
        u = jax.nn.gelu(z[:, g * A_CH:(g + 1) * A_CH])
        vg = jax.nn.gelu(z[:, A_WIDTH + g * A_CH:A_WIDTH + (g + 1) * A_CH])
        vn = _rms(vg, vg_ref[:, lanes]).astype(BF16)
        ws = jnp.where(causal, ws_ref[g], 0.0).astype(BF16)
        bias = bs_ref[:, g:g + 1]
        mixed = jnp.concatenate(
            [_dot(ws, vn[c * CHUNK:(c + 1) * CHUNK, :]) + bias for c in range(tm // CHUNK)], axis=0)
        ya = u * mixed
        parts.append(ya)
        ssq = ssq + jnp.sum(ya * ya, axis=-1, keepdims=True)
    inv = lax.rsqrt(ssq * (1.0 / A_WIDTH) + EPS)
    for g in range(A_GROUPS):
        lanes = slice(g * A_CH, (g + 1) * A_CH)
        ya_ref[:, lanes] = (parts[g] * inv * ga_ref[:, lanes]).astype(BF16)

    lane = lax.broadcasted_iota(jnp.int32, (1, LANES), 1)
    q0 = 2 * A_WIDTH
    k0 = q0 + SB_WIDTH
    v0 = k0 + SB_WIDTH
    scale = 1.0 / (SB_HEAD_DIM ** 0.5)
    for hd in range(SB_HEADS):
        pair = hd // 2
        keep = (lane < SB_HEAD_DIM) if hd % 2 == 0 else (lane >= SB_HEAD_DIM)
        qs = z[:, q0 + pair * LANES:q0 + (pair + 1) * LANES] * scale
        vs = z[:, v0 + pair * LANES:v0 + (pair + 1) * LANES]
        q_ref[:, hd * LANES:(hd + 1) * LANES] = jnp.where(keep, qs, 0.0).astype(BF16)
        v_ref[:, hd * LANES:(hd + 1) * LANES] = jnp.where(keep, vs, 0.0).astype(BF16)
    k_ref[...] = z[:, k0:v0].astype(BF16)


def _mix_in_kernel(x_ref, *refs):
    _mix_in_body(x_ref[...], *refs)


def _moe_mix_in_kernel(dest_ref, x2_ref, info_ref, ybuf_ref, g_ref, w_ref, vg_ref, ws_ref, bs_ref, ga_ref,
                       x_ref, q_ref, k_ref, v_ref, ya_ref, rows_ref, sems):
    x = jnp.concatenate(_moe_output_chunks(dest_ref, x2_ref, info_ref, ybuf_ref, rows_ref, sems), axis=-1)
    x_ref[...] = x
    _mix_in_body(x, g_ref, w_ref, vg_ref, ws_ref, bs_ref, ga_ref, q_ref, k_ref, v_ref, ya_ref)


def _mix_in_specs(l, index):
    lay = lambda *a: (l, 0, 0)
    in_specs = [
        pl.BlockSpec((None, 1, D_MODEL), lay),
        pl.BlockSpec((None, D_MODEL, IN_WIDTH), lay),
        pl.BlockSpec((None, 1, A_WIDTH), lay),
        pl.BlockSpec((None, A_GROUPS, CHUNK, CHUNK), lambda *a: (l, 0, 0, 0)),
        pl.BlockSpec((None, CHUNK, A_GROUPS), lay),
        pl.BlockSpec((None, 1, A_WIDTH), lay),
    ]
    out_specs = [
        pl.BlockSpec((TM_IN, SB_HEADS * LANES), index),
        pl.BlockSpec((TM_IN, SB_WIDTH), index),
        pl.BlockSpec((TM_IN, SB_HEADS * LANES), index),
        pl.BlockSpec((TM_IN, A_WIDTH), index),
    ]
    return in_specs, out_specs


def _mix_in_out_shapes(T):
    return [
        jax.ShapeDtypeStruct((T, SB_HEADS * LANES), BF16),
        jax.ShapeDtypeStruct((T, SB_WIDTH), BF16),
        jax.ShapeDtypeStruct((T, SB_HEADS * LANES), BF16),
        jax.ShapeDtypeStruct((T, A_WIDTH), BF16),
    ]


def _mix_in(x2d, g, w_in, vg, ws, bs_t, ga, l):
    T = x2d.shape[0]
    row = lambda i: (i, 0)
    in_specs, out_specs = _mix_in_specs(l, row)
    return pl.pallas_call(
        _mix_in_kernel,
        grid=(T // TM_IN,),
        in_specs=[pl.BlockSpec((TM_IN, D_MODEL), row)] + in_specs,
        out_specs=out_specs,
        out_shape=_mix_in_out_shapes(T),
        compiler_params=pltpu.CompilerParams(dimension_semantics=("arbitrary",), vmem_limit_bytes=VMEM_LIMIT),
        name="mix_in",
    )(x2d, g, w_in, vg, ws, bs_t, ga)


def _moe_mix_in(dest, x2, info, ybuf, g, w_in, vg, ws, bs_t, ga, l):
    T = x2.shape[0]
    row = lambda i, d: (i, 0)
    in_specs, out_specs = _mix_in_specs(l, row)
    return pl.pallas_call(
        _moe_mix_in_kernel,
        grid_spec=pltpu.PrefetchScalarGridSpec(
            num_scalar_prefetch=1,
            grid=(T // TM_IN,),
            in_specs=[
                pl.BlockSpec((TM_IN, D_MODEL), row),
                pl.BlockSpec((TM_IN, LANES), row),
                pl.BlockSpec(memory_space=pl.ANY),
            ] + in_specs,
            out_specs=[pl.BlockSpec((TM_IN, D_MODEL), row)] + out_specs,
            scratch_shapes=_moe_gather_scratch(TM_IN),
        ),
        out_shape=[jax.ShapeDtypeStruct((T, D_MODEL), F32)] + _mix_in_out_shapes(T),
        compiler_params=pltpu.CompilerParams(dimension_semantics=("arbitrary",), vmem_limit_bytes=VMEM_LIMIT),
        name="moe_mix_in",
    )(dest, x2, info, ybuf, g, w_in, vg, ws, bs_t, ga)


def _stick_break_kernel(q_ref, k_ref, v_ref, o_ref, c_ref, live_ref):
    qi = pl.program_id(1)
    r_idx = lax.broadcasted_iota(jnp.int32, (TK, TK), 0)
    c_idx = lax.broadcasted_iota(jnp.int32, (TK, TK), 1)
    later = jnp.where(c_idx < r_idx, -1.0, 0.0).astype(BF16)

    def sweep(half, key0, n_keys, later_m, visible):
        rows = slice(half * HALF, (half + 1) * HALF)
        first = visible is not None
        sps, lszs = [], []
        for pr in range(SB_PAIRS):
            kb = k_ref[pl.ds(key0, n_keys), pr * LANES:(pr + 1) * LANES]
            qq = jnp.concatenate([q_ref[rows, hd * LANES:(hd + 1) * LANES] for hd in (2 * pr, 2 * pr + 1)], axis=0)
            zz = _dot_nt(qq, kb)
            for z in (zz[0:HALF, :], zz[HALF:2 * HALF, :]):
                sp = jnp.maximum(z, 0.0) + jnp.log(1.0 + jnp.exp2(jnp.abs(z) * (-LOG2E)))
                lszs.append(z - sp)
                sps.append(jnp.where(visible, sp, 0.0) if first else sp)
        after = _dot(jnp.concatenate([sp.astype(BF16) for sp in sps], axis=0), later_m)
        top = None
        for pr in range(SB_PAIRS):
            acc = None
            for hd in (2 * pr, 2 * pr + 1):
                p = jnp.exp(lszs[hd] + after[hd * HALF:(hd + 1) * HALF, :])
                if first:
                    p = jnp.where(visible, p, 0.0)
                term = _dot(p.astype(BF16), v_ref[pl.ds(key0, n_keys), hd * LANES:(hd + 1) * LANES])
                c = -jnp.sum(sps[hd], axis=-1, keepdims=True)
                if not first:
                    term = term * jnp.exp(c_ref[hd, rows, :])
                    c = c_ref[hd, rows, :] + c
                c_ref[hd, rows, :] = c
                acc = term if acc is None else acc + term
                top = jnp.exp(c) if top is None else jnp.maximum(top, jnp.exp(c))
            lanes = slice(pr * LANES, (pr + 1) * LANES)
            o_ref[rows, lanes] = acc if first else o_ref[rows, lanes] + acc
        live_ref[half] = (jnp.max(top) > 0.0).astype(jnp.int32)

    d0 = pl.multiple_of(qi * TK, TK)
    rows_a = lax.broadcasted_iota(jnp.int32, (HALF, HALF), 0)
    cols_a = lax.broadcasted_iota(jnp.int32, (HALF, HALF), 1)
    sweep(0, d0, HALF, later[0:HALF, 0:HALF], cols_a < rows_a)
    rows_b = lax.broadcasted_iota(jnp.int32, (HALF, TK), 0)
    cols_b = lax.broadcasted_iota(jnp.int32, (HALF, TK), 1)
    sweep(1, d0, TK, later, cols_b < rows_b + HALF)

    def cond(jj):
        return jnp.logical_and(jj < qi, live_ref[0] + live_ref[1] > 0)

    def body(jj):
        key0 = pl.multiple_of((qi - 1 - jj) * TK, TK)
        for half in range(2):
            @pl.when(live_ref[half] > 0)
            def _():
                sweep(half, key0, TK, later, None)
        return jj + 1

    lax.while_loop(cond, body, jnp.int32(0))


def _stick_break(q, k, v, B, S):
    return pl.pallas_call(
        _stick_break_kernel,
        grid=(B, S // TQ),
        in_specs=[
            pl.BlockSpec((None, TQ, SB_HEADS * LANES), lambda b, i: (b, i, 0)),
            pl.BlockSpec((None, S, SB_WIDTH), lambda b, i: (b, 0, 0)),
            pl.BlockSpec((None, S, SB_HEADS * LANES), lambda b, i: (b, 0, 0)),
        ],
        out_specs=pl.BlockSpec((None, TQ, SB_WIDTH), lambda b, i: (b, i, 0)),
        out_shape=jax.ShapeDtypeStruct((B, S, SB_WIDTH), F32),
        scratch_shapes=[pltpu.VMEM((SB_HEADS, TQ, 1), F32), pltpu.SMEM((2,), jnp.int32)],
        compiler_params=pltpu.CompilerParams(
            dimension_semantics=("arbitrary", "arbitrary"), vmem_limit_bytes=VMEM_LIMIT),
        name="stick_break",
    )(q.reshape(B, S, SB_HEADS * LANES), k.reshape(B, S, SB_WIDTH), v.reshape(B, S, SB_HEADS * LANES))


def _mem_kv_kernel(m_ref, g_ref, wk_ref, wv_ref, k_ref, v_ref):
    m = _rms(m_ref[...], g_ref[...]).astype(BF16)
    k_ref[...] = _dot(m, wk_ref[...]).astype(BF16)
    v_ref[...] = _dot(m, wv_ref[...]).astype(BF16)


def _mem_kv(mem, g, wk, wv):
    B, M, _ = mem.shape
    L = wk.shape[0]
    kv_spec = pl.BlockSpec((None, None, M, D_MODEL), lambda l, b: (l, b, 0, 0))
    w_spec = pl.BlockSpec((None, D_MODEL, D_MODEL), lambda l, b: (l, 0, 0))
    return pl.pallas_call(
        _mem_kv_kernel,
        grid=(L, B),
        in_specs=[
            pl.BlockSpec((None, M, D_MODEL), lambda l, b: (b, 0, 0)),
            pl.BlockSpec((None, 1, D_MODEL), lambda l, b: (l, 0, 0)),
            w_spec, w_spec,
        ],
        out_specs=[kv_spec, kv_spec],
        out_shape=[jax.ShapeDtypeStruct((L, B, M, D_MODEL), BF16)] * 2,
        compiler_params=pltpu.CompilerParams(
            dimension_semantics=("arbitrary", "arbitrary"), vmem_limit_bytes=VMEM_LIMIT),
        name="mem_kv",
    )(mem, g, wk, wv)


def _post_kernel(x_ref, ya_ref, yb_ref, gb_ref, wo_ref, gc_ref, wq_ref, kx_ref, vx_ref, wxo_ref, gm_ref,
                 wr_ref, br_ref, x2_ref, hm_ref, info_ref, cnt_ref, run_ref):
    tm = x_ref.shape[0]

    @pl.when(pl.program_id(0) == 0)
    def _():
        run_ref[...] = jnp.zeros_like(run_ref)

    ybn = _rms(yb_ref[...], gb_ref[...]).astype(BF16)
    x1 = x_ref[...] + _dot(jnp.concatenate([ya_ref[...], ybn], axis=-1), wo_ref[...])

    h = _rms(x1, gc_ref[...]).astype(BF16)
    q = (_dot(h, wq_ref[...]) * (1.0 / (X_HEAD_DIM ** 0.5))).astype(BF16)
    heads = []
    for hd in range(X_HEADS):
        cols = slice(hd * X_HEAD_DIM, (hd + 1) * X_HEAD_DIM)
        s = _dot_nt(q[:, cols], kx_ref[:, cols])
        p = jnp.exp(s - jnp.max(s, axis=-1, keepdims=True))
        p = p / jnp.sum(p, axis=-1, keepdims=True)
        heads.append(_dot(p.astype(BF16), vx_ref[:, cols]).astype(BF16))
    x2 = x1 + _dot(jnp.concatenate(heads, axis=-1), wxo_ref[...])
    x2_ref[...] = x2

    hm = _rms(x2, gm_ref[...])
    _store_packed(hm, hm_ref)

    hm_hi, hm_lo = _split_bf16(hm)
    both = _dot(hm_hi, wr_ref[...])
    logits = both[:, 0:LANES] + both[:, LANES:2 * LANES] + _dot(hm_lo, wr_ref[:, 0:LANES]) + br_ref[...]

    lane = lax.broadcasted_iota(jnp.int32, (tm, LANES), 1)
    lanef = lane.astype(F32)
    big = float(LANES)
    is_group = lane < N_GROUPS
    gl = jnp.where(is_group, logits, NEG)
    gmax = jnp.max(gl, axis=-1, keepdims=True)
    gsel = jnp.min(jnp.where(gl == gmax, lanef, big), axis=-1, keepdims=True)
    gden = jnp.sum(jnp.where(is_group, jnp.exp(gl - gmax), 0.0), axis=-1, keepdims=True)
    g_gate = 1.0 / gden
    lo = ROUTE_BASE + EXPERTS_PER_GROUP * gsel
    in_group = (lanef >= lo) & (lanef < lo + EXPERTS_PER_GROUP)
    el = jnp.where(in_group, logits, NEG)
    v1 = jnp.max(el, axis=-1, keepdims=True)
    i1 = jnp.min(jnp.where(el == v1, lanef, big), axis=-1, keepdims=True)
    el2 = jnp.where(lanef == i1, NEG, el)
    v2 = jnp.max(el2, axis=-1, keepdims=True)
    i2 = jnp.min(jnp.where(el2 == v2, lanef, big), axis=-1, keepdims=True)
    t = jnp.exp(v2 - v1)
    den = 1.0 + t
    gate1 = g_gate * (1.0 / den)
    gate2 = g_gate * (t / den)

    hit1 = lanef == i1
    hit2 = lanef == i2
    multi = jnp.where(hit1 | hit2, 1.0, 0.0)
    r_idx = lax.broadcasted_iota(jnp.int32, (tm, tm), 0)
    c_idx = lax.broadcasted_iota(jnp.int32, (tm, tm), 1)
    earlier = jnp.where(c_idx < r_idx, 1.0, 0.0).astype(BF16)
    before = _dot(earlier, multi.astype(BF16)) + run_ref[0:1, :]
    rank1 = jnp.sum(jnp.where(hit1, before, 0.0), axis=-1, keepdims=True)
    rank2 = jnp.sum(jnp.where(hit2, before, 0.0), axis=-1, keepdims=True)
    run_ref[...] = run_ref[...] + jnp.sum(multi, axis=0, keepdims=True)
    cnt_ref[...] = run_ref[...]

    info = jnp.where(lane == 0, i1 - ROUTE_BASE, 0.0)
    info = jnp.where(lane == 1, i2 - ROUTE_BASE, info)
    info = jnp.where(lane == 2, gate1, info)
    info = jnp.where(lane == 3, gate2, info)
    info = jnp.where(lane == 4, rank1, info)
    info = jnp.where(lane == 5, rank2, info)
    info_ref[...] = info


def _post(x2d, ya, yb, gb, wo, gc, wq, kx, vx, wxo, gm, wr, br, l, S):
    T = x2d.shape[0]
    M = kx.shape[2]
    grid = (T // TM_POST,)
    row = lambda i: (i, 0)
    lay = lambda i: (l, 0, 0)
    batch = lambda i: (l, (i * TM_POST) // S, 0, 0)
    wspec = pl.BlockSpec((None, D_MODEL, D_MODEL), lay)
    gspec = pl.BlockSpec((None, 1, D_MODEL), lay)
    return pl.pallas_call(
        _post_kernel,
        grid=grid,
        in_specs=[
            pl.BlockSpec((TM_POST, D_MODEL), row),
            pl.BlockSpec((TM_POST, A_WIDTH), row),
            pl.BlockSpec((TM_POST, SB_WIDTH), row),
            pl.BlockSpec((None, 1, SB_WIDTH), lay),
            wspec, gspec, wspec,
            pl.BlockSpec((None, None, M, D_MODEL), batch),
            pl.BlockSpec((None, None, M, D_MODEL), batch),
            wspec, gspec,
            pl.BlockSpec((None, D_MODEL, 2 * LANES), lay),
            pl.BlockSpec((None, 1, LANES), lay),
        ],
        out_specs=[
            pl.BlockSpec((TM_POST, D_MODEL), row),
            pl.BlockSpec((TM_POST * PACK_TILES, LANES), row),
            pl.BlockSpec((TM_POST, LANES), row),
            pl.BlockSpec((SUBLANES, LANES), lambda i: (0, 0)),
        ],
        out_shape=[
            jax.ShapeDtypeStruct((T, D_MODEL), F32),
            jax.ShapeDtypeStruct((T * PACK_TILES, LANES), U32),
            jax.ShapeDtypeStruct((T, LANES), F32),
            jax.ShapeDtypeStruct((SUBLANES, LANES), F32),
        ],
        scratch_shapes=[pltpu.VMEM((SUBLANES, LANES), F32)],
        compiler_params=pltpu.CompilerParams(dimension_semantics=("arbitrary",), vmem_limit_bytes=VMEM_LIMIT),
        name="post",
    )(x2d, ya, yb, gb, wo, gc, wq, kx, vx, wxo, gm, wr, br)


def _dispatch_kernel(dest_ref, hm_ref, buf_in_ref, buf_ref, sem):
    del buf_in_ref
    base = pl.program_id(0) * (TD * TOP_K)

    def issue(r, c):
        for k in range(TOP_K):
            _row_copy(hm_ref, r, buf_ref, dest_ref[base + TOP_K * r + k], sem).start(priority=k)
        return c

    lax.fori_loop(0, TD, issue, 0)
    for k in range(TOP_K):
        pltpu.make_async_copy(hm_ref, buf_ref.at[pl.ds(0, TD * PACK_TILES), :], sem).wait()


def _dispatch(dest, hm, buf0):
    T = hm.shape[0] // PACK_TILES
    return pl.pallas_call(
        _dispatch_kernel,
        grid_spec=pltpu.PrefetchScalarGridSpec(
            num_scalar_prefetch=1,
            grid=(T // TD,),
            in_specs=[
                pl.BlockSpec((TD * PACK_TILES, LANES), lambda i, d: (i, 0)),
                pl.BlockSpec(memory_space=pl.ANY),
            ],
            out_specs=pl.BlockSpec(memory_space=pl.ANY),
            scratch_shapes=[pltpu.SemaphoreType.DMA(())],
        ),
        out_shape=jax.ShapeDtypeStruct(buf0.shape, U32),
        input_output_aliases={2: 0},
        compiler_params=pltpu.CompilerParams(dimension_semantics=("arbitrary",), vmem_limit_bytes=VMEM_LIMIT),
        name="dispatch",
    )(dest, hm, buf0)


def _experts_kernel(blk_e_ref, n_used_ref, buf_ref, w1_ref, w3_ref, w2_ref, y_ref, w1b_ref, w3b_ref, w2b_ref):
    i = pl.program_id(0)
    expert = blk_e_ref[i]
    previous = blk_e_ref[jnp.maximum(i - 1, 0)]

    @pl.when(jnp.logical_or(i == 0, expert != previous))
    def _():
        w1b_ref[...] = w1_ref[...].astype(BF16)
        w3b_ref[...] = w3_ref[...].astype(BF16)
        w2b_ref[...] = w2_ref[...].astype(BF16)

    @pl.when(i < n_used_ref[0])
    def _():
        xb = jnp.concatenate(_load_packed_chunks(buf_ref, 0, BM), axis=-1).astype(BF16)
        h1 = _dot(xb, w1b_ref[...])
        h3 = _dot(xb, w3b_ref[...])
        a = (h1 * jax.nn.sigmoid(h1) * h3).astype(BF16)
        _store_packed(_dot(a, w2b_ref[...]), y_ref)

    @pl.when(i >= n_used_ref[0])
    def _():
        y_ref[...] = jnp.zeros_like(y_ref)


def _experts(blk_e, n_used, buf, w1, w3, w2, l):
    n_blocks = buf.shape[0] // (BM * PACK_TILES)
    rows = pl.BlockSpec((BM * PACK_TILES, LANES), lambda i, e, n: (i, 0))
    return pl.pallas_call(
        _experts_kernel,
        grid_spec=pltpu.PrefetchScalarGridSpec(
            num_scalar_prefetch=2,
            grid=(n_blocks,),
            in_specs=[
                rows,
                pl.BlockSpec((None, None, D_MODEL, D_EXPERT), lambda i, e, n: (l, e[i], 0, 0)),
                pl.BlockSpec((None, None, D_MODEL, D_EXPERT), lambda i, e, n: (l, e[i], 0, 0)),
                pl.BlockSpec((None, None, D_EXPERT, D_MODEL), lambda i, e, n: (l, e[i], 0, 0)),
            ],
            out_specs=rows,
            scratch_shapes=[
                pltpu.VMEM((D_MODEL, D_EXPERT), BF16),
                pltpu.VMEM((D_MODEL, D_EXPERT), BF16),
                pltpu.VMEM((D_EXPERT, D_MODEL), BF16),
            ],
        ),
        out_shape=jax.ShapeDtypeStruct(buf.shape, U32),
        compiler_params=pltpu.CompilerParams(dimension_semantics=("arbitrary",), vmem_limit_bytes=VMEM_LIMIT),
        name="experts",
    )(blk_e, n_used, buf, w1, w3, w2)


def _combine_final_kernel(dest_ref, x2_ref, info_ref, gf_ref, ybuf_ref, out_ref, rows_ref, sems):
    chunks = _moe_output_chunks(dest_ref, x2_ref, info_ref, ybuf_ref, rows_ref, sems)
    ssq = jnp.zeros((x2_ref.shape[0], 1), F32)
    for xc in chunks:
        ssq = ssq + jnp.sum(xc * xc, axis=-1, keepdims=True)
    inv = lax.rsqrt(ssq * (1.0 / D_MODEL) + EPS)
    for c, xc in enumerate(chunks):
        lanes = slice(c * LANES, (c + 1) * LANES)
        out_ref[:, lanes] = xc * inv * gf_ref[:, lanes]


def _combine_final(dest, x2, info, gf, ybuf):
    T = x2.shape[0]
    row = lambda i, d: (i, 0)
    return pl.pallas_call(
        _combine_final_kernel,
        grid_spec=pltpu.PrefetchScalarGridSpec(
            num_scalar_prefetch=1,
            grid=(T // TE,),
            in_specs=[
                pl.BlockSpec((TE, D_MODEL), row),
                pl.BlockSpec((TE, LANES), row),
                pl.BlockSpec((1, D_MODEL), lambda i, d: (0, 0)),
                pl.BlockSpec(memory_space=pl.ANY),
            ],
            out_specs=pl.BlockSpec((TE, D_MODEL), row),
            scratch_shapes=_moe_gather_scratch(TE),
        ),
        out_shape=jax.ShapeDtypeStruct((T, D_MODEL), F32),
        compiler_params=pltpu.CompilerParams(dimension_semantics=("arbitrary",), vmem_limit_bytes=VMEM_LIMIT),
        name="combine_final",
    )(dest, x2, info, gf, ybuf)


def kernel(x, mem, norm_mix, w_in, v_norm, w_spatial, b_spatial, out_norm_a, out_norm_b, w_out, norm_cross, norm_mem, w_xq, w_xk, w_xv, w_xo, norm_moe, w_group, b_group, w_router, b_router, w1, w3, w2, norm_final):
    B, S, D = x.shape
    L = w_in.shape[0]
    T = B * S
    assert D == D_MODEL and S % TQ == 0 and T % TM_IN == 0 and S % TM_POST == 0 and T % TD == 0 and T % TE == 0

    row3 = lambda a: a.reshape(L, 1, -1)
    w_in_b, w_out_b = w_in.astype(BF16), w_out.astype(BF16)
    w_xq_b, w_xk_b, w_xv_b, w_xo_b = (w.astype(BF16) for w in (w_xq, w_xk, w_xv, w_xo))
    pad = jnp.zeros((L, D, LANES - N_GROUPS - N_EXPERTS), F32)
    w_route = jnp.concatenate([w_group, w_router, pad], axis=-1)
    w_route_hi = w_route.astype(BF16)
    w_route_lo = (w_route - w_route_hi.astype(F32)).astype(BF16)
    w_route_b = jnp.concatenate([w_route_hi, w_route_lo], axis=-1)
    b_route = jnp.concatenate([b_group, b_router, pad[:, 0, :]], axis=-1).reshape(L, 1, LANES)
    bs_t = jnp.swapaxes(b_spatial, 1, 2)

    kx, vx = _mem_kv(mem, row3(norm_mem), w_xk_b, w_xv_b)

    n_slots = T * TOP_K + N_EXPERTS * BM
    n_blocks = n_slots // BM
    xs = x.reshape(T, D)
    moe = None
    for l in range(L):
        mix_args = (row3(norm_mix), w_in_b, row3(v_norm), w_spatial, bs_t, row3(out_norm_a), l)
        if moe is None:
            q, k, v, ya = _mix_in(xs, *mix_args)
        else:
            xs, q, k, v, ya = _moe_mix_in(*moe, *mix_args)
        yb = _stick_break(q, k, v, B, S).reshape(T, SB_WIDTH)
        x2, hm, info, cnt = _post(xs, ya, yb, row3(out_norm_b), w_out_b, row3(norm_cross), w_xq_b, kx, vx, w_xo_b,
                                  row3(norm_moe), w_route_b, b_route, l, S)
        eid = info[:, 0:TOP_K].astype(jnp.int32)
        rank = info[:, 4:4 + TOP_K].astype(jnp.int32)
        counts = cnt[0, ROUTE_BASE:ROUTE_BASE + N_EXPERTS].astype(jnp.int32)
        padded = ((counts + BM - 1) // BM) * BM
        seg_end = jnp.cumsum(padded)
        seg_start = seg_end - padded
        expert_ids = jnp.arange(N_EXPERTS, dtype=jnp.int32)
        dest = (rank + jnp.sum(jnp.where(eid[..., None] == expert_ids, seg_start, 0), axis=-1)).reshape(-1)
        blk_row = jnp.arange(n_blocks, dtype=jnp.int32) * BM
        blk_e = jnp.minimum(
            jnp.sum((seg_end[None, :] <= blk_row[:, None]).astype(jnp.int32), axis=1), N_EXPERTS - 1)
        n_used = (seg_end[-1:] // BM).astype(jnp.int32)
        buf = _dispatch(dest, hm, jnp.zeros((n_slots * PACK_TILES, LANES), U32))
        ybuf = _experts(blk_e, n_used, buf, w1, w3, w2, l)
        moe = (dest, x2, info, ybuf)
    dest, x2, info, ybuf = moe
    return _combine_final(dest, x2, info, norm_final.reshape(1, D), ybuf).reshape(B, S, D)
```

```python
import functools

import jax
import jax.numpy as jnp
from jax import lax
from jax.experimental import pallas as pl
from jax.experimental.pallas import tpu as pltpu

F32 = jnp.float32
BF16 = jnp.bfloat16
U32 = jnp.uint32

D_MODEL = 1024
A_WIDTH = 512
A_GROUPS = 4
A_CH = 128
CHUNK = 128
SB_WIDTH = 512
SB_HEADS = 8
SB_HEAD_DIM = 64
SB_PAIRS = SB_HEADS // 2
IN_WIDTH = 2 * A_WIDTH + 3 * SB_WIDTH
X_HEADS = 4
X_HEAD_DIM = 256
N_GROUPS = 4
EXPERTS_PER_GROUP = 8
N_EXPERTS = 32
TOP_K = 2
D_EXPERT = 512
EPS = 1e-6

LANES = 128
SUBLANES = 8
ROW_TILES = D_MODEL // LANES
PACK_TILES = ROW_TILES // 2

TM_IN = 512
TQ = 256
TK = 256
HALF = TQ // 2
TM_POST = 512
BM = 512
TE = 256
ROUTE_BASE = N_GROUPS
NEG = -1e30
LOG2E = 1.4426950408889634

VMEM_LIMIT = 52 * 1024 * 1024


def _rms(x, g):
    return x * lax.rsqrt(jnp.mean(x * x, axis=-1, keepdims=True) + EPS) * g


def _dot(a, b):
    return jnp.dot(a, b, preferred_element_type=F32)


def _dot_nt(a, b):
    return lax.dot_general(a, b, (((1,), (1,)), ((), ())), preferred_element_type=F32)


def _split_bf16(x):
    hi = x.astype(BF16)
    lo = (x - hi.astype(F32)).astype(BF16)
    return hi, lo


def _row_copy(src, src_row, dst, dst_row, sem):
    return pltpu.make_async_copy(
        src.at[pl.ds(pl.multiple_of(src_row * PACK_TILES, PACK_TILES), PACK_TILES), :],
        dst.at[pl.ds(pl.multiple_of(dst_row * PACK_TILES, PACK_TILES), PACK_TILES), :],
        sem)


def _store_packed(x, ref):
    half = D_MODEL // 2
    hi = pltpu.bitcast(x[:, 0:half].astype(BF16).astype(F32), U32)
    lo = pltpu.bitcast(x[:, half:D_MODEL].astype(BF16).astype(F32), U32)
    words = hi | (lo >> 16)
    for c in range(PACK_TILES):
        ref[pl.ds(c, x.shape[0], stride=PACK_TILES), :] = words[:, c * LANES:(c + 1) * LANES]


def _load_packed_chunks(ref, row0, rows):
    words = [ref[pl.ds(row0 * PACK_TILES + c, rows, stride=PACK_TILES), :] for c in range(PACK_TILES)]
    return ([pltpu.bitcast(w & jnp.uint32(0xFFFF0000), F32) for w in words]
            + [pltpu.bitcast(w << 16, F32) for w in words])


def _moe_output_chunks(dest_ref, x2_ref, info_ref, ybuf_ref, rows_ref, sems):
    tm = x2_ref.shape[0]
    i = pl.program_id(0)

    def start_gather(step, slot):
        base = step * (tm * TOP_K)

        def issue(r, c):
            for k in range(TOP_K):
                _row_copy(ybuf_ref, dest_ref[base + TOP_K * r + k], rows_ref.at[slot], k * tm + r,
                          sems.at[slot]).start(priority=1)
            return c

        lax.fori_loop(0, tm, issue, 0)

    @pl.when(i == 0)
    def _():
        start_gather(0, 0)

    @pl.when(i + 1 < pl.num_programs(0))
    def _():
        start_gather(i + 1, (i + 1) % 2)

    slot = i % 2
    rows = rows_ref.at[slot]
    pltpu.make_async_copy(ybuf_ref.at[pl.ds(0, TOP_K * tm * PACK_TILES), :], rows, sems.at[slot]).wait()

    info = info_ref[...]
    gate1 = info[:, 2:3]
    gate2 = info[:, 3:4]
    y1 = _load_packed_chunks(rows, 0, tm)
    y2 = _load_packed_chunks(rows, tm, tm)
    return [x2_ref[:, c * LANES:(c + 1) * LANES] + (gate1 * y1[c] + gate2 * y2[c]) for c in range(ROW_TILES)]


def _moe_gather_scratch(tm):
    return [pltpu.VMEM((2, TOP_K * tm * PACK_TILES, LANES), U32), pltpu.SemaphoreType.DMA((2,))]


def _mix_in_body(x, g_ref, w_ref, vg_ref, ws_ref, bs_ref, ga_ref, q_ref, k_ref, v_ref, ya_ref):
    tm = x.shape[0]
    h = _rms(x, g_ref[...]).astype(BF16)
    z = _dot(h, w_ref[...])

    t_idx = lax.broadcasted_iota(jnp.int32, (CHUNK, CHUNK), 0)
    s_idx = lax.broadcasted_iota(jnp.int32, (CHUNK, CHUNK), 1)
    causal = s_idx <= t_idx
    parts = []
    ssq = jnp.zeros((tm, 1), F32)
    for g in range(A_GROUPS):
        lanes = slice(g * A_CH, (g + 1) * A_CH)
        u = jax.nn.gelu(z[:, g * A_CH:(g + 1) * A_CH])
        vg = jax.nn.gelu(z[:, A_WIDTH + g * A_CH:A_WIDTH + (g + 1) * A_CH])
        vn = _rms(vg, vg_ref[:, lanes]).astype(BF16)
        ws = jnp.where(causal, ws_ref[g], 0.0).astype(BF16)
        bias = bs_ref[:, g:g + 1]
        mixed = jnp.concatenate(
            [_dot(ws, vn[c * CHUNK:(c + 1) * CHUNK, :]) + bias for c in range(tm // CHUNK)], axis=0)
        ya = u * mixed
        parts.append(ya)
        ssq = ssq + jnp.sum(ya * ya, axis=-1, keepdims=True)
    inv = lax.rsqrt(ssq * (1.0 / A_WIDTH) + EPS)
    for g in range(A_GROUPS):
        lanes = slice(g * A_CH, (g + 1) * A_CH)
        ya_ref[:, lanes] = (parts[g] * inv * ga_ref[:, lanes]).astype(BF16)

    lane = lax.broadcasted_iota(jnp.int32, (1, LANES), 1)
    q0 = 2 * A_WIDTH
    k0 = q0 + SB_WIDTH
    v0 = k0 + SB_WIDTH
    scale = 1.0 / (SB_HEAD_DIM ** 0.5)
    for hd in range(SB_HEADS):
        pair = hd // 2
        keep = (lane < SB_HEAD_DIM) if hd % 2 == 0 else (lane >= SB_HEAD_DIM)
        qs = z[:, q0 + pair * LANES:q0 + (pair + 1) * LANES] * scale
        vs = z[:, v0 + pair * LANES:v0 + (pair + 1) * LANES]
        q_ref[:, hd * LANES:(hd + 1) * LANES] = jnp.where(keep, qs, 0.0).astype(BF16)
        v_ref[:, hd * LANES:(hd + 1) * LANES] = jnp.where(keep, vs, 0.0).astype(BF16)
    k_ref[...] = z[:, k0:v0].astype(BF16)


def _mix_in_kernel(x_ref, *refs):
    _mix_in_body(x_ref[...], *refs)


def _moe_mix_in_kernel(dest_ref, x2_ref, info_ref, ybuf_ref, g_ref, w_ref, vg_ref, ws_ref, bs_ref, ga_ref,
                       x_ref, q_ref, k_ref, v_ref, ya_ref, rows_ref, sems):
    x = jnp.concatenate(_moe_output_chunks(dest_ref, x2_ref, info_ref, ybuf_ref, rows_ref, sems), axis=-1)
    x_ref[...] = x
    _mix_in_body(x, g_ref, w_ref, vg_ref, ws_ref, bs_ref, ga_ref, q_ref, k_ref, v_ref, ya_ref)


def _mix_in_specs(l, index):
    lay = lambda *a: (l, 0, 0)
    in_specs = [
        pl.BlockSpec((None, 1, D_MODEL), lay),
        pl.BlockSpec((None, D_MODEL, IN_WIDTH), lay),
        pl.BlockSpec((None, 1, A_WIDTH), lay),
        pl.BlockSpec((None, A_GROUPS, CHUNK, CHUNK), lambda *a: (l, 0, 0, 0)),
        pl.BlockSpec((None, CHUNK, A_GROUPS), lay),
        pl.BlockSpec((None, 1, A_WIDTH), lay),
    ]
    out_specs = [
        pl.BlockSpec((TM_IN, SB_HEADS * LANES), index),
        pl.BlockSpec((TM_IN, SB_WIDTH), index),
        pl.BlockSpec((TM_IN, SB_HEADS * LANES), index),
        pl.BlockSpec((TM_IN, A_WIDTH), index),
    ]
    return in_specs, out_specs


def _mix_in_out_shapes(T):
    return [
        jax.ShapeDtypeStruct((T, SB_HEADS * LANES), BF16),
        jax.ShapeDtypeStruct((T, SB_WIDTH), BF16),
        jax.ShapeDtypeStruct((T, SB_HEADS * LANES), BF16),
        jax.ShapeDtypeStruct((T, A_WIDTH), BF16),
    ]


def _mix_in(x2d, g, w_in, vg, ws, bs_t, ga, l):
    T = x2d.shape[0]
    row = lambda i: (i, 0)
    in_specs, out_specs = _mix_in_specs(l, row)
    return pl.pallas_call(
        _mix_in_kernel,
        grid=(T // TM_IN,),
        in_specs=[pl.BlockSpec((TM_IN, D_MODEL), row)] + in_specs,
        out_specs=out_specs,
        out_shape=_mix_in_out_shapes(T),
        compiler_params=pltpu.CompilerParams(dimension_semantics=("arbitrary",), vmem_limit_bytes=VMEM_LIMIT),
        name="mix_in",
    )(x2d, g, w_in, vg, ws, bs_t, ga)


def _moe_mix_in(dest, x2, info, ybuf, g, w_in, vg, ws, bs_t, ga, l):
    T = x2.shape[0]
    row = lambda i, d: (i, 0)
    in_specs, out_specs = _mix_in_specs(l, row)
    return pl.pallas_call(
        _moe_mix_in_kernel,
        grid_spec=pltpu.PrefetchScalarGridSpec(
            num_scalar_prefetch=1,
            grid=(T // TM_IN,),
            in_specs=[
                pl.BlockSpec((TM_IN, D_MODEL), row),
                pl.BlockSpec((TM_IN, LANES), row),
                pl.BlockSpec(memory_space=pl.ANY),
            ] + in_specs,
            out_specs=[pl.BlockSpec((TM_IN, D_MODEL), row)] + out_specs,
            scratch_shapes=_moe_gather_scratch(TM_IN),
        ),
        out_shape=[jax.ShapeDtypeStruct((T, D_MODEL), F32)] + _mix_in_out_shapes(T),
        compiler_params=pltpu.CompilerParams(dimension_semantics=("arbitrary",), vmem_limit_bytes=VMEM_LIMIT),
        name="moe_mix_in",
    )(dest, x2, info, ybuf, g, w_in, vg, ws, bs_t, ga)


def _stick_break_kernel(q_ref, k_ref, v_ref, o_ref, c_ref, live_ref):
    qi = pl.program_id(1)
    r_idx = lax.broadcasted_iota(jnp.int32, (TK, TK), 0)
    c_idx = lax.broadcasted_iota(jnp.int32, (TK, TK), 1)
    later = jnp.where(c_idx < r_idx, -1.0, 0.0).astype(BF16)

    def sweep(half, key0, n_keys, later_m, visible):
        rows = slice(half * HALF, (half + 1) * HALF)
        first = visible is not None
        sps, lszs = [], []
        for pr in range(SB_PAIRS):
            kb = k_ref[pl.ds(key0, n_keys), pr * LANES:(pr + 1) * LANES]
            qq = jnp.concatenate([q_ref[rows, hd * LANES:(hd + 1) * LANES] for hd in (2 * pr, 2 * pr + 1)], axis=0)
            zz = _dot_nt(qq, kb)
            for z in (zz[0:HALF, :], zz[HALF:2 * HALF, :]):
                sp = jnp.maximum(z, 0.0) + jnp.log(1.0 + jnp.exp2(jnp.abs(z) * (-LOG2E)))
                lszs.append(z - sp)
                sps.append(jnp.where(visible, sp, 0.0) if first else sp)
        after = _dot(jnp.concatenate([sp.astype(BF16) for sp in sps], axis=0), later_m)
        top = None
        for pr in range(SB_PAIRS):
            acc = None
            for hd in (2 * pr, 2 * pr + 1):
                p = jnp.exp(lszs[hd] + after[hd * HALF:(hd + 1) * HALF, :])
                if first:
                    p = jnp.where(visible, p, 0.0)
                term = _dot(p.astype(BF16), v_ref[pl.ds(key0, n_keys), hd * LANES:(hd + 1) * LANES])
                c = -jnp.sum(sps[hd], axis=-1, keepdims=True)
                if not first:
                    term = term * jnp.exp(c_ref[hd, rows, :])
                    c = c_ref[hd, rows, :] + c
                c_ref[hd, rows, :] = c
                acc = term if acc is None else acc + term
                top = jnp.exp(c) if top is None else jnp.maximum(top, jnp.exp(c))
            lanes = slice(pr * LANES, (pr + 1) * LANES)
            o_ref[rows, lanes] = acc if first else o_ref[rows, lanes] + acc
        live_ref[half] = (jnp.max(top) > 0.0).astype(jnp.int32)

    d0 = pl.multiple_of(qi * TK, TK)
    rows_a = lax.broadcasted_iota(jnp.int32, (HALF, HALF), 0)
    cols_a = lax.broadcasted_iota(jnp.int32, (HALF, HALF), 1)
    sweep(0, d0, HALF, later[0:HALF, 0:HALF], cols_a < rows_a)
    rows_b = lax.broadcasted_iota(jnp.int32, (HALF, TK), 0)
    cols_b = lax.broadcasted_iota(jnp.int32, (HALF, TK), 1)
    sweep(1, d0, TK, later, cols_b < rows_b + HALF)

    def cond(jj):
        return jnp.logical_and(jj < qi, live_ref[0] + live_ref[1] > 0)

    def body(jj):
        key0 = pl.multiple_of((qi - 1 - jj) * TK, TK)
        for half in range(2):
            @pl.when(live_ref[half] > 0)
            def _():
                sweep(half, key0, TK, later, None)
        return jj + 1

    lax.while_loop(cond, body, jnp.int32(0))


def _stick_break(q, k, v, B, S):
    return pl.pallas_call(
        _stick_break_kernel,
        grid=(B, S // TQ),
        in_specs=[
            pl.BlockSpec((None, TQ, SB_HEADS * LANES), lambda b, i: (b, i, 0)),
            pl.BlockSpec((None, S, SB_WIDTH), lambda b, i: (b, 0, 0)),
            pl.BlockSpec((None, S, SB_HEADS * LANES), lambda b, i: (b, 0, 0)),
        ],
        out_specs=pl.BlockSpec((None, TQ, SB_WIDTH), lambda b, i: (b, i, 0)),
        out_shape=jax.ShapeDtypeStruct((B, S, SB_WIDTH), F32),
        scratch_shapes=[pltpu.VMEM((SB_HEADS, TQ, 1), F32), pltpu.SMEM((2,), jnp.int32)],
        compiler_params=pltpu.CompilerParams(
            dimension_semantics=("arbitrary", "arbitrary"), vmem_limit_bytes=VMEM_LIMIT),
        name="stick_break",
    )(q.reshape(B, S, SB_HEADS * LANES), k.reshape(B, S, SB_WIDTH), v.reshape(B, S, SB_HEADS * LANES))


def _mem_kv_kernel(m_ref, g_ref, wk_ref, wv_ref, k_ref, v_ref):
    m = _rms(m_ref[...], g_ref[...]).astype(BF16)
    k_ref[...] = _dot(m, wk_ref[...]).astype(BF16)
    v_ref[...] = _dot(m, wv_ref[...]).astype(BF16)


def _mem_kv(mem, g, wk, wv):
    B, M, _ = mem.shape
    L = wk.shape[0]
    kv_spec = pl.BlockSpec((None, None, M, D_MODEL), lambda l, b: (l, b, 0, 0))
    w_spec = pl.BlockSpec((None, D_MODEL, D_MODEL), lambda l, b: (l, 0, 0))
    return pl.pallas_call(
        _mem_kv_kernel,
        grid=(L, B),
        in_specs=[
            pl.BlockSpec((None, M, D_MODEL), lambda l, b: (b, 0, 0)),
            pl.BlockSpec((None, 1, D_MODEL), lambda l, b: (l, 0, 0)),
            w_spec, w_spec,
        ],
        out_specs=[kv_spec, kv_spec],
        out_shape=[jax.ShapeDtypeStruct((L, B, M, D_MODEL), BF16)] * 2,
        compiler_params=pltpu.CompilerParams(
            dimension_semantics=("arbitrary", "arbitrary"), vmem_limit_bytes=VMEM_LIMIT),
        name="mem_kv",
    )(mem, g, wk, wv)


def _post_kernel(x_ref, ya_ref, yb_ref, gb_ref, wo_ref, gc_ref, wq_ref, kx_ref, vx_ref, wxo_ref, gm_ref,
                 wr_ref, br_ref, x2_ref, info_ref, cnt_ref, buf_ref,
                 run_ref, hm_ref, slot_v_ref, slot_s_ref, sem_slots, sem_rows, *, cap):
    tm = x_ref.shape[0]
    i = pl.program_id(0)

    def wait_rows():
        for k in range(TOP_K):
            pltpu.make_async_copy(hm_ref, buf_ref.at[pl.ds(0, tm * PACK_TILES), :], sem_rows).wait()

    @pl.when(i == 0)
    def _():
        run_ref[...] = jnp.zeros_like(run_ref)

    ybn = _rms(yb_ref[...], gb_ref[...]).astype(BF16)
    x1 = x_ref[...] + _dot(jnp.concatenate([ya_ref[...], ybn], axis=-1), wo_ref[...])

    h = _rms(x1, gc_ref[...]).astype(BF16)
    q = (_dot(h, wq_ref[...]) * (1.0 / (X_HEAD_DIM ** 0.5))).astype(BF16)
    heads = []
    for hd in range(X_HEADS):
        cols = slice(hd * X_HEAD_DIM, (hd + 1) * X_HEAD_DIM)
        s = _dot_nt(q[:, cols], kx_ref[:, cols])
        p = jnp.exp(s - jnp.max(s, axis=-1, keepdims=True))
        p = p / jnp.sum(p, axis=-1, keepdims=True)
        heads.append(_dot(p.astype(BF16), vx_ref[:, cols]).astype(BF16))
    x2 = x1 + _dot(jnp.concatenate(heads, axis=-1), wxo_ref[...])
    x2_ref[...] = x2

    hm = _rms(x2, gm_ref[...])

    @pl.when(i > 0)
    def _():
        wait_rows()

    _store_packed(hm, hm_ref)

    hm_hi, hm_lo = _split_bf16(hm)
    both = _dot(hm_hi, wr_ref[...])
    logits = both[:, 0:LANES] + both[:, LANES:2 * LANES] + _dot(hm_lo, wr_ref[:, 0:LANES]) + br_ref[...]

    lane = lax.broadcasted_iota(jnp.int32, (tm, LANES), 1)
    lanef = lane.astype(F32)
    big = float(LANES)
    is_group = lane < N_GROUPS
    gl = jnp.where(is_group, logits, NEG)
    gmax = jnp.max(gl, axis=-1, keepdims=True)
    gsel = jnp.min(jnp.where(gl == gmax, lanef, big), axis=-1, keepdims=True)
    gden = jnp.sum(jnp.where(is_group, jnp.exp(gl - gmax), 0.0), axis=-1, keepdims=True)
    g_gate = 1.0 / gden
    lo = ROUTE_BASE + EXPERTS_PER_GROUP * gsel
    in_group = (lanef >= lo) & (lanef < lo + EXPERTS_PER_GROUP)
    el = jnp.where(in_group, logits, NEG)
    v1 = jnp.max(el, axis=-1, keepdims=True)
    i1 = jnp.min(jnp.where(el == v1, lanef, big), axis=-1, keepdims=True)
    el2 = jnp.where(lanef == i1, NEG, el)
    v2 = jnp.max(el2, axis=-1, keepdims=True)
    i2 = jnp.min(jnp.where(el2 == v2, lanef, big), axis=-1, keepdims=True)
    t = jnp.exp(v2 - v1)
    den = 1.0 + t
    gate1 = g_gate * (1.0 / den)
    gate2 = g_gate * (t / den)

    hit1 = lanef == i1
    hit2 = lanef == i2
    multi = jnp.where(hit1 | hit2, 1.0, 0.0)
    r_idx = lax.broadcasted_iota(jnp.int32, (tm, tm), 0)
    c_idx = lax.broadcasted_iota(jnp.int32, (tm, tm), 1)
    earlier = jnp.where(c_idx < r_idx, 1.0, 0.0).astype(BF16)
    before = _dot(earlier, multi.astype(BF16)) + run_ref[0:1, :]
    rank1 = jnp.sum(jnp.where(hit1, before, 0.0), axis=-1, keepdims=True)
    rank2 = jnp.sum(jnp.where(hit2, before, 0.0), axis=-1, keepdims=True)
    run_ref[...] = run_ref[...] + jnp.sum(multi, axis=0, keepdims=True)
    cnt_ref[...] = run_ref[...]

    slot1 = (i1 - ROUTE_BASE) * cap + rank1
    slot2 = (i2 - ROUTE_BASE) * cap + rank2
    info = jnp.where(lane == 0, slot1, 0.0)
    info = jnp.where(lane == 1, slot2, info)
    info = jnp.where(lane == 2, gate1, info)
    info = jnp.where(lane == 3, gate2, info)
    info_ref[...] = info

    slot_v_ref[...] = info.astype(jnp.int32)
    to_smem = pltpu.make_async_copy(slot_v_ref, slot_s_ref, sem_slots)
    to_smem.start()
    to_smem.wait()

    def issue(r, c):
        for k in range(TOP_K):
            _row_copy(hm_ref, r, buf_ref, slot_s_ref[r, k], sem_rows).start(priority=k)
        return c

    lax.fori_loop(0, tm, issue, 0)

    @pl.when(i == pl.num_programs(0) - 1)
    def _():
        wait_rows()


def _post(x2d, ya, yb, gb, wo, gc, wq, kx, vx, wxo, gm, wr, br, l, S, cap):
    T = x2d.shape[0]
    M = kx.shape[2]
    grid = (T // TM_POST,)
    row = lambda i: (i, 0)
    lay = lambda i: (l, 0, 0)
    batch = lambda i: (l, (i * TM_POST) // S, 0, 0)
    wspec = pl.BlockSpec((None, D_MODEL, D_MODEL), lay)
    gspec = pl.BlockSpec((None, 1, D_MODEL), lay)
    return pl.pallas_call(
        functools.partial(_post_kernel, cap=cap),
        grid=grid,
        in_specs=[
            pl.BlockSpec((TM_POST, D_MODEL), row),
            pl.BlockSpec((TM_POST, A_WIDTH), row),
            pl.BlockSpec((TM_POST, SB_WIDTH), row),
            pl.BlockSpec((None, 1, SB_WIDTH), lay),
            wspec, gspec, wspec,
            pl.BlockSpec((None, None, M, D_MODEL), batch),
            pl.BlockSpec((None, None, M, D_MODEL), batch),
            wspec, gspec,
            pl.BlockSpec((None, D_MODEL, 2 * LANES), lay),
            pl.BlockSpec((None, 1, LANES), lay),
        ],
        out_specs=[
            pl.BlockSpec((TM_POST, D_MODEL), row),
            pl.BlockSpec((TM_POST, LANES), row),
            pl.BlockSpec((SUBLANES, LANES), lambda i: (0, 0)),
            pl.BlockSpec(memory_space=pl.ANY),
        ],
        out_shape=[
            jax.ShapeDtypeStruct((T, D_MODEL), F32),
            jax.ShapeDtypeStruct((T, LANES), F32),
            jax.ShapeDtypeStruct((SUBLANES, LANES), F32),
            jax.ShapeDtypeStruct((N_EXPERTS * cap * PACK_TILES, LANES), U32),
        ],
        scratch_shapes=[
            pltpu.VMEM((SUBLANES, LANES), F32),
            pltpu.VMEM((TM_POST * PACK_TILES, LANES), U32),
            pltpu.VMEM((TM_POST, LANES), jnp.int32),
            pltpu.SMEM((TM_POST, LANES), jnp.int32),
            pltpu.SemaphoreType.DMA(()),
            pltpu.SemaphoreType.DMA(()),
        ],
        compiler_params=pltpu.CompilerParams(dimension_semantics=("arbitrary",), vmem_limit_bytes=VMEM_LIMIT),
        name="post",
    )(x2d, ya, yb, gb, wo, gc, wq, kx, vx, wxo, gm, wr, br)


def _pad_fill_kernel(tail_ref, buf_in_ref, buf_ref, zeros_ref, sem):
    del buf_in_ref
    zeros_ref[...] = jnp.zeros_like(zeros_ref)
    copies = [
        pltpu.make_async_copy(
            zeros_ref, buf_ref.at[pl.ds(pl.multiple_of(tail_ref[e] * PACK_TILES, PACK_TILES), BM * PACK_TILES), :], sem)
        for e in range(N_EXPERTS)]
    for cp in copies:
        cp.start()
    for cp in copies:
        cp.wait()


def _pad_fill(tail, buf):
    return pl.pallas_call(
        _pad_fill_kernel,
        grid_spec=pltpu.PrefetchScalarGridSpec(
            num_scalar_prefetch=1,
            grid=(1,),
            in_specs=[pl.BlockSpec(memory_space=pl.ANY)],
            out_specs=pl.BlockSpec(memory_space=pl.ANY),
            scratch_shapes=[pltpu.VMEM((BM * PACK_TILES, LANES), U32), pltpu.SemaphoreType.DMA(())],
        ),
        out_shape=jax.ShapeDtypeStruct(buf.shape, U32),
        input_output_aliases={1: 0},
        compiler_params=pltpu.CompilerParams(dimension_semantics=("arbitrary",), vmem_limit_bytes=VMEM_LIMIT),
        name="pad_fill",
    )(tail, buf)


def _experts_kernel(blk_e_ref, blk_at_ref, n_used_ref, buf_ref, w1_ref, w3_ref, w2_ref, y_ref,
                    w1b_ref, w3b_ref, w2b_ref):
    del blk_at_ref
    i = pl.program_id(0)
    expert = blk_e_ref[i]
    previous = blk_e_ref[jnp.maximum(i - 1, 0)]

    @pl.when(jnp.logical_or(i == 0, expert != previous))
    def _():
        w1b_ref[...] = w1_ref[...].astype(BF16)
        w3b_ref[...] = w3_ref[...].astype(BF16)
        w2b_ref[...] = w2_ref[...].astype(BF16)

    @pl.when(i < n_used_ref[0])
    def _():
        xb = jnp.concatenate(_load_packed_chunks(buf_ref, 0, BM), axis=-1).astype(BF16)
        h1 = _dot(xb, w1b_ref[...])
        h3 = _dot(xb, w3b_ref[...])
        a = (h1 * jax.nn.sigmoid(h1) * h3).astype(BF16)
        _store_packed(_dot(a, w2b_ref[...]), y_ref)


def _experts(blk_e, blk_at, n_used, n_blocks, buf, w1, w3, w2, l):
    rows = pl.BlockSpec((BM * PACK_TILES, LANES), lambda i, e, at, n: (at[i], 0))
    return pl.pallas_call(
        _experts_kernel,
        grid_spec=pltpu.PrefetchScalarGridSpec(
            num_scalar_prefetch=3,
            grid=(n_blocks,),
            in_specs=[
                rows,
                pl.BlockSpec((None, None, D_MODEL, D_EXPERT), lambda i, e, at, n: (l, e[i], 0, 0)),
                pl.BlockSpec((None, None, D_MODEL, D_EXPERT), lambda i, e, at, n: (l, e[i], 0, 0)),
                pl.BlockSpec((None, None, D_EXPERT, D_MODEL), lambda i, e, at, n: (l, e[i], 0, 0)),
            ],
            out_specs=rows,
            scratch_shapes=[
                pltpu.VMEM((D_MODEL, D_EXPERT), BF16),
                pltpu.VMEM((D_MODEL, D_EXPERT), BF16),
                pltpu.VMEM((D_EXPERT, D_MODEL), BF16),
            ],
        ),
        out_shape=jax.ShapeDtypeStruct(buf.shape, U32),
        compiler_params=pltpu.CompilerParams(dimension_semantics=("arbitrary",), vmem_limit_bytes=VMEM_LIMIT),
        name="experts",
    )(blk_e, blk_at, n_used, buf, w1, w3, w2)


def _combine_final_kernel(dest_ref, x2_ref, info_ref, gf_ref, ybuf_ref, out_ref, rows_ref, sems):
    chunks = _moe_output_chunks(dest_ref, x2_ref, info_ref, ybuf_ref, rows_ref, sems)
    ssq = jnp.zeros((x2_ref.shape[0], 1), F32)
    for xc in chunks:
        ssq = ssq + jnp.sum(xc * xc, axis=-1, keepdims=True)
    inv = lax.rsqrt(ssq * (1.0 / D_MODEL) + EPS)
    for c, xc in enumerate(chunks):
        lanes = slice(c * LANES, (c + 1) * LANES)
        out_ref[:, lanes] = xc * inv * gf_ref[:, lanes]


def _combine_final(dest, x2, info, gf, ybuf):
    T = x2.shape[0]
    row = lambda i, d: (i, 0)
    return pl.pallas_call(
        _combine_final_kernel,
        grid_spec=pltpu.PrefetchScalarGridSpec(
            num_scalar_prefetch=1,
            grid=(T // TE,),
            in_specs=[
                pl.BlockSpec((TE, D_MODEL), row),
                pl.BlockSpec((TE, LANES), row),
                pl.BlockSpec((1, D_MODEL), lambda i, d: (0, 0)),
                pl.BlockSpec(memory_space=pl.ANY),
            ],
            out_specs=pl.BlockSpec((TE, D_MODEL), row),
            scratch_shapes=_moe_gather_scratch(TE),
        ),
        out_shape=jax.ShapeDtypeStruct((T, D_MODEL), F32),
        compiler_params=pltpu.CompilerParams(dimension_semantics=("arbitrary",), vmem_limit_bytes=VMEM_LIMIT),
        name="combine_final",
    )(dest, x2, info, gf, ybuf)


def kernel(x, mem, norm_mix, w_in, v_norm, w_spatial, b_spatial, out_norm_a, out_norm_b, w_out, norm_cross, norm_mem, w_xq, w_xk, w_xv, w_xo, norm_moe, w_group, b_group, w_router, b_router, w1, w3, w2, norm_final):
    B, S, D = x.shape
    L = w_in.shape[0]
    T = B * S
    assert D == D_MODEL and S % TQ == 0 and T % TM_IN == 0 and S % TM_POST == 0 and T % TE == 0

    row3 = lambda a: a.reshape(L, 1, -1)
    w_in_b, w_out_b = w_in.astype(BF16), w_out.astype(BF16)
    w_xq_b, w_xk_b, w_xv_b, w_xo_b = (w.astype(BF16) for w in (w_xq, w_xk, w_xv, w_xo))
    pad = jnp.zeros((L, D, LANES - N_GROUPS - N_EXPERTS), F32)
    w_route = jnp.concatenate([w_group, w_router, pad], axis=-1)
    w_route_hi = w_route.astype(BF16)
    w_route_lo = (w_route - w_route_hi.astype(F32)).astype(BF16)
    w_route_b = jnp.concatenate([w_route_hi, w_route_lo], axis=-1)
    b_route = jnp.concatenate([b_group, b_router, pad[:, 0, :]], axis=-1).reshape(L, 1, LANES)
    bs_t = jnp.swapaxes(b_spatial, 1, 2)

    kx, vx = _mem_kv(mem, row3(norm_mem), w_xk_b, w_xv_b)

    n_blocks = (T * TOP_K) // BM + N_EXPERTS
    cap = T + BM
    assert cap % BM == 0 and N_EXPERTS * cap < 2 ** 24
    cap_blocks = cap // BM
    xs = x.reshape(T, D)
    moe = None
    for l in range(L):
        mix_args = (row3(norm_mix), w_in_b, row3(v_norm), w_spatial, bs_t, row3(out_norm_a), l)
        if moe is None:
            q, k, v, ya = _mix_in(xs, *mix_args)
        else:
            xs, q, k, v, ya = _moe_mix_in(*moe, *mix_args)
        yb = _stick_break(q, k, v, B, S).reshape(T, SB_WIDTH)
        x2, info, cnt, buf = _post(xs, ya, yb, row3(out_norm_b), w_out_b, row3(norm_cross), w_xq_b, kx, vx, w_xo_b,
                                   row3(norm_moe), w_route_b, b_route, l, S, cap)
        dest = info[:, 0:TOP_K].astype(jnp.int32).reshape(-1)
        counts = cnt[0, ROUTE_BASE:ROUTE_BASE + N_EXPERTS].astype(jnp.int32)
        expert_ids = jnp.arange(N_EXPERTS, dtype=jnp.int32)
        blocks = (counts + BM - 1) // BM
        blk_end = jnp.cumsum(blocks)
        n_used = blk_end[-1:]
        blk = jnp.minimum(jnp.arange(n_blocks, dtype=jnp.int32), n_used - 1)
        blk_e = jnp.minimum(jnp.sum((blk_end[None, :] <= blk[:, None]).astype(jnp.int32), axis=1), N_EXPERTS - 1)
        blk_at = blk_e * cap_blocks + blk - (blk_end - blocks)[blk_e]
        buf = _pad_fill(expert_ids * cap + counts, buf)
        ybuf = _experts(blk_e, blk_at, n_used, n_blocks, buf, w1, w3, w2, l)
        moe = (dest, x2, info, ybuf)
    dest, x2, info, ybuf = moe
    return _combine_final(dest, x2, info, norm_final.reshape(1, D), ybuf).reshape(B, S, D)
```

```python
import functools

import jax
import jax.numpy as jnp
from jax import lax
from jax.experimental import pallas as pl
from jax.experimental.pallas import tpu as pltpu

F32 = jnp.float32
BF16 = jnp.bfloat16
U32 = jnp.uint32

D_MODEL = 1024
A_WIDTH = 512
A_GROUPS = 4
A_CH = 128
CHUNK = 128
SB_WIDTH = 512
SB_HEADS = 8
SB_HEAD_DIM = 64
SB_PAIRS = SB_HEADS // 2
IN_WIDTH = 2 * A_WIDTH + 3 * SB_WIDTH
X_HEADS = 4
X_HEAD_DIM = 256
N_GROUPS = 4
EXPERTS_PER_GROUP = 8
N_EXPERTS = 32
TOP_K = 2
D_EXPERT = 512
EPS = 1e-6

LANES = 128
SUBLANES = 8
ROW_TILES = D_MODEL // LANES
PACK_TILES = ROW_TILES // 2

TM_IN = 512
TQ = 256
TK = 256
HALF = TQ // 2
TM_POST = 512
TD = 512
BM = 512
TE = 256
ROUTE_BASE = N_GROUPS
NEG = -1e30
LOG2E = 1.4426950408889634

VMEM_LIMIT = 52 * 1024 * 1024


def _rms(x, g):
    return x * lax.rsqrt(jnp.mean(x * x, axis=-1, keepdims=True) + EPS) * g


def _dot(a, b):
    return jnp.dot(a, b, preferred_element_type=F32)


def _dot_nt(a, b):
    return lax.dot_general(a, b, (((1,), (1,)), ((), ())), preferred_element_type=F32)


def _split_bf16(x):
    hi = x.astype(BF16)
    lo = (x - hi.astype(F32)).astype(BF16)
    return hi, lo


def _row_copy(src, src_row, dst, dst_row, sem):
    return pltpu.make_async_copy(
        src.at[pl.ds(pl.multiple_of(src_row * PACK_TILES, PACK_TILES), PACK_TILES), :],
        dst.at[pl.ds(pl.multiple_of(dst_row * PACK_TILES, PACK_TILES), PACK_TILES), :],
        sem)


def _store_packed(x, ref):
    half = D_MODEL // 2
    hi = pltpu.bitcast(x[:, 0:half].astype(BF16).astype(F32), U32)
    lo = pltpu.bitcast(x[:, half:D_MODEL].astype(BF16).astype(F32), U32)
    words = hi | (lo >> 16)
    for c in range(PACK_TILES):
        ref[pl.ds(c, x.shape[0], stride=PACK_TILES), :] = words[:, c * LANES:(c + 1) * LANES]


def _load_packed_chunks(ref, row0, rows):
    words = [ref[pl.ds(row0 * PACK_TILES + c, rows, stride=PACK_TILES), :] for c in range(PACK_TILES)]
    return ([pltpu.bitcast(w & jnp.uint32(0xFFFF0000), F32) for w in words]
            + [pltpu.bitcast(w << 16, F32) for w in words])


def _moe_output_chunks(dest_ref, x2_ref, info_ref, ybuf_ref, rows_ref, sems):
    tm = x2_ref.shape[0]
    i = pl.program_id(0)

    def start_gather(step, slot):
        base = step * (tm * TOP_K)

        def issue(r, c):
            for k in range(TOP_K):
                _row_copy(ybuf_ref, dest_ref[base + TOP_K * r + k], rows_ref.at[slot], k * tm + r,
                          sems.at[slot]).start(priority=1)
            return c

        lax.fori_loop(0, tm, issue, 0)

    @pl.when(i == 0)
    def _():
        start_gather(0, 0)

    @pl.when(i + 1 < pl.num_programs(0))
    def _():
        start_gather(i + 1, (i + 1) % 2)

    slot = i % 2
    rows = rows_ref.at[slot]
    pltpu.make_async_copy(ybuf_ref.at[pl.ds(0, TOP_K * tm * PACK_TILES), :], rows, sems.at[slot]).wait()

    info = info_ref[...]
    gate1 = info[:, 2:3]
    gate2 = info[:, 3:4]
    y1 = _load_packed_chunks(rows, 0, tm)
    y2 = _load_packed_chunks(rows, tm, tm)
    return [x2_ref[:, c * LANES:(c + 1) * LANES] + (gate1 * y1[c] + gate2 * y2[c]) for c in range(ROW_TILES)]


def _moe_gather_scratch(tm):
    return [pltpu.VMEM((2, TOP_K * tm * PACK_TILES, LANES), U32), pltpu.SemaphoreType.DMA((2,))]


def _mix_in_body(x, g_ref, w_ref, vg_ref, ws_ref, bs_ref, ga_ref, q_ref, k_ref, v_ref, ya_ref):
    tm = x.shape[0]
    h = _rms(x, g_ref[...]).astype(BF16)
    z = _dot(h, w_ref[...])

    t_idx = lax.broadcasted_iota(jnp.int32, (CHUNK, CHUNK), 0)
    s_idx = lax.broadcasted_iota(jnp.int32, (CHUNK, CHUNK), 1)
    causal = s_idx <= t_idx
    parts = []
    ssq = jnp.zeros((tm, 1), F32)
    for g in range(A_GROUPS):
        lanes = slice(g * A_CH, (g + 1) * A_CH)
        u = jax.nn.gelu(z[:, g * A_CH:(g + 1) * A_CH])
        vg = jax.nn.gelu(z[:, A_WIDTH + g * A_CH:A_WIDTH + (g + 1) * A_CH])
        vn = _rms(vg, vg_ref[:, lanes]).astype(BF16)
        ws = jnp.where(causal, ws_ref[g], 0.0).astype(BF16)
        bias = bs_ref[:, g:g + 1]
        mixed = jnp.concatenate(
            [_dot(ws, vn[c * CHUNK:(c + 1) * CHUNK, :]) + bias for c in range(tm // CHUNK)], axis=0)
        ya = u * mixed
        parts.append(ya)
        ssq = ssq + jnp.sum(ya * ya, axis=-1, keepdims=True)
    inv = lax.rsqrt(ssq * (1.0 / A_WIDTH) + EPS)
    for g in range(A_GROUPS):
        lanes = slice(g * A_CH, (g + 1) * A_CH)
        ya_ref[:, lanes] = (parts[g] * inv * ga_ref[:, lanes]).astype(BF16)

    lane = lax.broadcasted_iota(jnp.int32, (1, LANES), 1)
    q0 = 2 * A_WIDTH
    k0 = q0 + SB_WIDTH
    v0 = k0 + SB_WIDTH
    scale = 1.0 / (SB_HEAD_DIM ** 0.5)
    for hd in range(SB_HEADS):
        pair = hd // 2
        keep = (lane < SB_HEAD_DIM) if hd % 2 == 0 else (lane >= SB_HEAD_DIM)
        qs = z[:, q0 + pair * LANES:q0 + (pair + 1) * LANES] * scale
        vs = z[:, v0 + pair * LANES:v0 + (pair + 1) * LANES]
        q_ref[:, hd * LANES:(hd + 1) * LANES] = jnp.where(keep, qs, 0.0).astype(BF16)
        v_ref[:, hd * LANES:(hd + 1) * LANES] = jnp.where(keep, vs, 0.0).astype(BF16)
    k_ref[...] = z[:, k0:v0].astype(BF16)


def _mix_in_kernel(x_ref, *refs):
    _mix_in_body(x_ref[...], *refs)


def _moe_mix_in_kernel(dest_ref, x2_ref, info_ref, ybuf_ref, g_ref, w_ref, vg_ref, ws_ref, bs_ref, ga_ref,
                       x_ref, q_ref, k_ref, v_ref, ya_ref, rows_ref, sems):
    x = jnp.concatenate(_moe_output_chunks(dest_ref, x2_ref, info_ref, ybuf_ref, rows_ref, sems), axis=-1)
    x_ref[...] = x
    _mix_in_body(x, g_ref, w_ref, vg_ref, ws_ref, bs_ref, ga_ref, q_ref, k_ref, v_ref, ya_ref)


def _mix_in_specs(l, index):
    lay = lambda *a: (l, 0, 0)
    in_specs = [
        pl.BlockSpec((None, 1, D_MODEL), lay),
        pl.BlockSpec((None, D_MODEL, IN_WIDTH), lay),
        pl.BlockSpec((None, 1, A_WIDTH), lay),
        pl.BlockSpec((None, A_GROUPS, CHUNK, CHUNK), lambda *a: (l, 0, 0, 0)),
        pl.BlockSpec((None, CHUNK, A_GROUPS), lay),
        pl.BlockSpec((None, 1, A_WIDTH), lay),
    ]
    out_specs = [
        pl.BlockSpec((TM_IN, SB_HEADS * LANES), index),
        pl.BlockSpec((TM_IN, SB_WIDTH), index),
        pl.BlockSpec((TM_IN, SB_HEADS * LANES), index),
        pl.BlockSpec((TM_IN, A_WIDTH), index),
    ]
    return in_specs, out_specs


def _mix_in_out_shapes(T):
    return [
        jax.ShapeDtypeStruct((T, SB_HEADS * LANES), BF16),
        jax.ShapeDtypeStruct((T, SB_WIDTH), BF16),
        jax.ShapeDtypeStruct((T, SB_HEADS * LANES), BF16),
        jax.ShapeDtypeStruct((T, A_WIDTH), BF16),
    ]


def _mix_in(x2d, g, w_in, vg, ws, bs_t, ga, l):
    T = x2d.shape[0]
    row = lambda i: (i, 0)
    in_specs, out_specs = _mix_in_specs(l, row)
    return pl.pallas_call(
        _mix_in_kernel,
        grid=(T // TM_IN,),
        in_specs=[pl.BlockSpec((TM_IN, D_MODEL), row)] + in_specs,
        out_specs=out_specs,
        out_shape=_mix_in_out_shapes(T),
        compiler_params=pltpu.CompilerParams(dimension_semantics=("arbitrary",), vmem_limit_bytes=VMEM_LIMIT),
        name="mix_in",
    )(x2d, g, w_in, vg, ws, bs_t, ga)


def _moe_mix_in(dest, x2, info, ybuf, g, w_in, vg, ws, bs_t, ga, l):
    T = x2.shape[0]
    row = lambda i, d: (i, 0)
    in_specs, out_specs = _mix_in_specs(l, row)
    return pl.pallas_call(
        _moe_mix_in_kernel,
        grid_spec=pltpu.PrefetchScalarGridSpec(
            num_scalar_prefetch=1,
            grid=(T // TM_IN,),
            in_specs=[
                pl.BlockSpec((TM_IN, D_MODEL), row),
                pl.BlockSpec((TM_IN, LANES), row),
                pl.BlockSpec(memory_space=pl.ANY),
            ] + in_specs,
            out_specs=[pl.BlockSpec((TM_IN, D_MODEL), row)] + out_specs,
            scratch_shapes=_moe_gather_scratch(TM_IN),
        ),
        out_shape=[jax.ShapeDtypeStruct((T, D_MODEL), F32)] + _mix_in_out_shapes(T),
        compiler_params=pltpu.CompilerParams(dimension_semantics=("arbitrary",), vmem_limit_bytes=VMEM_LIMIT),
        name="moe_mix_in",
    )(dest, x2, info, ybuf, g, w_in, vg, ws, bs_t, ga)


def _stick_break_kernel(q_ref, k_ref, v_ref, o_ref, c_ref, live_ref):
    qi = pl.program_id(1)
    r_idx = lax.broadcasted_iota(jnp.int32, (TK, TK), 0)
    c_idx = lax.broadcasted_iota(jnp.int32, (TK, TK), 1)
    later = jnp.where(c_idx < r_idx, -1.0, 0.0).astype(BF16)

    def sweep(half, key0, n_keys, later_m, visible):
        rows = slice(half * HALF, (half + 1) * HALF)
        first = visible is not None
        sps, lszs = [], []
        for pr in range(SB_PAIRS):
            kb = k_ref[pl.ds(key0, n_keys), pr * LANES:(pr + 1) * LANES]
            qq = jnp.concatenate([q_ref[rows, hd * LANES:(hd + 1) * LANES] for hd in (2 * pr, 2 * pr + 1)], axis=0)
            zz = _dot_nt(qq, kb)
            for z in (zz[0:HALF, :], zz[HALF:2 * HALF, :]):
                sp = jnp.maximum(z, 0.0) + jnp.log(1.0 + jnp.exp2(jnp.abs(z) * (-LOG2E)))
                lszs.append(z - sp)
                sps.append(jnp.where(visible, sp, 0.0) if first else sp)
        after = _dot(jnp.concatenate([sp.astype(BF16) for sp in sps], axis=0), later_m)
        top = None
        for pr in range(SB_PAIRS):
            acc = None
            for hd in (2 * pr, 2 * pr + 1):
                p = jnp.exp(lszs[hd] + after[hd * HALF:(hd + 1) * HALF, :])
                if first:
                    p = jnp.where(visible, p, 0.0)
                term = _dot(p.astype(BF16), v_ref[pl.ds(key0, n_keys), hd * LANES:(hd + 1) * LANES])
                c = -jnp.sum(sps[hd], axis=-1, keepdims=True)
                if not first:
                    term = term * jnp.exp(c_ref[hd, rows, :])
                    c = c_ref[hd, rows, :] + c
                c_ref[hd, rows, :] = c
                acc = term if acc is None else acc + term
                top = jnp.exp(c) if top is None else jnp.maximum(top, jnp.exp(c))
            lanes = slice(pr * LANES, (pr + 1) * LANES)
            o_ref[rows, lanes] = acc if first else o_ref[rows, lanes] + acc
        live_ref[half] = (jnp.max(top) > 0.0).astype(jnp.int32)

    d0 = pl.multiple_of(qi * TK, TK)
    rows_a = lax.broadcasted_iota(jnp.int32, (HALF, HALF), 0)
    cols_a = lax.broadcasted_iota(jnp.int32, (HALF, HALF), 1)
    sweep(0, d0, HALF, later[0:HALF, 0:HALF], cols_a < rows_a)
    rows_b = lax.broadcasted_iota(jnp.int32, (HALF, TK), 0)
    cols_b = lax.broadcasted_iota(jnp.int32, (HALF, TK), 1)
    sweep(1, d0, TK, later, cols_b < rows_b + HALF)

    def cond(jj):
        return jnp.logical_and(jj < qi, live_ref[0] + live_ref[1] > 0)

    def body(jj):
        key0 = pl.multiple_of((qi - 1 - jj) * TK, TK)
        for half in range(2):
            @pl.when(live_ref[half] > 0)
            def _():
                sweep(half, key0, TK, later, None)
        return jj + 1

    lax.while_loop(cond, body, jnp.int32(0))


def _stick_break(q, k, v, B, S):
    return pl.pallas_call(
        _stick_break_kernel,
        grid=(B, S // TQ),
        in_specs=[
            pl.BlockSpec((None, TQ, SB_HEADS * LANES), lambda b, i: (b, i, 0)),
            pl.BlockSpec((None, S, SB_WIDTH), lambda b, i: (b, 0, 0)),
            pl.BlockSpec((None, S, SB_HEADS * LANES), lambda b, i: (b, 0, 0)),
        ],
        out_specs=pl.BlockSpec((None, TQ, SB_WIDTH), lambda b, i: (b, i, 0)),
        out_shape=jax.ShapeDtypeStruct((B, S, SB_WIDTH), F32),
        scratch_shapes=[pltpu.VMEM((SB_HEADS, TQ, 1), F32), pltpu.SMEM((2,), jnp.int32)],
        compiler_params=pltpu.CompilerParams(
            dimension_semantics=("arbitrary", "arbitrary"), vmem_limit_bytes=VMEM_LIMIT),
        name="stick_break",
    )(q.reshape(B, S, SB_HEADS * LANES), k.reshape(B, S, SB_WIDTH), v.reshape(B, S, SB_HEADS * LANES))


def _mem_kv_kernel(m_ref, g_ref, wk_ref, wv_ref, k_ref, v_ref):
    m = _rms(m_ref[...], g_ref[...]).astype(BF16)
    k_ref[...] = _dot(m, wk_ref[...]).astype(BF16)
    v_ref[...] = _dot(m, wv_ref[...]).astype(BF16)


def _mem_kv(mem, g, wk, wv):
    B, M, _ = mem.shape
    L = wk.shape[0]
    kv_spec = pl.BlockSpec((None, None, M, D_MODEL), lambda l, b: (l, b, 0, 0))
    w_spec = pl.BlockSpec((None, D_MODEL, D_MODEL), lambda l, b: (l, 0, 0))
    return pl.pallas_call(
        _mem_kv_kernel,
        grid=(L, B),
        in_specs=[
            pl.BlockSpec((None, M, D_MODEL), lambda l, b: (b, 0, 0)),
            pl.BlockSpec((None, 1, D_MODEL), lambda l, b: (l, 0, 0)),
            w_spec, w_spec,
        ],
        out_specs=[kv_spec, kv_spec],
        out_shape=[jax.ShapeDtypeStruct((L, B, M, D_MODEL), BF16)] * 2,
        compiler_params=pltpu.CompilerParams(
            dimension_semantics=("arbitrary", "arbitrary"), vmem_limit_bytes=VMEM_LIMIT),
        name="mem_kv",
    )(mem, g, wk, wv)


def _post_kernel(x_ref, ya_ref, yb_ref, gb_ref, wo_ref, gc_ref, wq_ref, kx_ref, vx_ref, wxo_ref, gm_ref,
                 wr_ref, br_ref, x2_ref, hm_ref, info_ref, cnt_ref, run_ref):
    tm = x_ref.shape[0]

    @pl.when(pl.program_id(0) == 0)
    def _():
        run_ref[...] = jnp.zeros_like(run_ref)

    ybn = _rms(yb_ref[...], gb_ref[...]).astype(BF16)
    x1 = x_ref[...] + _dot(jnp.concatenate([ya_ref[...], ybn], axis=-1), wo_ref[...])

    h = _rms(x1, gc_ref[...]).astype(BF16)
    q = (_dot(h, wq_ref[...]) * (1.0 / (X_HEAD_DIM ** 0.5))).astype(BF16)
    heads = []
    for hd in range(X_HEADS):
        cols = slice(hd * X_HEAD_DIM, (hd + 1) * X_HEAD_DIM)
        s = _dot_nt(q[:, cols], kx_ref[:, cols])
        p = jnp.exp(s - jnp.max(s, axis=-1, keepdims=True))
        p = p / jnp.sum(p, axis=-1, keepdims=True)
        heads.append(_dot(p.astype(BF16), vx_ref[:, cols]).astype(BF16))
    x2 = x1 + _dot(jnp.concatenate(heads, axis=-1), wxo_ref[...])
    x2_ref[...] = x2

    hm = _rms(x2, gm_ref[...])
    _store_packed(hm, hm_ref)

    hm_hi, hm_lo = _split_bf16(hm)
    both = _dot(hm_hi, wr_ref[...])
    logits = both[:, 0:LANES] + both[:, LANES:2 * LANES] + _dot(hm_lo, wr_ref[:, 0:LANES]) + br_ref[...]

    lane = lax.broadcasted_iota(jnp.int32, (tm, LANES), 1)
    lanef = lane.astype(F32)
    big = float(LANES)
    is_group = lane < N_GROUPS
    gl = jnp.where(is_group, logits, NEG)
    gmax = jnp.max(gl, axis=-1, keepdims=True)
    gsel = jnp.min(jnp.where(gl == gmax, lanef, big), axis=-1, keepdims=True)
    gden = jnp.sum(jnp.where(is_group, jnp.exp(gl - gmax), 0.0), axis=-1, keepdims=True)
    g_gate = 1.0 / gden
    lo = ROUTE_BASE + EXPERTS_PER_GROUP * gsel
    in_group = (lanef >= lo) & (lanef < lo + EXPERTS_PER_GROUP)
    el = jnp.where(in_group, logits, NEG)
    v1 = jnp.max(el, axis=-1, keepdims=True)
    i1 = jnp.min(jnp.where(el == v1, lanef, big), axis=-1, keepdims=True)
    el2 = jnp.where(lanef == i1, NEG, el)
    v2 = jnp.max(el2, axis=-1, keepdims=True)
    i2 = jnp.min(jnp.where(el2 == v2, lanef, big), axis=-1, keepdims=True)
    t = jnp.exp(v2 - v1)
    den = 1.0 + t
    gate1 = g_gate * (1.0 / den)
    gate2 = g_gate * (t / den)

    hit1 = lanef == i1
    hit2 = lanef == i2
    multi = jnp.where(hit1 | hit2, 1.0, 0.0)
    r_idx = lax.broadcasted_iota(jnp.int32, (tm, tm), 0)
    c_idx = lax.broadcasted_iota(jnp.int32, (tm, tm), 1)
    earlier = jnp.where(c_idx < r_idx, 1.0, 0.0).astype(BF16)
    before = _dot(earlier, multi.astype(BF16)) + run_ref[0:1, :]
    rank1 = jnp.sum(jnp.where(hit1, before, 0.0), axis=-1, keepdims=True)
    rank2 = jnp.sum(jnp.where(hit2, before, 0.0), axis=-1, keepdims=True)
    run_ref[...] = run_ref[...] + jnp.sum(multi, axis=0, keepdims=True)
    cnt_ref[...] = run_ref[...]

    info = jnp.where(lane == 0, i1 - ROUTE_BASE, 0.0)
    info = jnp.where(lane == 1, i2 - ROUTE_BASE, info)
    info = jnp.where(lane == 2, gate1, info)
    info = jnp.where(lane == 3, gate2, info)
    info = jnp.where(lane == 4, rank1, info)
    info = jnp.where(lane == 5, rank2, info)
    info_ref[...] = info


def _post(x2d, ya, yb, gb, wo, gc, wq, kx, vx, wxo, gm, wr, br, l, S):
    T = x2d.shape[0]
    M = kx.shape[2]
    grid = (T // TM_POST,)
    row = lambda i: (i, 0)
    lay = lambda i: (l, 0, 0)
    batch = lambda i: (l, (i * TM_POST) // S, 0, 0)
    wspec = pl.BlockSpec((None, D_MODEL, D_MODEL), lay)
    gspec = pl.BlockSpec((None, 1, D_MODEL), lay)
    return pl.pallas_call(
        _post_kernel,
        grid=grid,
        in_specs=[
            pl.BlockSpec((TM_POST, D_MODEL), row),
            pl.BlockSpec((TM_POST, A_WIDTH), row),
            pl.BlockSpec((TM_POST, SB_WIDTH), row),
            pl.BlockSpec((None, 1, SB_WIDTH), lay),
            wspec, gspec, wspec,
            pl.BlockSpec((None, None, M, D_MODEL), batch),
            pl.BlockSpec((None, None, M, D_MODEL), batch),
            wspec, gspec,
            pl.BlockSpec((None, D_MODEL, 2 * LANES), lay),
            pl.BlockSpec((None, 1, LANES), lay),
        ],
        out_specs=[
            pl.BlockSpec((TM_POST, D_MODEL), row),
            pl.BlockSpec((TM_POST * PACK_TILES, LANES), row),
            pl.BlockSpec((TM_POST, LANES), row),
            pl.BlockSpec((SUBLANES, LANES), lambda i: (0, 0)),
        ],
        out_shape=[
            jax.ShapeDtypeStruct((T, D_MODEL), F32),
            jax.ShapeDtypeStruct((T * PACK_TILES, LANES), U32),
            jax.ShapeDtypeStruct((T, LANES), F32),
            jax.ShapeDtypeStruct((SUBLANES, LANES), F32),
        ],
        scratch_shapes=[pltpu.VMEM((SUBLANES, LANES), F32)],
        compiler_params=pltpu.CompilerParams(dimension_semantics=("arbitrary",), vmem_limit_bytes=VMEM_LIMIT),
        name="post",
    )(x2d, ya, yb, gb, wo, gc, wq, kx, vx, wxo, gm, wr, br)


def _dispatch_kernel(dest_ref, hm_ref, buf_in_ref, buf_ref, sem):
    del buf_in_ref
    base = pl.program_id(0) * (TD * TOP_K)

    def issue(r, c):
        for k in range(TOP_K):
            _row_copy(hm_ref, r, buf_ref, dest_ref[base + TOP_K * r + k], sem).start(priority=k)
        return c

    lax.fori_loop(0, TD, issue, 0)
    for k in range(TOP_K):
        pltpu.make_async_copy(hm_ref, buf_ref.at[pl.ds(0, TD * PACK_TILES), :], sem).wait()


def _dispatch(dest, hm, buf0):
    T = hm.shape[0] // PACK_TILES
    return pl.pallas_call(
        _dispatch_kernel,
        grid_spec=pltpu.PrefetchScalarGridSpec(
            num_scalar_prefetch=1,
            grid=(T // TD,),
            in_specs=[
                pl.BlockSpec((TD * PACK_TILES, LANES), lambda i, d: (i, 0)),
                pl.BlockSpec(memory_space=pl.ANY),
            ],
            out_specs=pl.BlockSpec(memory_space=pl.ANY),
            scratch_shapes=[pltpu.SemaphoreType.DMA(())],
        ),
        out_shape=jax.ShapeDtypeStruct(buf0.shape, U32),
        input_output_aliases={2: 0},
        compiler_params=pltpu.CompilerParams(dimension_semantics=("arbitrary",), vmem_limit_bytes=VMEM_LIMIT),
        name="dispatch",
    )(dest, hm, buf0)


def _experts_kernel(blk_e_ref, first_ref, seg_ref, next_e_ref, n_used_ref, buf_ref, w1_hbm, w3_hbm, w2_hbm, y_ref,
                    w1s_ref, w3s_ref, w2s_ref, w1b_ref, w3b_ref, w2b_ref, sems, *, layer):
    i = pl.program_id(0)

    def fetch(expert, slot):
        return [pltpu.make_async_copy(w_hbm.at[layer, expert], stage.at[slot], sems.at[slot])
                for w_hbm, stage in ((w1_hbm, w1s_ref), (w3_hbm, w3s_ref), (w2_hbm, w2s_ref))]

    @pl.when(first_ref[i] > 0)
    def _():
        slot = seg_ref[i] % 2

        @pl.when(i == 0)
        def _():
            for cp in fetch(blk_e_ref[0], 0):
                cp.start()

        for cp in fetch(blk_e_ref[i], slot):
            cp.wait()
        w1b_ref[...] = w1s_ref[slot].astype(BF16)
        w3b_ref[...] = w3s_ref[slot].astype(BF16)
        w2b_ref[...] = w2s_ref[slot].astype(BF16)

        @pl.when(next_e_ref[i] >= 0)
        def _():
            for cp in fetch(next_e_ref[i], 1 - slot):
                cp.start()

    @pl.when(i < n_used_ref[0])
    def _():
        xb = jnp.concatenate(_load_packed_chunks(buf_ref, 0, BM), axis=-1).astype(BF16)
        h1 = _dot(xb, w1b_ref[...])
        h3 = _dot(xb, w3b_ref[...])
        a = (h1 * jax.nn.sigmoid(h1) * h3).astype(BF16)
        _store_packed(_dot(a, w2b_ref[...]), y_ref)

    @pl.when(i >= n_used_ref[0])
    def _():
        y_ref[...] = jnp.zeros_like(y_ref)


def _experts(blk_e, first, seg, next_e, n_used, buf, w1, w3, w2, l):
    n_blocks = buf.shape[0] // (BM * PACK_TILES)
    rows = pl.BlockSpec((BM * PACK_TILES, LANES), lambda i, *_: (i, 0))
    hbm = pl.BlockSpec(memory_space=pl.ANY)
    return pl.pallas_call(
        functools.partial(_experts_kernel, layer=l),
        grid_spec=pltpu.PrefetchScalarGridSpec(
            num_scalar_prefetch=5,
            grid=(n_blocks,),
            in_specs=[rows, hbm, hbm, hbm],
            out_specs=rows,
            scratch_shapes=[
                pltpu.VMEM((2, D_MODEL, D_EXPERT), F32),
                pltpu.VMEM((2, D_MODEL, D_EXPERT), F32),
                pltpu.VMEM((2, D_EXPERT, D_MODEL), F32),
                pltpu.VMEM((D_MODEL, D_EXPERT), BF16),
                pltpu.VMEM((D_MODEL, D_EXPERT), BF16),
                pltpu.VMEM((D_EXPERT, D_MODEL), BF16),
                pltpu.SemaphoreType.DMA((2,)),
            ],
        ),
        out_shape=jax.ShapeDtypeStruct(buf.shape, U32),
        compiler_params=pltpu.CompilerParams(dimension_semantics=("arbitrary",), vmem_limit_bytes=VMEM_LIMIT),
        name="experts",
    )(blk_e, first, seg, next_e, n_used, buf, w1, w3, w2)


def _combine_final_kernel(dest_ref, x2_ref, info_ref, gf_ref, ybuf_ref, out_ref, rows_ref, sems):
    chunks = _moe_output_chunks(dest_ref, x2_ref, info_ref, ybuf_ref, rows_ref, sems)
    ssq = jnp.zeros((x2_ref.shape[0], 1), F32)
    for xc in chunks:
        ssq = ssq + jnp.sum(xc * xc, axis=-1, keepdims=True)
    inv = lax.rsqrt(ssq * (1.0 / D_MODEL) + EPS)
    for c, xc in enumerate(chunks):
        lanes = slice(c * LANES, (c + 1) * LANES)
        out_ref[:, lanes] = xc * inv * gf_ref[:, lanes]


def _combine_final(dest, x2, info, gf, ybuf):
    T = x2.shape[0]
    row = lambda i, d: (i, 0)
    return pl.pallas_call(
        _combine_final_kernel,
        grid_spec=pltpu.PrefetchScalarGridSpec(
            num_scalar_prefetch=1,
            grid=(T // TE,),
            in_specs=[
                pl.BlockSpec((TE, D_MODEL), row),
                pl.BlockSpec((TE, LANES), row),
                pl.BlockSpec((1, D_MODEL), lambda i, d: (0, 0)),
                pl.BlockSpec(memory_space=pl.ANY),
            ],
            out_specs=pl.BlockSpec((TE, D_MODEL), row),
            scratch_shapes=_moe_gather_scratch(TE),
        ),
        out_shape=jax.ShapeDtypeStruct((T, D_MODEL), F32),
        compiler_params=pltpu.CompilerParams(dimension_semantics=("arbitrary",), vmem_limit_bytes=VMEM_LIMIT),
        name="combine_final",
    )(dest, x2, info, gf, ybuf)


def kernel(x, mem, norm_mix, w_in, v_norm, w_spatial, b_spatial, out_norm_a, out_norm_b, w_out, norm_cross, norm_mem, w_xq, w_xk, w_xv, w_xo, norm_moe, w_group, b_group, w_router, b_router, w1, w3, w2, norm_final):
    B, S, D = x.shape
    L = w_in.shape[0]
    T = B * S
    assert D == D_MODEL and S % TQ == 0 and T % TM_IN == 0 and S % TM_POST == 0 and T % TD == 0 and T % TE == 0

    row3 = lambda a: a.reshape(L, 1, -1)
    w_in_b, w_out_b = w_in.astype(BF16), w_out.astype(BF16)
    w_xq_b, w_xk_b, w_xv_b, w_xo_b = (w.astype(BF16) for w in (w_xq, w_xk, w_xv, w_xo))
    pad = jnp.zeros((L, D, LANES - N_GROUPS - N_EXPERTS), F32)
    w_route = jnp.concatenate([w_group, w_router, pad], axis=-1)
    w_route_hi = w_route.astype(BF16)
    w_route_lo = (w_route - w_route_hi.astype(F32)).astype(BF16)
    w_route_b = jnp.concatenate([w_route_hi, w_route_lo], axis=-1)
    b_route = jnp.concatenate([b_group, b_router, pad[:, 0, :]], axis=-1).reshape(L, 1, LANES)
    bs_t = jnp.swapaxes(b_spatial, 1, 2)

    kx, vx = _mem_kv(mem, row3(norm_mem), w_xk_b, w_xv_b)

    n_slots = T * TOP_K + N_EXPERTS * BM
    n_blocks = n_slots // BM
    xs = x.reshape(T, D)
    moe = None
    for l in range(L):
        mix_args = (row3(norm_mix), w_in_b, row3(v_norm), w_spatial, bs_t, row3(out_norm_a), l)
        if moe is None:
            q, k, v, ya = _mix_in(xs, *mix_args)
        else:
            xs, q, k, v, ya = _moe_mix_in(*moe, *mix_args)
        yb = _stick_break(q, k, v, B, S).reshape(T, SB_WIDTH)
        x2, hm, info, cnt = _post(xs, ya, yb, row3(out_norm_b), w_out_b, row3(norm_cross), w_xq_b, kx, vx, w_xo_b,
                                  row3(norm_moe), w_route_b, b_route, l, S)
        eid = info[:, 0:TOP_K].astype(jnp.int32)
        rank = info[:, 4:4 + TOP_K].astype(jnp.int32)
        counts = cnt[0, ROUTE_BASE:ROUTE_BASE + N_EXPERTS].astype(jnp.int32)
        padded = ((counts + BM - 1) // BM) * BM
        seg_end = jnp.cumsum(padded)
        seg_start = seg_end - padded
        expert_ids = jnp.arange(N_EXPERTS, dtype=jnp.int32)
        dest = (rank + jnp.sum(jnp.where(eid[..., None] == expert_ids, seg_start, 0), axis=-1)).reshape(-1)
        blk_row = jnp.arange(n_blocks, dtype=jnp.int32) * BM
        blk_e = jnp.minimum(
            jnp.sum((seg_end[None, :] <= blk_row[:, None]).astype(jnp.int32), axis=1), N_EXPERTS - 1)
        n_used = (seg_end[-1:] // BM).astype(jnp.int32)
        nonempty = counts > 0
        first = jnp.logical_and(blk_row == seg_start[blk_e], blk_row < seg_end[-1]).astype(jnp.int32)
        seg = (jnp.cumsum(nonempty.astype(jnp.int32)) - 1)[blk_e]
        later_id = jnp.where(nonempty, expert_ids, N_EXPERTS)
        after = jnp.concatenate([lax.cummin(later_id[::-1])[::-1][1:], jnp.full((1,), N_EXPERTS, jnp.int32)])
        next_e = jnp.where(after < N_EXPERTS, after, -1)[blk_e]
        buf = _dispatch(dest, hm, jnp.zeros((n_slots * PACK_TILES, LANES), U32))
        ybuf = _experts(blk_e, first, seg, next_e, n_used, buf, w1, w3, w2, l)
        moe = (dest, x2, info, ybuf)
    dest, x2, info, ybuf = moe
    return _combine_final(dest, x2, info, norm_final.reshape(1, D), ybuf).reshape(B, S, D)
```

```python
import functools

import jax
import jax.numpy as jnp
from jax import lax
from jax.experimental import pallas as pl
from jax.experimental.pallas import tpu as pltpu

F32 = jnp.float32
BF16 = jnp.bfloat16
U32 = jnp.uint32

D_MODEL = 1024
A_WIDTH = 512
A_GROUPS = 4
A_CH = 128
CHUNK = 128
SB_WIDTH = 512
SB_HEADS = 8
SB_HEAD_DIM = 64
SB_PAIRS = SB_HEADS // 2
IN_WIDTH = 2 * A_WIDTH + 3 * SB_WIDTH
X_HEADS = 4
X_HEAD_DIM = 256
N_GROUPS = 4
EXPERTS_PER_GROUP = 8
N_EXPERTS = 32
TOP_K = 2
D_EXPERT = 512
EPS = 1e-6

LANES = 128
SUBLANES = 8
ROW_TILES = D_MODEL // LANES
PACK_TILES = ROW_TILES // 2

TM_IN = 512
TQ = 256
TK = 256
HALF = TQ // 2
TM_POST = 512
TD = 512
BM = 512
TE = 256
ROUTE_BASE = N_GROUPS
NEG = -1e30
LOG2E = 1.4426950408889634

VMEM_LIMIT = 52 * 1024 * 1024


def _rms(x, g):
    return x * lax.rsqrt(jnp.mean(x * x, axis=-1, keepdims=True) + EPS) * g


def _dot(a, b):
    return jnp.dot(a, b, preferred_element_type=F32)


def _dot_nt(a, b):
    return lax.dot_general(a, b, (((1,), (1,)), ((), ())), preferred_element_type=F32)


def _split_bf16(x):
    hi = x.astype(BF16)
    lo = (x - hi.astype(F32)).astype(BF16)
    return hi, lo


def _row_copy(src, src_row, dst, dst_row, sem):
    return pltpu.make_async_copy(
        src.at[pl.ds(pl.multiple_of(src_row * PACK_TILES, PACK_TILES), PACK_TILES), :],
        dst.at[pl.ds(pl.multiple_of(dst_row * PACK_TILES, PACK_TILES), PACK_TILES), :],
        sem)


def _store_packed(x, ref):
    half = D_MODEL // 2
    hi = pltpu.bitcast(x[:, 0:half].astype(BF16).astype(F32), U32)
    lo = pltpu.bitcast(x[:, half:D_MODEL].astype(BF16).astype(F32), U32)
    words = hi | (lo >> 16)
    for c in range(PACK_TILES):
        ref[pl.ds(c, x.shape[0], stride=PACK_TILES), :] = words[:, c * LANES:(c + 1) * LANES]


def _load_packed_chunks(ref, first, rows, stride=PACK_TILES):
    words = [ref[pl.ds(first + c, rows, stride=stride), :] for c in range(PACK_TILES)]
    return ([pltpu.bitcast(w & jnp.uint32(0xFFFF0000), F32) for w in words]
            + [pltpu.bitcast(w << 16, F32) for w in words])


def _moe_output_chunks(x2_ref, info_ref, y_ref):
    tm = x2_ref.shape[0]
    info = info_ref[...]
    gate1 = info[:, 2:3]
    gate2 = info[:, 3:4]
    y1 = _load_packed_chunks(y_ref, 0, tm, TOP_K * PACK_TILES)
    y2 = _load_packed_chunks(y_ref, PACK_TILES, tm, TOP_K * PACK_TILES)
    return [x2_ref[:, c * LANES:(c + 1) * LANES] + (gate1 * y1[c] + gate2 * y2[c]) for c in range(ROW_TILES)]


def _mix_in_body(x, g_ref, w_ref, vg_ref, ws_ref, bs_ref, ga_ref, q_ref, k_ref, v_ref, ya_ref):
    tm = x.shape[0]
    h = _rms(x, g_ref[...]).astype(BF16)
    z = _dot(h, w_ref[...])

    t_idx = lax.broadcasted_iota(jnp.int32, (CHUNK, CHUNK), 0)
    s_idx = lax.broadcasted_iota(jnp.int32, (CHUNK, CHUNK), 1)
    causal = s_idx <= t_idx
    parts = []
    ssq = jnp.zeros((tm, 1), F32)
    for g in range(A_GROUPS):
        lanes = slice(g * A_CH, (g + 1) * A_CH)
        u = jax.nn.gelu(z[:, g * A_CH:(g + 1) * A_CH])
        vg = jax.nn.gelu(z[:, A_WIDTH + g * A_CH:A_WIDTH + (g + 1) * A_CH])
        vn = _rms(vg, vg_ref[:, lanes]).astype(BF16)
        ws = jnp.where(causal, ws_ref[g], 0.0).astype(BF16)
        bias = bs_ref[:, g:g + 1]
        mixed = jnp.concatenate(
            [_dot(ws, vn[c * CHUNK:(c + 1) * CHUNK, :]) + bias for c in range(tm // CHUNK)], axis=0)
        ya = u * mixed
        parts.append(ya)
        ssq = ssq + jnp.sum(ya * ya, axis=-1, keepdims=True)
    inv = lax.rsqrt(ssq * (1.0 / A_WIDTH) + EPS)
    for g in range(A_GROUPS):
        lanes = slice(g * A_CH, (g + 1) * A_CH)
        ya_ref[:, lanes] = (parts[g] * inv * ga_ref[:, lanes]).astype(BF16)

    lane = lax.broadcasted_iota(jnp.int32, (1, LANES), 1)
    q0 = 2 * A_WIDTH
    k0 = q0 + SB_WIDTH
    v0 = k0 + SB_WIDTH
    scale = 1.0 / (SB_HEAD_DIM ** 0.5)
    for hd in range(SB_HEADS):
        pair = hd // 2
        keep = (lane < SB_HEAD_DIM) if hd % 2 == 0 else (lane >= SB_HEAD_DIM)
        qs = z[:, q0 + pair * LANES:q0 + (pair + 1) * LANES] * scale
        vs = z[:, v0 + pair * LANES:v0 + (pair + 1) * LANES]
        q_ref[:, hd * LANES:(hd + 1) * LANES] = jnp.where(keep, qs, 0.0).astype(BF16)
        v_ref[:, hd * LANES:(hd + 1) * LANES] = jnp.where(keep, vs, 0.0).astype(BF16)
    k_ref[...] = z[:, k0:v0].astype(BF16)


def _mix_in_kernel(x_ref, *refs):
    _mix_in_body(x_ref[...], *refs)


def _moe_mix_in_kernel(x2_ref, info_ref, y_ref, g_ref, w_ref, vg_ref, ws_ref, bs_ref, ga_ref,
                       x_ref, q_ref, k_ref, v_ref, ya_ref):
    x = jnp.concatenate(_moe_output_chunks(x2_ref, info_ref, y_ref), axis=-1)
    x_ref[...] = x
    _mix_in_body(x, g_ref, w_ref, vg_ref, ws_ref, bs_ref, ga_ref, q_ref, k_ref, v_ref, ya_ref)


def _mix_in_specs(l, index):
    lay = lambda *a: (l, 0, 0)
    in_specs = [
        pl.BlockSpec((None, 1, D_MODEL), lay),
        pl.BlockSpec((None, D_MODEL, IN_WIDTH), lay),
        pl.BlockSpec((None, 1, A_WIDTH), lay),
        pl.BlockSpec((None, A_GROUPS, CHUNK, CHUNK), lambda *a: (l, 0, 0, 0)),
        pl.BlockSpec((None, CHUNK, A_GROUPS), lay),
        pl.BlockSpec((None, 1, A_WIDTH), lay),
    ]
    out_specs = [
        pl.BlockSpec((TM_IN, SB_HEADS * LANES), index),
        pl.BlockSpec((TM_IN, SB_WIDTH), index),
        pl.BlockSpec((TM_IN, SB_HEADS * LANES), index),
        pl.BlockSpec((TM_IN, A_WIDTH), index),
    ]
    return in_specs, out_specs


def _mix_in_out_shapes(T):
    return [
        jax.ShapeDtypeStruct((T, SB_HEADS * LANES), BF16),
        jax.ShapeDtypeStruct((T, SB_WIDTH), BF16),
        jax.ShapeDtypeStruct((T, SB_HEADS * LANES), BF16),
        jax.ShapeDtypeStruct((T, A_WIDTH), BF16),
    ]


def _mix_in(x2d, g, w_in, vg, ws, bs_t, ga, l):
    T = x2d.shape[0]
    row = lambda i: (i, 0)
    in_specs, out_specs = _mix_in_specs(l, row)
    return pl.pallas_call(
        _mix_in_kernel,
        grid=(T // TM_IN,),
        in_specs=[pl.BlockSpec((TM_IN, D_MODEL), row)] + in_specs,
        out_specs=out_specs,
        out_shape=_mix_in_out_shapes(T),
        compiler_params=pltpu.CompilerParams(dimension_semantics=("arbitrary",), vmem_limit_bytes=VMEM_LIMIT),
        name="mix_in",
    )(x2d, g, w_in, vg, ws, bs_t, ga)


def _moe_mix_in(x2, info, y_pairs, g, w_in, vg, ws, bs_t, ga, l):
    T = x2.shape[0]
    row = lambda i: (i, 0)
    in_specs, out_specs = _mix_in_specs(l, row)
    return pl.pallas_call(
        _moe_mix_in_kernel,
        grid=(T // TM_IN,),
        in_specs=[
            pl.BlockSpec((TM_IN, D_MODEL), row),
            pl.BlockSpec((TM_IN, LANES), row),
            pl.BlockSpec((TM_IN * TOP_K * PACK_TILES, LANES), row),
        ] + in_specs,
        out_specs=[pl.BlockSpec((TM_IN, D_MODEL), row)] + out_specs,
        out_shape=[jax.ShapeDtypeStruct((T, D_MODEL), F32)] + _mix_in_out_shapes(T),
        compiler_params=pltpu.CompilerParams(dimension_semantics=("arbitrary",), vmem_limit_bytes=VMEM_LIMIT),
        name="moe_mix_in",
    )(x2, info, y_pairs, g, w_in, vg, ws, bs_t, ga)


def _stick_break_kernel(q_ref, k_ref, v_ref, o_ref, c_ref, live_ref):
    qi = pl.program_id(1)
    r_idx = lax.broadcasted_iota(jnp.int32, (TK, TK), 0)
    c_idx = lax.broadcasted_iota(jnp.int32, (TK, TK), 1)
    later = jnp.where(c_idx < r_idx, -1.0, 0.0).astype(BF16)

    def sweep(half, key0, n_keys, later_m, visible):
        rows = slice(half * HALF, (half + 1) * HALF)
        first = visible is not None
        sps, lszs = [], []
        for pr in range(SB_PAIRS):
            kb = k_ref[pl.ds(key0, n_keys), pr * LANES:(pr + 1) * LANES]
            qq = jnp.concatenate([q_ref[rows, hd * LANES:(hd + 1) * LANES] for hd in (2 * pr, 2 * pr + 1)], axis=0)
            zz = _dot_nt(qq, kb)
            for z in (zz[0:HALF, :], zz[HALF:2 * HALF, :]):
                sp = jnp.maximum(z, 0.0) + jnp.log(1.0 + jnp.exp2(jnp.abs(z) * (-LOG2E)))
                lszs.append(z - sp)
                sps.append(jnp.where(visible, sp, 0.0) if first else sp)
        after = _dot(jnp.concatenate([sp.astype(BF16) for sp in sps], axis=0), later_m)
        top = None
        for pr in range(SB_PAIRS):
            acc = None
            for hd in (2 * pr, 2 * pr + 1):
                p = jnp.exp(lszs[hd] + after[hd * HALF:(hd + 1) * HALF, :])
                if first:
                    p = jnp.where(visible, p, 0.0)
                term = _dot(p.astype(BF16), v_ref[pl.ds(key0, n_keys), hd * LANES:(hd + 1) * LANES])
                c = -jnp.sum(sps[hd], axis=-1, keepdims=True)
                if not first:
                    term = term * jnp.exp(c_ref[hd, rows, :])
                    c = c_ref[hd, rows, :] + c
                c_ref[hd, rows, :] = c
                acc = term if acc is None else acc + term
                top = jnp.exp(c) if top is None else jnp.maximum(top, jnp.exp(c))
            lanes = slice(pr * LANES, (pr + 1) * LANES)
            o_ref[rows, lanes] = acc if first else o_ref[rows, lanes] + acc
        live_ref[half] = (jnp.max(top) > 0.0).astype(jnp.int32)

    d0 = pl.multiple_of(qi * TK, TK)
    rows_a = lax.broadcasted_iota(jnp.int32, (HALF, HALF), 0)
    cols_a = lax.broadcasted_iota(jnp.int32, (HALF, HALF), 1)
    sweep(0, d0, HALF, later[0:HALF, 0:HALF], cols_a < rows_a)
    rows_b = lax.broadcasted_iota(jnp.int32, (HALF, TK), 0)
    cols_b = lax.broadcasted_iota(jnp.int32, (HALF, TK), 1)
    sweep(1, d0, TK, later, cols_b < rows_b + HALF)

    def cond(jj):
        return jnp.logical_and(jj < qi, live_ref[0] + live_ref[1] > 0)

    def body(jj):
        key0 = pl.multiple_of((qi - 1 - jj) * TK, TK)
        for half in range(2):
            @pl.when(live_ref[half] > 0)
            def _():
                sweep(half, key0, TK, later, None)
        return jj + 1

    lax.while_loop(cond, body, jnp.int32(0))


def _stick_break(q, k, v, B, S):
    return pl.pallas_call(
        _stick_break_kernel,
        grid=(B, S // TQ),
        in_specs=[
            pl.BlockSpec((None, TQ, SB_HEADS * LANES), lambda b, i: (b, i, 0)),
            pl.BlockSpec((None, S, SB_WIDTH), lambda b, i: (b, 0, 0)),
            pl.BlockSpec((None, S, SB_HEADS * LANES), lambda b, i: (b, 0, 0)),
        ],
        out_specs=pl.BlockSpec((None, TQ, SB_WIDTH), lambda b, i: (b, i, 0)),
        out_shape=jax.ShapeDtypeStruct((B, S, SB_WIDTH), F32),
        scratch_shapes=[pltpu.VMEM((SB_HEADS, TQ, 1), F32), pltpu.SMEM((2,), jnp.int32)],
        compiler_params=pltpu.CompilerParams(
            dimension_semantics=("arbitrary", "arbitrary"), vmem_limit_bytes=VMEM_LIMIT),
        name="stick_break",
    )(q.reshape(B, S, SB_HEADS * LANES), k.reshape(B, S, SB_WIDTH), v.reshape(B, S, SB_HEADS * LANES))


def _mem_kv_kernel(m_ref, g_ref, wk_ref, wv_ref, k_ref, v_ref):
    m = _rms(m_ref[...], g_ref[...]).astype(BF16)
    k_ref[...] = _dot(m, wk_ref[...]).astype(BF16)
    v_ref[...] = _dot(m, wv_ref[...]).astype(BF16)


def _mem_kv(mem, g, wk, wv):
    B, M, _ = mem.shape
    L = wk.shape[0]
    kv_spec = pl.BlockSpec((None, None, M, D_MODEL), lambda l, b: (l, b, 0, 0))
    w_spec = pl.BlockSpec((None, D_MODEL, D_MODEL), lambda l, b: (l, 0, 0))
    return pl.pallas_call(
        _mem_kv_kernel,
        grid=(L, B),
        in_specs=[
            pl.BlockSpec((None, M, D_MODEL), lambda l, b: (b, 0, 0)),
            pl.BlockSpec((None, 1, D_MODEL), lambda l, b: (l, 0, 0)),
            w_spec, w_spec,
        ],
        out_specs=[kv_spec, kv_spec],
        out_shape=[jax.ShapeDtypeStruct((L, B, M, D_MODEL), BF16)] * 2,
        compiler_params=pltpu.CompilerParams(
            dimension_semantics=("arbitrary", "arbitrary"), vmem_limit_bytes=VMEM_LIMIT),
        name="mem_kv",
    )(mem, g, wk, wv)


def _post_kernel(x_ref, ya_ref, yb_ref, gb_ref, wo_ref, gc_ref, wq_ref, kx_ref, vx_ref, wxo_ref, gm_ref,
                 wr_ref, br_ref, x2_ref, hm_ref, info_ref, cnt_ref, run_ref):
    tm = x_ref.shape[0]

    @pl.when(pl.program_id(0) == 0)
    def _():
        run_ref[...] = jnp.zeros_like(run_ref)

    ybn = _rms(yb_ref[...], gb_ref[...]).astype(BF16)
    x1 = x_ref[...] + _dot(jnp.concatenate([ya_ref[...], ybn], axis=-1), wo_ref[...])

    h = _rms(x1, gc_ref[...]).astype(BF16)
    q = (_dot(h, wq_ref[...]) * (1.0 / (X_HEAD_DIM ** 0.5))).astype(BF16)
    heads = []
    for hd in range(X_HEADS):
        cols = slice(hd * X_HEAD_DIM, (hd + 1) * X_HEAD_DIM)
        s = _dot_nt(q[:, cols], kx_ref[:, cols])
        p = jnp.exp(s - jnp.max(s, axis=-1, keepdims=True))
        p = p / jnp.sum(p, axis=-1, keepdims=True)
        heads.append(_dot(p.astype(BF16), vx_ref[:, cols]).astype(BF16))
    x2 = x1 + _dot(jnp.concatenate(heads, axis=-1), wxo_ref[...])
    x2_ref[...] = x2

    hm = _rms(x2, gm_ref[...])
    _store_packed(hm, hm_ref)

    hm_hi, hm_lo = _split_bf16(hm)
    both = _dot(hm_hi, wr_ref[...])
    logits = both[:, 0:LANES] + both[:, LANES:2 * LANES] + _dot(hm_lo, wr_ref[:, 0:LANES]) + br_ref[...]

    lane = lax.broadcasted_iota(jnp.int32, (tm, LANES), 1)
    lanef = lane.astype(F32)
    big = float(LANES)
    is_group = lane < N_GROUPS
    gl = jnp.where(is_group, logits, NEG)
    gmax = jnp.max(gl, axis=-1, keepdims=True)
    gsel = jnp.min(jnp.where(gl == gmax, lanef, big), axis=-1, keepdims=True)
    gden = jnp.sum(jnp.where(is_group, jnp.exp(gl - gmax), 0.0), axis=-1, keepdims=True)
    g_gate = 1.0 / gden
    lo = ROUTE_BASE + EXPERTS_PER_GROUP * gsel
    in_group = (lanef >= lo) & (lanef < lo + EXPERTS_PER_GROUP)
    el = jnp.where(in_group, logits, NEG)
    v1 = jnp.max(el, axis=-1, keepdims=True)
    i1 = jnp.min(jnp.where(el == v1, lanef, big), axis=-1, keepdims=True)
    el2 = jnp.where(lanef == i1, NEG, el)
    v2 = jnp.max(el2, axis=-1, keepdims=True)
    i2 = jnp.min(jnp.where(el2 == v2, lanef, big), axis=-1, keepdims=True)
    t = jnp.exp(v2 - v1)
    den = 1.0 + t
    gate1 = g_gate * (1.0 / den)
    gate2 = g_gate * (t / den)

    hit1 = lanef == i1
    hit2 = lanef == i2
    multi = jnp.where(hit1 | hit2, 1.0, 0.0)
    r_idx = lax.broadcasted_iota(jnp.int32, (tm, tm), 0)
    c_idx = lax.broadcasted_iota(jnp.int32, (tm, tm), 1)
    earlier = jnp.where(c_idx < r_idx, 1.0, 0.0).astype(BF16)
    before = _dot(earlier, multi.astype(BF16)) + run_ref[0:1, :]
    rank1 = jnp.sum(jnp.where(hit1, before, 0.0), axis=-1, keepdims=True)
    rank2 = jnp.sum(jnp.where(hit2, before, 0.0), axis=-1, keepdims=True)
    run_ref[...] = run_ref[...] + jnp.sum(multi, axis=0, keepdims=True)
    cnt_ref[...] = run_ref[...]

    info = jnp.where(lane == 0, i1 - ROUTE_BASE, 0.0)
    info = jnp.where(lane == 1, i2 - ROUTE_BASE, info)
    info = jnp.where(lane == 2, gate1, info)
    info = jnp.where(lane == 3, gate2, info)
    info = jnp.where(lane == 4, rank1, info)
    info = jnp.where(lane == 5, rank2, info)
    info_ref[...] = info


def _post(x2d, ya, yb, gb, wo, gc, wq, kx, vx, wxo, gm, wr, br, l, S):
    T = x2d.shape[0]
    M = kx.shape[2]
    grid = (T // TM_POST,)
    row = lambda i: (i, 0)
    lay = lambda i: (l, 0, 0)
    batch = lambda i: (l, (i * TM_POST) // S, 0, 0)
    wspec = pl.BlockSpec((None, D_MODEL, D_MODEL), lay)
    gspec = pl.BlockSpec((None, 1, D_MODEL), lay)
    return pl.pallas_call(
        _post_kernel,
        grid=grid,
        in_specs=[
            pl.BlockSpec((TM_POST, D_MODEL), row),
            pl.BlockSpec((TM_POST, A_WIDTH), row),
            pl.BlockSpec((TM_POST, SB_WIDTH), row),
            pl.BlockSpec((None, 1, SB_WIDTH), lay),
            wspec, gspec, wspec,
            pl.BlockSpec((None, None, M, D_MODEL), batch),
            pl.BlockSpec((None, None, M, D_MODEL), batch),
            wspec, gspec,
            pl.BlockSpec((None, D_MODEL, 2 * LANES), lay),
            pl.BlockSpec((None, 1, LANES), lay),
        ],
        out_specs=[
            pl.BlockSpec((TM_POST, D_MODEL), row),
            pl.BlockSpec((TM_POST * PACK_TILES, LANES), row),
            pl.BlockSpec((TM_POST, LANES), row),
            pl.BlockSpec((SUBLANES, LANES), lambda i: (0, 0)),
        ],
        out_shape=[
            jax.ShapeDtypeStruct((T, D_MODEL), F32),
            jax.ShapeDtypeStruct((T * PACK_TILES, LANES), U32),
            jax.ShapeDtypeStruct((T, LANES), F32),
            jax.ShapeDtypeStruct((SUBLANES, LANES), F32),
        ],
        scratch_shapes=[pltpu.VMEM((SUBLANES, LANES), F32)],
        compiler_params=pltpu.CompilerParams(dimension_semantics=("arbitrary",), vmem_limit_bytes=VMEM_LIMIT),
        name="post",
    )(x2d, ya, yb, gb, wo, gc, wq, kx, vx, wxo, gm, wr, br)


def _dispatch_kernel(dest_ref, hm_ref, buf_in_ref, buf_ref, back_ref, sem):
    del buf_in_ref
    base = pl.program_id(0) * (TD * TOP_K)

    @pl.when(pl.program_id(0) == 0)
    def _():
        spare = pl.num_programs(0) * (TD * TOP_K)

        def fill(p, c):
            back_ref[p] = spare + (p & (BM - 1))
            return c

        lax.fori_loop(0, back_ref.shape[0], fill, 0)

    def issue(r, c):
        for k in range(TOP_K):
            slot = dest_ref[base + TOP_K * r + k]
            back_ref[slot] = base + TOP_K * r + k
            _row_copy(hm_ref, r, buf_ref, slot, sem).start(priority=k)
        return c

    lax.fori_loop(0, TD, issue, 0)
    for k in range(TOP_K):
        pltpu.make_async_copy(hm_ref, buf_ref.at[pl.ds(0, TD * PACK_TILES), :], sem).wait()


def _dispatch(dest, hm, buf0):
    T = hm.shape[0] // PACK_TILES
    return pl.pallas_call(
        _dispatch_kernel,
        grid_spec=pltpu.PrefetchScalarGridSpec(
            num_scalar_prefetch=1,
            grid=(T // TD,),
            in_specs=[
                pl.BlockSpec((TD * PACK_TILES, LANES), lambda i, d: (i, 0)),
                pl.BlockSpec(memory_space=pl.ANY),
            ],
            out_specs=[pl.BlockSpec(memory_space=pl.ANY), pl.BlockSpec(memory_space=pltpu.SMEM)],
            scratch_shapes=[pltpu.SemaphoreType.DMA(())],
        ),
        out_shape=[jax.ShapeDtypeStruct(buf0.shape, U32),
                   jax.ShapeDtypeStruct((buf0.shape[0] // PACK_TILES,), jnp.int32)],
        input_output_aliases={2: 0},
        compiler_params=pltpu.CompilerParams(dimension_semantics=("arbitrary",), vmem_limit_bytes=VMEM_LIMIT),
        name="dispatch",
    )(dest, hm, buf0)


def _experts_kernel(blk_e_ref, first_ref, seg_ref, next_e_ref, n_used_ref, back_ref, buf_ref, w1_hbm, w3_hbm, w2_hbm,
                    y_hbm, w1s_ref, w3s_ref, w2s_ref, w1b_ref, w3b_ref, w2b_ref, yb_ref, sems, sem_out, *, layer):
    i = pl.program_id(0)
    n_used = n_used_ref[0]

    def wait_rows():
        pltpu.make_async_copy(yb_ref, y_hbm.at[pl.ds(0, BM * PACK_TILES), :], sem_out).wait()

    @pl.when(i == 0)
    def _():
        yb_ref[...] = jnp.zeros_like(yb_ref)
        spare = pltpu.make_async_copy(
            yb_ref, y_hbm.at[pl.ds(y_hbm.shape[0] - BM * PACK_TILES, BM * PACK_TILES), :], sem_out)
        spare.start()
        spare.wait()

    def fetch(expert, slot):
        return [pltpu.make_async_copy(w_hbm.at[layer, expert], stage.at[slot], sems.at[slot])
                for w_hbm, stage in ((w1_hbm, w1s_ref), (w3_hbm, w3s_ref), (w2_hbm, w2s_ref))]

    @pl.when(first_ref[i] > 0)
    def _():
        slot = seg_ref[i] % 2

        @pl.when(i == 0)
        def _():
            for cp in fetch(blk_e_ref[0], 0):
                cp.start()

        for cp in fetch(blk_e_ref[i], slot):
            cp.wait()
        w1b_ref[...] = w1s_ref[slot].astype(BF16)
        w3b_ref[...] = w3s_ref[slot].astype(BF16)
        w2b_ref[...] = w2s_ref[slot].astype(BF16)

        @pl.when(next_e_ref[i] >= 0)
        def _():
            for cp in fetch(next_e_ref[i], 1 - slot):
                cp.start()

    @pl.when(i < n_used)
    def _():
        xb = jnp.concatenate(_load_packed_chunks(buf_ref, 0, BM), axis=-1).astype(BF16)
        h1 = _dot(xb, w1b_ref[...])
        h3 = _dot(xb, w3b_ref[...])
        a = (h1 * jax.nn.sigmoid(h1) * h3).astype(BF16)
        y = _dot(a, w2b_ref[...])

        @pl.when(i > 0)
        def _():
            wait_rows()

        _store_packed(y, yb_ref)
        base = i * BM

        def issue(r, c):
            for k in range(2):
                _row_copy(yb_ref, 2 * r + k, y_hbm, back_ref[base + 2 * r + k], sem_out).start(priority=k)
            return c

        lax.fori_loop(0, BM // 2, issue, 0)

    @pl.when(jnp.logical_or(jnp.logical_and(i == n_used, i > 0),
                            jnp.logical_and(i == pl.num_programs(0) - 1, i < n_used)))
    def _():
        wait_rows()


def _experts(blk_e, first, seg, next_e, n_used, back, buf, w1, w3, w2, l, n_tokens):
    n_blocks = buf.shape[0] // (BM * PACK_TILES)
    rows = pl.BlockSpec((BM * PACK_TILES, LANES), lambda i, *_: (i, 0))
    hbm = pl.BlockSpec(memory_space=pl.ANY)
    return pl.pallas_call(
        functools.partial(_experts_kernel, layer=l),
        grid_spec=pltpu.PrefetchScalarGridSpec(
            num_scalar_prefetch=6,
            grid=(n_blocks,),
            in_specs=[rows, hbm, hbm, hbm],
            out_specs=hbm,
            scratch_shapes=[
                pltpu.VMEM((2, D_MODEL, D_EXPERT), F32),
                pltpu.VMEM((2, D_MODEL, D_EXPERT), F32),
                pltpu.VMEM((2, D_EXPERT, D_MODEL), F32),
                pltpu.VMEM((D_MODEL, D_EXPERT), BF16),
                pltpu.VMEM((D_MODEL, D_EXPERT), BF16),
                pltpu.VMEM((D_EXPERT, D_MODEL), BF16),
                pltpu.VMEM((BM * PACK_TILES, LANES), U32),
                pltpu.SemaphoreType.DMA((2,)),
                pltpu.SemaphoreType.DMA(()),
            ],
        ),
        out_shape=jax.ShapeDtypeStruct(((n_tokens * TOP_K + BM) * PACK_TILES, LANES), U32),
        compiler_params=pltpu.CompilerParams(dimension_semantics=("arbitrary",), vmem_limit_bytes=VMEM_LIMIT),
        name="experts",
    )(blk_e, first, seg, next_e, n_used, back, buf, w1, w3, w2)


def _combine_final_kernel(x2_ref, info_ref, gf_ref, y_ref, out_ref):
    chunks = _moe_output_chunks(x2_ref, info_ref, y_ref)
    ssq = jnp.zeros((x2_ref.shape[0], 1), F32)
    for xc in chunks:
        ssq = ssq + jnp.sum(xc * xc, axis=-1, keepdims=True)
    inv = lax.rsqrt(ssq * (1.0 / D_MODEL) + EPS)
    for c, xc in enumerate(chunks):
        lanes = slice(c * LANES, (c + 1) * LANES)
        out_ref[:, lanes] = xc * inv * gf_ref[:, lanes]


def _combine_final(x2, info, gf, y_pairs):
    T = x2.shape[0]
    row = lambda i: (i, 0)
    return pl.pallas_call(
        _combine_final_kernel,
        grid=(T // TE,),
        in_specs=[
            pl.BlockSpec((TE, D_MODEL), row),
            pl.BlockSpec((TE, LANES), row),
            pl.BlockSpec((1, D_MODEL), lambda i: (0, 0)),
            pl.BlockSpec((TE * TOP_K * PACK_TILES, LANES), row),
        ],
        out_specs=pl.BlockSpec((TE, D_MODEL), row),
        out_shape=jax.ShapeDtypeStruct((T, D_MODEL), F32),
        compiler_params=pltpu.CompilerParams(dimension_semantics=("arbitrary",), vmem_limit_bytes=VMEM_LIMIT),
        name="combine_final",
    )(x2, info, gf, y_pairs)


def kernel(x, mem, norm_mix, w_in, v_norm, w_spatial, b_spatial, out_norm_a, out_norm_b, w_out, norm_cross, norm_mem, w_xq, w_xk, w_xv, w_xo, norm_moe, w_group, b_group, w_router, b_router, w1, w3, w2, norm_final):
    B, S, D = x.shape
    L = w_in.shape[0]
    T = B * S
    assert D == D_MODEL and S % TQ == 0 and T % TM_IN == 0 and S % TM_POST == 0 and T % TD == 0 and T % TE == 0

    row3 = lambda a: a.reshape(L, 1, -1)
    w_in_b, w_out_b = w_in.astype(BF16), w_out.astype(BF16)
    w_xq_b, w_xk_b, w_xv_b, w_xo_b = (w.astype(BF16) for w in (w_xq, w_xk, w_xv, w_xo))
    pad = jnp.zeros((L, D, LANES - N_GROUPS - N_EXPERTS), F32)
    w_route = jnp.concatenate([w_group, w_router, pad], axis=-1)
    w_route_hi = w_route.astype(BF16)
    w_route_lo = (w_route - w_route_hi.astype(F32)).astype(BF16)
    w_route_b = jnp.concatenate([w_route_hi, w_route_lo], axis=-1)
    b_route = jnp.concatenate([b_group, b_router, pad[:, 0, :]], axis=-1).reshape(L, 1, LANES)
    bs_t = jnp.swapaxes(b_spatial, 1, 2)

    kx, vx = _mem_kv(mem, row3(norm_mem), w_xk_b, w_xv_b)

    n_slots = T * TOP_K + N_EXPERTS * BM
    n_blocks = n_slots // BM
    xs = x.reshape(T, D)
    moe = None
    for l in range(L):
        mix_args = (row3(norm_mix), w_in_b, row3(v_norm), w_spatial, bs_t, row3(out_norm_a), l)
        if moe is None:
            q, k, v, ya = _mix_in(xs, *mix_args)
        else:
            xs, q, k, v, ya = _moe_mix_in(*moe, *mix_args)
        yb = _stick_break(q, k, v, B, S).reshape(T, SB_WIDTH)
        x2, hm, info, cnt = _post(xs, ya, yb, row3(out_norm_b), w_out_b, row3(norm_cross), w_xq_b, kx, vx, w_xo_b,
                                  row3(norm_moe), w_route_b, b_route, l, S)
        eid = info[:, 0:TOP_K].astype(jnp.int32)
        rank = info[:, 4:4 + TOP_K].astype(jnp.int32)
        counts = cnt[0, ROUTE_BASE:ROUTE_BASE + N_EXPERTS].astype(jnp.int32)
        padded = ((counts + BM - 1) // BM) * BM
        seg_end = jnp.cumsum(padded)
        seg_start = seg_end - padded
        expert_ids = jnp.arange(N_EXPERTS, dtype=jnp.int32)
        dest = (rank + jnp.sum(jnp.where(eid[..., None] == expert_ids, seg_start, 0), axis=-1)).reshape(-1)
        blk_row = jnp.arange(n_blocks, dtype=jnp.int32) * BM
        blk_e = jnp.minimum(
            jnp.sum((seg_end[None, :] <= blk_row[:, None]).astype(jnp.int32), axis=1), N_EXPERTS - 1)
        n_used = (seg_end[-1:] // BM).astype(jnp.int32)
        nonempty = counts > 0
        first = jnp.logical_and(blk_row == seg_start[blk_e], blk_row < seg_end[-1]).astype(jnp.int32)
        seg = (jnp.cumsum(nonempty.astype(jnp.int32)) - 1)[blk_e]
        later_id = jnp.where(nonempty, expert_ids, N_EXPERTS)
        after = jnp.concatenate([lax.cummin(later_id[::-1])[::-1][1:], jnp.full((1,), N_EXPERTS, jnp.int32)])
        next_e = jnp.where(after < N_EXPERTS, after, -1)[blk_e]
        buf, back = _dispatch(dest, hm, jnp.zeros((n_slots * PACK_TILES, LANES), U32))
        y_pairs = _experts(blk_e, first, seg, next_e, n_used, back, buf, w1, w3, w2, l, T)
        moe = (x2, info, y_pairs)
    x2, info, y_pairs = moe
    return _combine_final(x2, info, norm_final.reshape(1, D), y_pairs).reshape(B, S, D)
```

```python
import functools

import jax
import jax.numpy as jnp
from jax import lax
from jax.experimental import pallas as pl
from jax.experimental.pallas import tpu as pltpu

F32 = jnp.float32
BF16 = jnp.bfloat16
U32 = jnp.uint32

D_MODEL = 1024
A_WIDTH = 512
A_GROUPS = 4
A_CH = 128
CHUNK = 128
SB_WIDTH = 512
SB_HEADS = 8
SB_HEAD_DIM = 64
SB_PAIRS = SB_HEADS // 2
IN_WIDTH = 2 * A_WIDTH + 3 * SB_WIDTH
X_HEADS = 4
X_HEAD_DIM = 256
N_GROUPS = 4
EXPERTS_PER_GROUP = 8
N_EXPERTS = 32
TOP_K = 2
D_EXPERT = 512
EPS = 1e-6

LANES = 128
SUBLANES = 8
ROW_TILES = D_MODEL // LANES
PACK_TILES = ROW_TILES // 2

TM_IN = 512
TQ = 256
TK = 256
HALF = TQ // 2
TM_POST = 512
TD = 512
BM = 512
TE = 256
ROUTE_BASE = N_GROUPS
NEG = -1e30
LOG2E = 1.4426950408889634

VMEM_LIMIT = 52 * 1024 * 1024


def _rms(x, g):
    return x * lax.rsqrt(jnp.mean(x * x, axis=-1, keepdims=True) + EPS) * g


def _dot(a, b):
    return jnp.dot(a, b, preferred_element_type=F32)


def _dot_nt(a, b):
    return lax.dot_general(a, b, (((1,), (1,)), ((), ())), preferred_element_type=F32)


def _split_bf16(x):
    hi = x.astype(BF16)
    lo = (x - hi.astype(F32)).astype(BF16)
    return hi, lo


def _row_copy(src, src_row, dst, dst_row, sem):
    return pltpu.make_async_copy(
        src.at[pl.ds(pl.multiple_of(src_row * PACK_TILES, PACK_TILES), PACK_TILES), :],
        dst.at[pl.ds(pl.multiple_of(dst_row * PACK_TILES, PACK_TILES), PACK_TILES), :],
        sem)


def _store_packed(x, ref):
    half = D_MODEL // 2
    hi = pltpu.bitcast(x[:, 0:half].astype(BF16).astype(F32), U32)
    lo = pltpu.bitcast(x[:, half:D_MODEL].astype(BF16).astype(F32), U32)
    words = hi | (lo >> 16)
    for c in range(PACK_TILES):
        ref[pl.ds(c, x.shape[0], stride=PACK_TILES), :] = words[:, c * LANES:(c + 1) * LANES]


def _load_packed_chunks(ref, row0, rows):
    words = [ref[pl.ds(row0 * PACK_TILES + c, rows, stride=PACK_TILES), :] for c in range(PACK_TILES)]
    return ([pltpu.bitcast(w & jnp.uint32(0xFFFF0000), F32) for w in words]
            + [pltpu.bitcast(w << 16, F32) for w in words])


def _moe_output_chunks(dest_ref, x2_ref, info_ref, ybuf_ref, rows_ref, sems):
    tm = x2_ref.shape[0]
    i = pl.program_id(0)

    def start_gather(step, slot):
        base = step * (tm * TOP_K)

        def issue(r, c):
            for k in range(TOP_K):
                _row_copy(ybuf_ref, dest_ref[base + TOP_K * r + k], rows_ref.at[slot], k * tm + r,
                          sems.at[slot]).start(priority=1)
            return c

        lax.fori_loop(0, tm, issue, 0)

    @pl.when(i == 0)
    def _():
        start_gather(0, 0)

    @pl.when(i + 1 < pl.num_programs(0))
    def _():
        start_gather(i + 1, (i + 1) % 2)

    slot = i % 2
    rows = rows_ref.at[slot]
    pltpu.make_async_copy(ybuf_ref.at[pl.ds(0, TOP_K * tm * PACK_TILES), :], rows, sems.at[slot]).wait()

    info = info_ref[...]
    gate1 = info[:, 2:3]
    gate2 = info[:, 3:4]
    y1 = _load_packed_chunks(rows, 0, tm)
    y2 = _load_packed_chunks(rows, tm, tm)
    return [x2_ref[:, c * LANES:(c + 1) * LANES] + (gate1 * y1[c] + gate2 * y2[c]) for c in range(ROW_TILES)]


def _moe_gather_scratch(tm):
    return [pltpu.VMEM((2, TOP_K * tm * PACK_TILES, LANES), U32), pltpu.SemaphoreType.DMA((2,))]


def _mix_in_body(x, g_ref, w_ref, vg_ref, ws_ref, bs_ref, ga_ref, q_ref, k_ref, v_ref, ya_ref):
    tm = x.shape[0]
    h = _rms(x, g_ref[...]).astype(BF16)
    z = _dot(h, w_ref[...])

    t_idx = lax.broadcasted_iota(jnp.int32, (CHUNK, CHUNK), 0)
    s_idx = lax.broadcasted_iota(jnp.int32, (CHUNK, CHUNK), 1)
    causal = s_idx <= t_idx
    parts = []
    ssq = jnp.zeros((tm, 1), F32)
    for g in range(A_GROUPS):
        lanes = slice(g * A_CH, (g + 1) * A_CH)
        u = jax.nn.gelu(z[:, g * A_CH:(g + 1) * A_CH])
        vg = jax.nn.gelu(z[:, A_WIDTH + g * A_CH:A_WIDTH + (g + 1) * A_CH])
        vn = _rms(vg, vg_ref[:, lanes]).astype(BF16)
        ws = jnp.where(causal, ws_ref[g], 0.0).astype(BF16)
        bias = bs_ref[:, g:g + 1]
        n_chunks = tm // CHUNK
        side = _dot(ws, jnp.concatenate([vn[c * CHUNK:(c + 1) * CHUNK, :] for c in range(n_chunks)], axis=1))
        mixed = jnp.concatenate([side[:, c * A_CH:(c + 1) * A_CH] + bias for c in range(n_chunks)], axis=0)
        ya = u * mixed
        parts.append(ya)
        ssq = ssq + jnp.sum(ya * ya, axis=-1, keepdims=True)
    inv = lax.rsqrt(ssq * (1.0 / A_WIDTH) + EPS)
    for g in range(A_GROUPS):
        lanes = slice(g * A_CH, (g + 1) * A_CH)
        ya_ref[:, lanes] = (parts[g] * inv * ga_ref[:, lanes]).astype(BF16)

    lane = lax.broadcasted_iota(jnp.int32, (1, LANES), 1)
    q0 = 2 * A_WIDTH
    k0 = q0 + SB_WIDTH
    v0 = k0 + SB_WIDTH
    scale = 1.0 / (SB_HEAD_DIM ** 0.5)
    for hd in range(SB_HEADS):
        pair = hd // 2
        keep = (lane < SB_HEAD_DIM) if hd % 2 == 0 else (lane >= SB_HEAD_DIM)
        qs = z[:, q0 + pair * LANES:q0 + (pair + 1) * LANES] * scale
        vs = z[:, v0 + pair * LANES:v0 + (pair + 1) * LANES]
        q_ref[:, hd * LANES:(hd + 1) * LANES] = jnp.where(keep, qs, 0.0).astype(BF16)
        v_ref[:, hd * LANES:(hd + 1) * LANES] = jnp.where(keep, vs, 0.0).astype(BF16)
    k_ref[...] = z[:, k0:v0].astype(BF16)


def _mix_in_kernel(x_ref, *refs):
    _mix_in_body(x_ref[...], *refs)


def _moe_mix_in_kernel(dest_ref, x2_ref, info_ref, ybuf_ref, g_ref, w_ref, vg_ref, ws_ref, bs_ref, ga_ref,
                       x_ref, q_ref, k_ref, v_ref, ya_ref, rows_ref, sems):
    x = jnp.concatenate(_moe_output_chunks(dest_ref, x2_ref, info_ref, ybuf_ref, rows_ref, sems), axis=-1)
    x_ref[...] = x
    _mix_in_body(x, g_ref, w_ref, vg_ref, ws_ref, bs_ref, ga_ref, q_ref, k_ref, v_ref, ya_ref)


def _mix_in_specs(l, index):
    lay = lambda *a: (l, 0, 0)
    in_specs = [
        pl.BlockSpec((None, 1, D_MODEL), lay),
        pl.BlockSpec((None, D_MODEL, IN_WIDTH), lay),
        pl.BlockSpec((None, 1, A_WIDTH), lay),
        pl.BlockSpec((None, A_GROUPS, CHUNK, CHUNK), lambda *a: (l, 0, 0, 0)),
        pl.BlockSpec((None, CHUNK, A_GROUPS), lay),
        pl.BlockSpec((None, 1, A_WIDTH), lay),
    ]
    out_specs = [
        pl.BlockSpec((TM_IN, SB_HEADS * LANES), index),
        pl.BlockSpec((TM_IN, SB_WIDTH), index),
        pl.BlockSpec((TM_IN, SB_HEADS * LANES), index),
        pl.BlockSpec((TM_IN, A_WIDTH), index),
    ]
    return in_specs, out_specs


def _mix_in_out_shapes(T):
    return [
        jax.ShapeDtypeStruct((T, SB_HEADS * LANES), BF16),
        jax.ShapeDtypeStruct((T, SB_WIDTH), BF16),
        jax.ShapeDtypeStruct((T, SB_HEADS * LANES), BF16),
        jax.ShapeDtypeStruct((T, A_WIDTH), BF16),
    ]


def _mix_in(x2d, g, w_in, vg, ws, bs_t, ga, l):
    T = x2d.shape[0]
    row = lambda i: (i, 0)
    in_specs, out_specs = _mix_in_specs(l, row)
    return pl.pallas_call(
        _mix_in_kernel,
        grid=(T // TM_IN,),
        in_specs=[pl.BlockSpec((TM_IN, D_MODEL), row)] + in_specs,
        out_specs=out_specs,
        out_shape=_mix_in_out_shapes(T),
        compiler_params=pltpu.CompilerParams(dimension_semantics=("arbitrary",), vmem_limit_bytes=VMEM_LIMIT),
        name="mix_in",
    )(x2d, g, w_in, vg, ws, bs_t, ga)


def _moe_mix_in(dest, x2, info, ybuf, g, w_in, vg, ws, bs_t, ga, l):
    T = x2.shape[0]
    row = lambda i, d: (i, 0)
    in_specs, out_specs = _mix_in_specs(l, row)
    return pl.pallas_call(
        _moe_mix_in_kernel,
        grid_spec=pltpu.PrefetchScalarGridSpec(
            num_scalar_prefetch=1,
            grid=(T // TM_IN,),
            in_specs=[
                pl.BlockSpec((TM_IN, D_MODEL), row),
                pl.BlockSpec((TM_IN, LANES), row),
                pl.BlockSpec(memory_space=pl.ANY),
            ] + in_specs,
            out_specs=[pl.BlockSpec((TM_IN, D_MODEL), row)] + out_specs,
            scratch_shapes=_moe_gather_scratch(TM_IN),
        ),
        out_shape=[jax.ShapeDtypeStruct((T, D_MODEL), F32)] + _mix_in_out_shapes(T),
        compiler_params=pltpu.CompilerParams(dimension_semantics=("arbitrary",), vmem_limit_bytes=VMEM_LIMIT),
        name="moe_mix_in",
    )(dest, x2, info, ybuf, g, w_in, vg, ws, bs_t, ga)


def _stick_break_kernel(q_ref, k_ref, v_ref, o_ref, c_ref, live_ref):
    qi = pl.program_id(1)
    r_idx = lax.broadcasted_iota(jnp.int32, (TK, TK), 0)
    c_idx = lax.broadcasted_iota(jnp.int32, (TK, TK), 1)
    later = jnp.where(c_idx < r_idx, -1.0, 0.0).astype(BF16)

    def sweep(half, key0, n_keys, later_m, visible):
        rows = slice(half * HALF, (half + 1) * HALF)
        first = visible is not None
        sps, lszs = [], []
        for pr in range(SB_PAIRS):
            kb = k_ref[pl.ds(key0, n_keys), pr * LANES:(pr + 1) * LANES]
            qq = jnp.concatenate([q_ref[rows, hd * LANES:(hd + 1) * LANES] for hd in (2 * pr, 2 * pr + 1)], axis=0)
            zz = _dot_nt(qq, kb)
            for z in (zz[0:HALF, :], zz[HALF:2 * HALF, :]):
                sp = jnp.maximum(z, 0.0) + jnp.log(1.0 + jnp.exp2(jnp.abs(z) * (-LOG2E)))
                lszs.append(z - sp)
                sps.append(jnp.where(visible, sp, 0.0) if first else sp)
        after = _dot(jnp.concatenate([sp.astype(BF16) for sp in sps], axis=0), later_m)
        top = None
        for pr in range(SB_PAIRS):
            acc = None
            for hd in (2 * pr, 2 * pr + 1):
                p = jnp.exp(lszs[hd] + after[hd * HALF:(hd + 1) * HALF, :])
                if first:
                    p = jnp.where(visible, p, 0.0)
                term = _dot(p.astype(BF16), v_ref[pl.ds(key0, n_keys), hd * LANES:(hd + 1) * LANES])
                c = -jnp.sum(sps[hd], axis=-1, keepdims=True)
                if not first:
                    c_old = c_ref[hd, rows, :]
                    term = term * jnp.exp(c_old)
                    c = c_old + c
                c_ref[hd, rows, :] = c
                acc = term if acc is None else acc + term
                top = c if top is None else jnp.maximum(top, c)
            lanes = slice(pr * LANES, (pr + 1) * LANES)
            o_ref[rows, lanes] = acc if first else o_ref[rows, lanes] + acc
        edge = jnp.exp(jnp.max(top, axis=0, keepdims=True) + 1.0)
        live_ref[half] = (jnp.max(edge) > 0.0).astype(jnp.int32)

    d0 = pl.multiple_of(qi * TK, TK)
    rows_a = lax.broadcasted_iota(jnp.int32, (HALF, HALF), 0)
    cols_a = lax.broadcasted_iota(jnp.int32, (HALF, HALF), 1)
    sweep(0, d0, HALF, later[0:HALF, 0:HALF], cols_a < rows_a)
    rows_b = lax.broadcasted_iota(jnp.int32, (HALF, TK), 0)
    cols_b = lax.broadcasted_iota(jnp.int32, (HALF, TK), 1)
    sweep(1, d0, TK, later, cols_b < rows_b + HALF)

    def cond(jj):
        return jnp.logical_and(jj < qi, live_ref[0] + live_ref[1] > 0)

    def body(jj):
        key0 = pl.multiple_of((qi - 1 - jj) * TK, TK)
        for half in range(2):
            @pl.when(live_ref[half] > 0)
            def _():
                sweep(half, key0, TK, later, None)
        return jj + 1

    lax.while_loop(cond, body, jnp.int32(0))


def _stick_break(q, k, v, B, S):
    return pl.pallas_call(
        _stick_break_kernel,
        grid=(B, S // TQ),
        in_specs=[
            pl.BlockSpec((None, TQ, SB_HEADS * LANES), lambda b, i: (b, i, 0)),
            pl.BlockSpec((None, S, SB_WIDTH), lambda b, i: (b, 0, 0)),
            pl.BlockSpec((None, S, SB_HEADS * LANES), lambda b, i: (b, 0, 0)),
        ],
        out_specs=pl.BlockSpec((None, TQ, SB_WIDTH), lambda b, i: (b, i, 0)),
        out_shape=jax.ShapeDtypeStruct((B, S, SB_WIDTH), F32),
        scratch_shapes=[pltpu.VMEM((SB_HEADS, TQ, 1), F32), pltpu.SMEM((2,), jnp.int32)],
        compiler_params=pltpu.CompilerParams(
            dimension_semantics=("arbitrary", "arbitrary"), vmem_limit_bytes=VMEM_LIMIT),
        name="stick_break",
    )(q.reshape(B, S, SB_HEADS * LANES), k.reshape(B, S, SB_WIDTH), v.reshape(B, S, SB_HEADS * LANES))


def _mem_kv_kernel(m_ref, g_ref, wk_ref, wv_ref, k_ref, v_ref):
    m = _rms(m_ref[...], g_ref[...]).astype(BF16)
    k_ref[...] = _dot(m, wk_ref[...]).astype(BF16)
    v_ref[...] = _dot(m, wv_ref[...]).astype(BF16)


def _mem_kv(mem, g, wk, wv):
    B, M, _ = mem.shape
    L = wk.shape[0]
    kv_spec = pl.BlockSpec((None, None, M, D_MODEL), lambda l, b: (l, b, 0, 0))
    w_spec = pl.BlockSpec((None, D_MODEL, D_MODEL), lambda l, b: (l, 0, 0))
    return pl.pallas_call(
        _mem_kv_kernel,
        grid=(L, B),
        in_specs=[
            pl.BlockSpec((None, M, D_MODEL), lambda l, b: (b, 0, 0)),
            pl.BlockSpec((None, 1, D_MODEL), lambda l, b: (l, 0, 0)),
            w_spec, w_spec,
        ],
        out_specs=[kv_spec, kv_spec],
        out_shape=[jax.ShapeDtypeStruct((L, B, M, D_MODEL), BF16)] * 2,
        compiler_params=pltpu.CompilerParams(
            dimension_semantics=("arbitrary", "arbitrary"), vmem_limit_bytes=VMEM_LIMIT),
        name="mem_kv",
    )(mem, g, wk, wv)


def _post_kernel(x_ref, ya_ref, yb_ref, gb_ref, wo_ref, gc_ref, wq_ref, kx_ref, vx_ref, wxo_ref, gm_ref,
                 wr_ref, br_ref, x2_ref, hm_ref, info_ref, cnt_ref, run_ref):
    tm = x_ref.shape[0]

    @pl.when(pl.program_id(0) == 0)
    def _():
        run_ref[...] = jnp.zeros_like(run_ref)

    ybn = _rms(yb_ref[...], gb_ref[...]).astype(BF16)
    x1 = x_ref[...] + _dot(jnp.concatenate([ya_ref[...], ybn], axis=-1), wo_ref[...])

    h = _rms(x1, gc_ref[...]).astype(BF16)
    q = (_dot(h, wq_ref[...]) * (1.0 / (X_HEAD_DIM ** 0.5))).astype(BF16)
    heads = []
    for hd in range(X_HEADS):
        cols = slice(hd * X_HEAD_DIM, (hd + 1) * X_HEAD_DIM)
        s = _dot_nt(q[:, cols], kx_ref[:, cols])
        p = jnp.exp(s - jnp.max(s, axis=-1, keepdims=True))
        p = p / jnp.sum(p, axis=-1, keepdims=True)
        heads.append(_dot(p.astype(BF16), vx_ref[:, cols]).astype(BF16))
    x2 = x1 + _dot(jnp.concatenate(heads, axis=-1), wxo_ref[...])
    x2_ref[...] = x2

    hm = _rms(x2, gm_ref[...])
    _store_packed(hm, hm_ref)

    hm_hi, hm_lo = _split_bf16(hm)
    both = _dot(hm_hi, wr_ref[...])
    logits = both[:, 0:LANES] + both[:, LANES:2 * LANES] + _dot(hm_lo, wr_ref[:, 0:LANES]) + br_ref[...]

    lane = lax.broadcasted_iota(jnp.int32, (tm, LANES), 1)
    lanef = lane.astype(F32)
    big = float(LANES)
    is_group = lane < N_GROUPS
    gl = jnp.where(is_group, logits, NEG)
    gmax = jnp.max(gl, axis=-1, keepdims=True)
    gsel = jnp.min(jnp.where(gl == gmax, lanef, big), axis=-1, keepdims=True)
    gden = jnp.sum(jnp.where(is_group, jnp.exp(gl - gmax), 0.0), axis=-1, keepdims=True)
    g_gate = 1.0 / gden
    lo = ROUTE_BASE + EXPERTS_PER_GROUP * gsel
    in_group = (lanef >= lo) & (lanef < lo + EXPERTS_PER_GROUP)
    el = jnp.where(in_group, logits, NEG)
    v1 = jnp.max(el, axis=-1, keepdims=True)
    i1 = jnp.min(jnp.where(el == v1, lanef, big), axis=-1, keepdims=True)
    el2 = jnp.where(lanef == i1, NEG, el)
    v2 = jnp.max(el2, axis=-1, keepdims=True)
    i2 = jnp.min(jnp.where(el2 == v2, lanef, big), axis=-1, keepdims=True)
    t = jnp.exp(v2 - v1)
    den = 1.0 + t
    gate1 = g_gate * (1.0 / den)
    gate2 = g_gate * (t / den)

    hit1 = lanef == i1
    hit2 = lanef == i2
    multi = jnp.where(hit1 | hit2, 1.0, 0.0)
    r_idx = lax.broadcasted_iota(jnp.int32, (tm, tm), 0)
    c_idx = lax.broadcasted_iota(jnp.int32, (tm, tm), 1)
    earlier = jnp.where(c_idx < r_idx, 1.0, 0.0).astype(BF16)
    before = _dot(earlier, multi.astype(BF16)) + run_ref[0:1, :]
    rank1 = jnp.sum(jnp.where(hit1, before, 0.0), axis=-1, keepdims=True)
    rank2 = jnp.sum(jnp.where(hit2, before, 0.0), axis=-1, keepdims=True)
    run_ref[...] = run_ref[...] + jnp.sum(multi, axis=0, keepdims=True)
    cnt_ref[...] = run_ref[...]

    info = jnp.where(lane == 0, i1 - ROUTE_BASE, 0.0)
    info = jnp.where(lane == 1, i2 - ROUTE_BASE, info)
    info = jnp.where(lane == 2, gate1, info)
    info = jnp.where(lane == 3, gate2, info)
    info = jnp.where(lane == 4, rank1, info)
    info = jnp.where(lane == 5, rank2, info)
    info_ref[...] = info


def _post(x2d, ya, yb, gb, wo, gc, wq, kx, vx, wxo, gm, wr, br, l, S):
    T = x2d.shape[0]
    M = kx.shape[2]
    grid = (T // TM_POST,)
    row = lambda i: (i, 0)
    lay = lambda i: (l, 0, 0)
    batch = lambda i: (l, (i * TM_POST) // S, 0, 0)
    wspec = pl.BlockSpec((None, D_MODEL, D_MODEL), lay)
    gspec = pl.BlockSpec((None, 1, D_MODEL), lay)
    return pl.pallas_call(
        _post_kernel,
        grid=grid,
        in_specs=[
            pl.BlockSpec((TM_POST, D_MODEL), row),
            pl.BlockSpec((TM_POST, A_WIDTH), row),
            pl.BlockSpec((TM_POST, SB_WIDTH), row),
            pl.BlockSpec((None, 1, SB_WIDTH), lay),
            wspec, gspec, wspec,
            pl.BlockSpec((None, None, M, D_MODEL), batch),
            pl.BlockSpec((None, None, M, D_MODEL), batch),
            wspec, gspec,
            pl.BlockSpec((None, D_MODEL, 2 * LANES), lay),
            pl.BlockSpec((None, 1, LANES), lay),
        ],
        out_specs=[
            pl.BlockSpec((TM_POST, D_MODEL), row),
            pl.BlockSpec((TM_POST * PACK_TILES, LANES), row),
            pl.BlockSpec((TM_POST, LANES), row),
            pl.BlockSpec((SUBLANES, LANES), lambda i: (0, 0)),
        ],
        out_shape=[
            jax.ShapeDtypeStruct((T, D_MODEL), F32),
            jax.ShapeDtypeStruct((T * PACK_TILES, LANES), U32),
            jax.ShapeDtypeStruct((T, LANES), F32),
            jax.ShapeDtypeStruct((SUBLANES, LANES), F32),
        ],
        scratch_shapes=[pltpu.VMEM((SUBLANES, LANES), F32)],
        compiler_params=pltpu.CompilerParams(dimension_semantics=("arbitrary",), vmem_limit_bytes=VMEM_LIMIT),
        name="post",
    )(x2d, ya, yb, gb, wo, gc, wq, kx, vx, wxo, gm, wr, br)


def _dispatch_kernel(dest_ref, hm_ref, hm_hbm, buf_in_ref, buf_ref, sem):
    del buf_in_ref
    row0 = pl.program_id(0) * TD
    base = row0 * TOP_K

    def issue(r, c):
        _row_copy(hm_ref, r, buf_ref, dest_ref[base + TOP_K * r], sem).start(priority=0)
        _row_copy(hm_hbm, row0 + r, buf_ref, dest_ref[base + TOP_K * r + 1], sem).start(priority=1)
        return c

    lax.fori_loop(0, TD, issue, 0)
    for k in range(TOP_K):
        pltpu.make_async_copy(hm_ref, buf_ref.at[pl.ds(0, TD * PACK_TILES), :], sem).wait()


def _dispatch(dest, hm, buf0):
    T = hm.shape[0] // PACK_TILES
    return pl.pallas_call(
        _dispatch_kernel,
        grid_spec=pltpu.PrefetchScalarGridSpec(
            num_scalar_prefetch=1,
            grid=(T // TD,),
            in_specs=[
                pl.BlockSpec((TD * PACK_TILES, LANES), lambda i, d: (i, 0)),
                pl.BlockSpec(memory_space=pl.ANY),
                pl.BlockSpec(memory_space=pl.ANY),
            ],
            out_specs=pl.BlockSpec(memory_space=pl.ANY),
            scratch_shapes=[pltpu.SemaphoreType.DMA(())],
        ),
        out_shape=jax.ShapeDtypeStruct(buf0.shape, U32),
        input_output_aliases={3: 0},
        compiler_params=pltpu.CompilerParams(dimension_semantics=("arbitrary",), vmem_limit_bytes=VMEM_LIMIT),
        name="dispatch",
    )(dest, hm, hm, buf0)


def _experts_kernel(blk_e_ref, first_ref, seg_ref, next_e_ref, n_used_ref, buf_ref, w1_hbm, w3_hbm, w2_hbm, y_ref,
                    w1s_ref, w3s_ref, w2s_ref, w1b_ref, w3b_ref, w2b_ref, sems, *, layer):
    i = pl.program_id(0)

    def fetch(expert, slot):
        return [pltpu.make_async_copy(w_hbm.at[layer, expert], stage.at[slot], sems.at[slot])
                for w_hbm, stage in ((w1_hbm, w1s_ref), (w3_hbm, w3s_ref), (w2_hbm, w2s_ref))]

    @pl.when(first_ref[i] > 0)
    def _():
        slot = seg_ref[i] % 2

        @pl.when(i == 0)
        def _():
            for cp in fetch(blk_e_ref[0], 0):
                cp.start()

        for cp in fetch(blk_e_ref[i], slot):
            cp.wait()
        w1b_ref[...] = w1s_ref[slot].astype(BF16)
        w3b_ref[...] = w3s_ref[slot].astype(BF16)
        w2b_ref[...] = w2s_ref[slot].astype(BF16)

        @pl.when(next_e_ref[i] >= 0)
        def _():
            for cp in fetch(next_e_ref[i], 1 - slot):
                cp.start()

    @pl.when(i < n_used_ref[0])
    def _():
        xb = jnp.concatenate(_load_packed_chunks(buf_ref, 0, BM), axis=-1).astype(BF16)
        h1 = _dot(xb, w1b_ref[...])
        h3 = _dot(xb, w3b_ref[...])
        a = (h1 * jax.nn.sigmoid(h1) * h3).astype(BF16)
        _store_packed(_dot(a, w2b_ref[...]), y_ref)

    @pl.when(i >= n_used_ref[0])
    def _():
        y_ref[...] = jnp.zeros_like(y_ref)


def _experts(blk_e, first, seg, next_e, n_used, buf, w1, w3, w2, l):
    n_blocks = buf.shape[0] // (BM * PACK_TILES)
    rows = pl.BlockSpec((BM * PACK_TILES, LANES), lambda i, *_: (i, 0))
    hbm = pl.BlockSpec(memory_space=pl.ANY)
    return pl.pallas_call(
        functools.partial(_experts_kernel, layer=l),
        grid_spec=pltpu.PrefetchScalarGridSpec(
            num_scalar_prefetch=5,
            grid=(n_blocks,),
            in_specs=[rows, hbm, hbm, hbm],
            out_specs=rows,
            scratch_shapes=[
                pltpu.VMEM((2, D_MODEL, D_EXPERT), F32),
                pltpu.VMEM((2, D_MODEL, D_EXPERT), F32),
                pltpu.VMEM((2, D_EXPERT, D_MODEL), F32),
                pltpu.VMEM((D_MODEL, D_EXPERT), BF16),
                pltpu.VMEM((D_MODEL, D_EXPERT), BF16),
                pltpu.VMEM((D_EXPERT, D_MODEL), BF16),
                pltpu.SemaphoreType.DMA((2,)),
            ],
        ),
        out_shape=jax.ShapeDtypeStruct(buf.shape, U32),
        compiler_params=pltpu.CompilerParams(dimension_semantics=("arbitrary",), vmem_limit_bytes=VMEM_LIMIT),
        name="experts",
    )(blk_e, first, seg, next_e, n_used, buf, w1, w3, w2)


def _combine_final_kernel(dest_ref, x2_ref, info_ref, gf_ref, ybuf_ref, out_ref, rows_ref, sems):
    chunks = _moe_output_chunks(dest_ref, x2_ref, info_ref, ybuf_ref, rows_ref, sems)
    ssq = jnp.zeros((x2_ref.shape[0], 1), F32)
    for xc in chunks:
        ssq = ssq + jnp.sum(xc * xc, axis=-1, keepdims=True)
    inv = lax.rsqrt(ssq * (1.0 / D_MODEL) + EPS)
    for c, xc in enumerate(chunks):
        lanes = slice(c * LANES, (c + 1) * LANES)
        out_ref[:, lanes] = xc * inv * gf_ref[:, lanes]


def _combine_final(dest, x2, info, gf, ybuf):
    T = x2.shape[0]
    row = lambda i, d: (i, 0)
    return pl.pallas_call(
        _combine_final_kernel,
        grid_spec=pltpu.PrefetchScalarGridSpec(
            num_scalar_prefetch=1,
            grid=(T // TE,),
            in_specs=[
                pl.BlockSpec((TE, D_MODEL), row),
                pl.BlockSpec((TE, LANES), row),
                pl.BlockSpec((1, D_MODEL), lambda i, d: (0, 0)),
                pl.BlockSpec(memory_space=pl.ANY),
            ],
            out_specs=pl.BlockSpec((TE, D_MODEL), row),
            scratch_shapes=_moe_gather_scratch(TE),
        ),
        out_shape=jax.ShapeDtypeStruct((T, D_MODEL), F32),
        compiler_params=pltpu.CompilerParams(dimension_semantics=("arbitrary",), vmem_limit_bytes=VMEM_LIMIT),
        name="combine_final",
    )(dest, x2, info, gf, ybuf)


def kernel(x, mem, norm_mix, w_in, v_norm, w_spatial, b_spatial, out_norm_a, out_norm_b, w_out, norm_cross, norm_mem, w_xq, w_xk, w_xv, w_xo, norm_moe, w_group, b_group, w_router, b_router, w1, w3, w2, norm_final):
    B, S, D = x.shape
    L = w_in.shape[0]
    T = B * S
    assert D == D_MODEL and S % TQ == 0 and T % TM_IN == 0 and S % TM_POST == 0 and T % TD == 0 and T % TE == 0

    row3 = lambda a: a.reshape(L, 1, -1)
    w_in_b, w_out_b = w_in.astype(BF16), w_out.astype(BF16)
    w_xq_b, w_xk_b, w_xv_b, w_xo_b = (w.astype(BF16) for w in (w_xq, w_xk, w_xv, w_xo))
    pad = jnp.zeros((L, D, LANES - N_GROUPS - N_EXPERTS), F32)
    w_route = jnp.concatenate([w_group, w_router, pad], axis=-1)
    w_route_hi = w_route.astype(BF16)
    w_route_lo = (w_route - w_route_hi.astype(F32)).astype(BF16)
    w_route_b = jnp.concatenate([w_route_hi, w_route_lo], axis=-1)
    b_route = jnp.concatenate([b_group, b_router, pad[:, 0, :]], axis=-1).reshape(L, 1, LANES)
    bs_t = jnp.swapaxes(b_spatial, 1, 2)

    kx, vx = _mem_kv(mem, row3(norm_mem), w_xk_b, w_xv_b)

    n_slots = T * TOP_K + N_EXPERTS * BM
    n_blocks = n_slots // BM
    xs = x.reshape(T, D)
    moe = None
    for l in range(L):
        mix_args = (row3(norm_mix), w_in_b, row3(v_norm), w_spatial, bs_t, row3(out_norm_a), l)
        if moe is None:
            q, k, v, ya = _mix_in(xs, *mix_args)
        else:
            xs, q, k, v, ya = _moe_mix_in(*moe, *mix_args)
        yb = _stick_break(q, k, v, B, S).reshape(T, SB_WIDTH)
        x2, hm, info, cnt = _post(xs, ya, yb, row3(out_norm_b), w_out_b, row3(norm_cross), w_xq_b, kx, vx, w_xo_b,
                                  row3(norm_moe), w_route_b, b_route, l, S)
        eid = info[:, 0:TOP_K].astype(jnp.int32)
        rank = info[:, 4:4 + TOP_K].astype(jnp.int32)
        counts = cnt[0, ROUTE_BASE:ROUTE_BASE + N_EXPERTS].astype(jnp.int32)
        padded = ((counts + BM - 1) // BM) * BM
        seg_end = jnp.cumsum(padded)
        seg_start = seg_end - padded
        expert_ids = jnp.arange(N_EXPERTS, dtype=jnp.int32)
        dest = (rank + jnp.sum(jnp.where(eid[..., None] == expert_ids, seg_start, 0), axis=-1)).reshape(-1)
        blk_row = jnp.arange(n_blocks, dtype=jnp.int32) * BM
        blk_e = jnp.minimum(
            jnp.sum((seg_end[None, :] <= blk_row[:, None]).astype(jnp.int32), axis=1), N_EXPERTS - 1)
        n_used = (seg_end[-1:] // BM).astype(jnp.int32)
        nonempty = counts > 0
        first = jnp.logical_and(blk_row == seg_start[blk_e], blk_row < seg_end[-1]).astype(jnp.int32)
        seg = (jnp.cumsum(nonempty.astype(jnp.int32)) - 1)[blk_e]
        later_id = jnp.where(nonempty, expert_ids, N_EXPERTS)
        after = jnp.concatenate([lax.cummin(later_id[::-1])[::-1][1:], jnp.full((1,), N_EXPERTS, jnp.int32)])
        next_e = jnp.where(after < N_EXPERTS, after, -1)[blk_e]
        buf = _dispatch(dest, hm, jnp.zeros((n_slots * PACK_TILES, LANES), U32))
        ybuf = _experts(blk_e, first, seg, next_e, n_used, buf, w1, w3, w2, l)
        moe = (dest, x2, info, ybuf)
    dest, x2, info, ybuf = moe
    return _combine_final(dest, x2, info, norm_final.reshape(1, D), ybuf).reshape(B, S, D)
```

```python
import functools

import jax
import jax.numpy as jnp
from jax import lax
from jax.experimental import pallas as pl
from jax.experimental.pallas import tpu as pltpu

F32 = jnp.float32
BF16 = jnp.bfloat16
U32 = jnp.uint32

D_MODEL = 1024
A_WIDTH = 512
A_GROUPS = 4
A_CH = 128
CHUNK = 128
SB_WIDTH = 512
SB_HEADS = 8
SB_HEAD_DIM = 64
SB_PAIRS = SB_HEADS // 2
IN_WIDTH = 2 * A_WIDTH + 3 * SB_WIDTH
X_HEADS = 4
X_HEAD_DIM = 256
N_GROUPS = 4
EXPERTS_PER_GROUP = 8
N_EXPERTS = 32
TOP_K = 2
D_EXPERT = 512
EPS = 1e-6

LANES = 128
SUBLANES = 8
ROW_TILES = D_MODEL // LANES
PACK_TILES = ROW_TILES // 2

TM_IN = 512
TQ = 256
TK = 256
HALF = TQ // 2
TM_POST = 512
TD = 512
BM = 512
TE = 256
ROUTE_BASE = N_GROUPS
NEG = -1e30
LOG2E = 1.4426950408889634

VMEM_LIMIT = 52 * 1024 * 1024


def _rms(x, g):
    return x * lax.rsqrt(jnp.mean(x * x, axis=-1, keepdims=True) + EPS) * g


def _dot(a, b):
    return jnp.dot(a, b, preferred_element_type=F32)


def _dot_nt(a, b):
    return lax.dot_general(a, b, (((1,), (1,)), ((), ())), preferred_element_type=F32)


def _split_bf16(x):
    hi = x.astype(BF16)
    lo = (x - hi.astype(F32)).astype(BF16)
    return hi, lo


def _row_copy(src, src_row, dst, dst_row, sem):
    return pltpu.make_async_copy(
        src.at[pl.ds(pl.multiple_of(src_row * PACK_TILES, PACK_TILES), PACK_TILES), :],
        dst.at[pl.ds(pl.multiple_of(dst_row * PACK_TILES, PACK_TILES), PACK_TILES), :],
        sem)


def _store_packed(x, ref):
    half = D_MODEL // 2
    hi = pltpu.bitcast(x[:, 0:half].astype(BF16).astype(F32), U32)
    lo = pltpu.bitcast(x[:, half:D_MODEL].astype(BF16).astype(F32), U32)
    words = hi | (lo >> 16)
    for c in range(PACK_TILES):
        ref[pl.ds(c, x.shape[0], stride=PACK_TILES), :] = words[:, c * LANES:(c + 1) * LANES]


def _load_packed_chunks(ref, row0, rows):
    words = [ref[pl.ds(row0 * PACK_TILES + c, rows, stride=PACK_TILES), :] for c in range(PACK_TILES)]
    return ([pltpu.bitcast(w & jnp.uint32(0xFFFF0000), F32) for w in words]
            + [pltpu.bitcast(w << 16, F32) for w in words])


def _moe_output_chunks(dest_ref, x2_ref, info_ref, ybuf_ref, rows_ref, sems):
    tm = x2_ref.shape[0]
    i = pl.program_id(0)

    def start_gather(step, slot):
        base = step * (tm * TOP_K)

        def issue(r, c):
            for k in range(TOP_K):
                _row_copy(ybuf_ref, dest_ref[base + TOP_K * r + k], rows_ref.at[slot], k * tm + r,
                          sems.at[slot]).start(priority=1)
            return c

        lax.fori_loop(0, tm, issue, 0)

    @pl.when(i == 0)
    def _():
        start_gather(0, 0)

    @pl.when(i + 1 < pl.num_programs(0))
    def _():
        start_gather(i + 1, (i + 1) % 2)

    slot = i % 2
    rows = rows_ref.at[slot]
    pltpu.make_async_copy(ybuf_ref.at[pl.ds(0, TOP_K * tm * PACK_TILES), :], rows, sems.at[slot]).wait()

    info = info_ref[...]
    gate1 = info[:, 2:3]
    gate2 = info[:, 3:4]
    y1 = _load_packed_chunks(rows, 0, tm)
    y2 = _load_packed_chunks(rows, tm, tm)
    return [x2_ref[:, c * LANES:(c + 1) * LANES] + (gate1 * y1[c] + gate2 * y2[c]) for c in range(ROW_TILES)]


def _moe_gather_scratch(tm):
    return [pltpu.VMEM((2, TOP_K * tm * PACK_TILES, LANES), U32), pltpu.SemaphoreType.DMA((2,))]


def _mix_in_body(x, g_ref, w_ref, vg_ref, ws_ref, bs_ref, ga_ref, q_ref, k_ref, v_ref, ya_ref):
    tm = x.shape[0]
    h = _rms(x, g_ref[...]).astype(BF16)
    z = _dot(h, w_ref[...])

    t_idx = lax.broadcasted_iota(jnp.int32, (CHUNK, CHUNK), 0)
    s_idx = lax.broadcasted_iota(jnp.int32, (CHUNK, CHUNK), 1)
    causal = s_idx <= t_idx
    parts = []
    ssq = jnp.zeros((tm, 1), F32)
    for g in range(A_GROUPS):
        lanes = slice(g * A_CH, (g + 1) * A_CH)
        u = jax.nn.gelu(z[:, g * A_CH:(g + 1) * A_CH])
        vg = jax.nn.gelu(z[:, A_WIDTH + g * A_CH:A_WIDTH + (g + 1) * A_CH])
        vn = _rms(vg, vg_ref[:, lanes]).astype(BF16)
        ws = jnp.where(causal, ws_ref[g], 0.0).astype(BF16)
        bias = bs_ref[:, g:g + 1]
        n_chunks = tm // CHUNK
        side = _dot(ws, jnp.concatenate([vn[c * CHUNK:(c + 1) * CHUNK, :] for c in range(n_chunks)], axis=1))
        mixed = jnp.concatenate([side[:, c * A_CH:(c + 1) * A_CH] + bias for c in range(n_chunks)], axis=0)
        ya = u * mixed
        parts.append(ya)
        ssq = ssq + jnp.sum(ya * ya, axis=-1, keepdims=True)
    inv = lax.rsqrt(ssq * (1.0 / A_WIDTH) + EPS)
    for g in range(A_GROUPS):
        lanes = slice(g * A_CH, (g + 1) * A_CH)
        ya_ref[:, lanes] = (parts[g] * inv * ga_ref[:, lanes]).astype(BF16)

    lane = lax.broadcasted_iota(jnp.int32, (1, LANES), 1)
    q0 = 2 * A_WIDTH
    k0 = q0 + SB_WIDTH
    v0 = k0 + SB_WIDTH
    scale = 1.0 / (SB_HEAD_DIM ** 0.5)
    for hd in range(SB_HEADS):
        pair = hd // 2
        keep = (lane < SB_HEAD_DIM) if hd % 2 == 0 else (lane >= SB_HEAD_DIM)
        qs = z[:, q0 + pair * LANES:q0 + (pair + 1) * LANES] * scale
        vs = z[:, v0 + pair * LANES:v0 + (pair + 1) * LANES]
        q_ref[:, hd * LANES:(hd + 1) * LANES] = jnp.where(keep, qs, 0.0).astype(BF16)
        v_ref[:, hd * LANES:(hd + 1) * LANES] = jnp.where(keep, vs, 0.0).astype(BF16)
    k_ref[...] = z[:, k0:v0].astype(BF16)


def _mix_in_kernel(x_ref, *refs):
    _mix_in_body(x_ref[...], *refs)


def _moe_mix_in_kernel(dest_ref, x2_ref, info_ref, ybuf_ref, g_ref, w_ref, vg_ref, ws_ref, bs_ref, ga_ref,
                       x_ref, q_ref, k_ref, v_ref, ya_ref, rows_ref, sems):
    x = jnp.concatenate(_moe_output_chunks(dest_ref, x2_ref, info_ref, ybuf_ref, rows_ref, sems), axis=-1)
    x_ref[...] = x
    _mix_in_body(x, g_ref, w_ref, vg_ref, ws_ref, bs_ref, ga_ref, q_ref, k_ref, v_ref, ya_ref)


def _mix_in_specs(l, index):
    lay = lambda *a: (l, 0, 0)
    in_specs = [
        pl.BlockSpec((None, 1, D_MODEL), lay),
        pl.BlockSpec((None, D_MODEL, IN_WIDTH), lay),
        pl.BlockSpec((None, 1, A_WIDTH), lay),
        pl.BlockSpec((None, A_GROUPS, CHUNK, CHUNK), lambda *a: (l, 0, 0, 0)),
        pl.BlockSpec((None, CHUNK, A_GROUPS), lay),
        pl.BlockSpec((None, 1, A_WIDTH), lay),
    ]
    out_specs = [
        pl.BlockSpec((TM_IN, SB_HEADS * LANES), index),
        pl.BlockSpec((TM_IN, SB_WIDTH), index),
        pl.BlockSpec((TM_IN, SB_HEADS * LANES), index),
        pl.BlockSpec((TM_IN, A_WIDTH), index),
    ]
    return in_specs, out_specs


def _mix_in_out_shapes(T):
    return [
        jax.ShapeDtypeStruct((T, SB_HEADS * LANES), BF16),
        jax.ShapeDtypeStruct((T, SB_WIDTH), BF16),
        jax.ShapeDtypeStruct((T, SB_HEADS * LANES), BF16),
        jax.ShapeDtypeStruct((T, A_WIDTH), BF16),
    ]


def _mix_in(x2d, g, w_in, vg, ws, bs_t, ga, l):
    T = x2d.shape[0]
    row = lambda i: (i, 0)
    in_specs, out_specs = _mix_in_specs(l, row)
    return pl.pallas_call(
        _mix_in_kernel,
        grid=(T // TM_IN,),
        in_specs=[pl.BlockSpec((TM_IN, D_MODEL), row)] + in_specs,
        out_specs=out_specs,
        out_shape=_mix_in_out_shapes(T),
        compiler_params=pltpu.CompilerParams(dimension_semantics=("arbitrary",), vmem_limit_bytes=VMEM_LIMIT),
        name="mix_in",
    )(x2d, g, w_in, vg, ws, bs_t, ga)


def _moe_mix_in(dest, x2, info, ybuf, g, w_in, vg, ws, bs_t, ga, l):
    T = x2.shape[0]
    row = lambda i, d: (i, 0)
    in_specs, out_specs = _mix_in_specs(l, row)
    return pl.pallas_call(
        _moe_mix_in_kernel,
        grid_spec=pltpu.PrefetchScalarGridSpec(
            num_scalar_prefetch=1,
            grid=(T // TM_IN,),
            in_specs=[
                pl.BlockSpec((TM_IN, D_MODEL), row),
                pl.BlockSpec((TM_IN, LANES), row),
                pl.BlockSpec(memory_space=pl.ANY),
            ] + in_specs,
            out_specs=[pl.BlockSpec((TM_IN, D_MODEL), row)] + out_specs,
            scratch_shapes=_moe_gather_scratch(TM_IN),
        ),
        out_shape=[jax.ShapeDtypeStruct((T, D_MODEL), F32)] + _mix_in_out_shapes(T),
        compiler_params=pltpu.CompilerParams(dimension_semantics=("arbitrary",), vmem_limit_bytes=VMEM_LIMIT),
        name="moe_mix_in",
    )(dest, x2, info, ybuf, g, w_in, vg, ws, bs_t, ga)


def _stick_break_kernel(q_ref, k_ref, v_ref, o_ref, c_ref, live_ref):
    qi = pl.program_id(1)
    r_idx = lax.broadcasted_iota(jnp.int32, (TK, TK), 0)
    c_idx = lax.broadcasted_iota(jnp.int32, (TK, TK), 1)
    later = jnp.where(c_idx < r_idx, -1.0, 0.0).astype(BF16)

    def sweep(half, key0, n_keys, later_m, visible):
        rows = slice(half * HALF, (half + 1) * HALF)
        first = visible is not None
        sps, lszs = [], []
        for pr in range(SB_PAIRS):
            kb = k_ref[pl.ds(key0, n_keys), pr * LANES:(pr + 1) * LANES]
            qq = jnp.concatenate([q_ref[rows, hd * LANES:(hd + 1) * LANES] for hd in (2 * pr, 2 * pr + 1)], axis=0)
            zz = _dot_nt(qq, kb)
            for z in (zz[0:HALF, :], zz[HALF:2 * HALF, :]):
                sp = jnp.maximum(z, 0.0) + jnp.log(1.0 + jnp.exp2(jnp.abs(z) * (-LOG2E)))
                lszs.append(z - sp)
                sps.append(jnp.where(visible, sp, 0.0) if first else sp)
        after = _dot(jnp.concatenate([sp.astype(BF16) for sp in sps], axis=0), later_m)
        top = None
        for pr in range(SB_PAIRS):
            acc = None
            for hd in (2 * pr, 2 * pr + 1):
                p = jnp.exp(lszs[hd] + after[hd * HALF:(hd + 1) * HALF, :])
                if first:
                    p = jnp.where(visible, p, 0.0)
                term = _dot(p.astype(BF16), v_ref[pl.ds(key0, n_keys), hd * LANES:(hd + 1) * LANES])
                c = -jnp.sum(sps[hd], axis=-1, keepdims=True)
                if not first:
                    c_old = c_ref[hd, rows, :]
                    term = term * jnp.exp(c_old)
                    c = c_old + c
                c_ref[hd, rows, :] = c
                acc = term if acc is None else acc + term
                top = c if top is None else jnp.maximum(top, c)
            lanes = slice(pr * LANES, (pr + 1) * LANES)
            o_ref[rows, lanes] = acc if first else o_ref[rows, lanes] + acc
        edge = jnp.exp(jnp.max(top, axis=0, keepdims=True) + 1.0)
        live_ref[half] = (jnp.max(edge) > 0.0).astype(jnp.int32)

    d0 = pl.multiple_of(qi * TK, TK)
    rows_a = lax.broadcasted_iota(jnp.int32, (HALF, HALF), 0)
    cols_a = lax.broadcasted_iota(jnp.int32, (HALF, HALF), 1)
    sweep(0, d0, HALF, later[0:HALF, 0:HALF], cols_a < rows_a)
    rows_b = lax.broadcasted_iota(jnp.int32, (HALF, TK), 0)
    cols_b = lax.broadcasted_iota(jnp.int32, (HALF, TK), 1)
    sweep(1, d0, TK, later, cols_b < rows_b + HALF)

    def cond(jj):
        return jnp.logical_and(jj < qi, live_ref[0] + live_ref[1] > 0)

    def body(jj):
        key0 = pl.multiple_of((qi - 1 - jj) * TK, TK)
        for half in range(2):
            @pl.when(live_ref[half] > 0)
            def _():
                sweep(half, key0, TK, later, None)
        return jj + 1

    lax.while_loop(cond, body, jnp.int32(0))


def _stick_break(q, k, v, B, S):
    return pl.pallas_call(
        _stick_break_kernel,
        grid=(B, S // TQ),
        in_specs=[
            pl.BlockSpec((None, TQ, SB_HEADS * LANES), lambda b, i: (b, i, 0)),
            pl.BlockSpec((None, S, SB_WIDTH), lambda b, i: (b, 0, 0)),
            pl.BlockSpec((None, S, SB_HEADS * LANES), lambda b, i: (b, 0, 0)),
        ],
        out_specs=pl.BlockSpec((None, TQ, SB_WIDTH), lambda b, i: (b, i, 0)),
        out_shape=jax.ShapeDtypeStruct((B, S, SB_WIDTH), F32),
        scratch_shapes=[pltpu.VMEM((SB_HEADS, TQ, 1), F32), pltpu.SMEM((2,), jnp.int32)],
        compiler_params=pltpu.CompilerParams(
            dimension_semantics=("arbitrary", "arbitrary"), vmem_limit_bytes=VMEM_LIMIT),
        name="stick_break",
    )(q.reshape(B, S, SB_HEADS * LANES), k.reshape(B, S, SB_WIDTH), v.reshape(B, S, SB_HEADS * LANES))


def _mem_kv_kernel(m_ref, g_ref, wk_ref, wv_ref, k_ref, v_ref):
    m = _rms(m_ref[...], g_ref[...]).astype(BF16)
    k_ref[...] = _dot(m, wk_ref[...]).astype(BF16)
    v_ref[...] = _dot(m, wv_ref[...]).astype(BF16)


def _mem_kv(mem, g, wk, wv):
    B, M, _ = mem.shape
    L = wk.shape[0]
    kv_spec = pl.BlockSpec((None, None, M, D_MODEL), lambda l, b: (l, b, 0, 0))
    w_spec = pl.BlockSpec((None, D_MODEL, D_MODEL), lambda l, b: (l, 0, 0))
    return pl.pallas_call(
        _mem_kv_kernel,
        grid=(L, B),
        in_specs=[
            pl.BlockSpec((None, M, D_MODEL), lambda l, b: (b, 0, 0)),
            pl.BlockSpec((None, 1, D_MODEL), lambda l, b: (l, 0, 0)),
            w_spec, w_spec,
        ],
        out_specs=[kv_spec, kv_spec],
        out_shape=[jax.ShapeDtypeStruct((L, B, M, D_MODEL), BF16)] * 2,
        compiler_params=pltpu.CompilerParams(
            dimension_semantics=("arbitrary", "arbitrary"), vmem_limit_bytes=VMEM_LIMIT),
        name="mem_kv",
    )(mem, g, wk, wv)


def _post_kernel(x_ref, ya_ref, yb_ref, gb_ref, wo_ref, gc_ref, wq_ref, kx_ref, vx_ref, wxo_ref, gm_ref,
                 wr_ref, br_ref, x2_ref, hm_ref, info_ref, cnt_ref, run_ref):
    tm = x_ref.shape[0]

    @pl.when(pl.program_id(0) == 0)
    def _():
        run_ref[...] = jnp.zeros_like(run_ref)

    ybn = _rms(yb_ref[...], gb_ref[...]).astype(BF16)
    x1 = x_ref[...] + _dot(jnp.concatenate([ya_ref[...], ybn], axis=-1), wo_ref[...])

    h = _rms(x1, gc_ref[...]).astype(BF16)
    q = (_dot(h, wq_ref[...]) * (1.0 / (X_HEAD_DIM ** 0.5))).astype(BF16)
    heads = []
    for hd in range(X_HEADS):
        cols = slice(hd * X_HEAD_DIM, (hd + 1) * X_HEAD_DIM)
        s = _dot_nt(q[:, cols], kx_ref[:, cols])
        p = jnp.exp(s - jnp.max(s, axis=-1, keepdims=True))
        p = p / jnp.sum(p, axis=-1, keepdims=True)
        heads.append(_dot(p.astype(BF16), vx_ref[:, cols]).astype(BF16))
    x2 = x1 + _dot(jnp.concatenate(heads, axis=-1), wxo_ref[...])
    x2_ref[...] = x2

    hm = _rms(x2, gm_ref[...])
    _store_packed(hm, hm_ref)

    hm_hi, hm_lo = _split_bf16(hm)
    both = _dot(hm_hi, wr_ref[...])
    logits = both[:, 0:LANES] + both[:, LANES:2 * LANES] + _dot(hm_lo, wr_ref[:, 0:LANES]) + br_ref[...]

    lane = lax.broadcasted_iota(jnp.int32, (tm, LANES), 1)
    lanef = lane.astype(F32)
    big = float(LANES)
    is_group = lane < N_GROUPS
    gl = jnp.where(is_group, logits, NEG)
    gmax = jnp.max(gl, axis=-1, keepdims=True)
    gsel = jnp.min(jnp.where(gl == gmax, lanef, big), axis=-1, keepdims=True)
    gden = jnp.sum(jnp.where(is_group, jnp.exp(gl - gmax), 0.0), axis=-1, keepdims=True)
    g_gate = 1.0 / gden
    lo = ROUTE_BASE + EXPERTS_PER_GROUP * gsel
    in_group = (lanef >= lo) & (lanef < lo + EXPERTS_PER_GROUP)
    el = jnp.where(in_group, logits, NEG)
    v1 = jnp.max(el, axis=-1, keepdims=True)
    i1 = jnp.min(jnp.where(el == v1, lanef, big), axis=-1, keepdims=True)
    el2 = jnp.where(lanef == i1, NEG, el)
    v2 = jnp.max(el2, axis=-1, keepdims=True)
    i2 = jnp.min(jnp.where(el2 == v2, lanef, big), axis=-1, keepdims=True)
    t = jnp.exp(v2 - v1)
    den = 1.0 + t
    gate1 = g_gate * (1.0 / den)
    gate2 = g_gate * (t / den)

    hit1 = lanef == i1
    hit2 = lanef == i2
    multi = jnp.where(hit1 | hit2, 1.0, 0.0)
    r_idx = lax.broadcasted_iota(jnp.int32, (tm, tm), 0)
    c_idx = lax.broadcasted_iota(jnp.int32, (tm, tm), 1)
    earlier = jnp.where(c_idx < r_idx, 1.0, 0.0).astype(BF16)
    before = _dot(earlier, multi.astype(BF16)) + run_ref[0:1, :]
    rank1 = jnp.sum(jnp.where(hit1, before, 0.0), axis=-1, keepdims=True)
    rank2 = jnp.sum(jnp.where(hit2, before, 0.0), axis=-1, keepdims=True)
    run_ref[...] = run_ref[...] + jnp.sum(multi, axis=0, keepdims=True)
    cnt_ref[...] = run_ref[...]

    info = jnp.where(lane == 0, i1 - ROUTE_BASE, 0.0)
    info = jnp.where(lane == 1, i2 - ROUTE_BASE, info)
    info = jnp.where(lane == 2, gate1, info)
    info = jnp.where(lane == 3, gate2, info)
    info = jnp.where(lane == 4, rank1, info)
    info = jnp.where(lane == 5, rank2, info)
    info_ref[...] = info


def _post(x2d, ya, yb, gb, wo, gc, wq, kx, vx, wxo, gm, wr, br, l, S):
    T = x2d.shape[0]
    M = kx.shape[2]
    grid = (T // TM_POST,)
    row = lambda i: (i, 0)
    lay = lambda i: (l, 0, 0)
    batch = lambda i: (l, (i * TM_POST) // S, 0, 0)
    wspec = pl.BlockSpec((None, D_MODEL, D_MODEL), lay)
    gspec = pl.BlockSpec((None, 1, D_MODEL), lay)
    return pl.pallas_call(
        _post_kernel,
        grid=grid,
        in_specs=[
            pl.BlockSpec((TM_POST, D_MODEL), row),
            pl.BlockSpec((TM_POST, A_WIDTH), row),
            pl.BlockSpec((TM_POST, SB_WIDTH), row),
            pl.BlockSpec((None, 1, SB_WIDTH), lay),
            wspec, gspec, wspec,
            pl.BlockSpec((None, None, M, D_MODEL), batch),
            pl.BlockSpec((None, None, M, D_MODEL), batch),
            wspec, gspec,
            pl.BlockSpec((None, D_MODEL, 2 * LANES), lay),
            pl.BlockSpec((None, 1, LANES), lay),
        ],
        out_specs=[
            pl.BlockSpec((TM_POST, D_MODEL), row),
            pl.BlockSpec((TM_POST * PACK_TILES, LANES), row),
            pl.BlockSpec((TM_POST, LANES), row),
            pl.BlockSpec((SUBLANES, LANES), lambda i: (0, 0)),
        ],
        out_shape=[
            jax.ShapeDtypeStruct((T, D_MODEL), F32),
            jax.ShapeDtypeStruct((T * PACK_TILES, LANES), U32),
            jax.ShapeDtypeStruct((T, LANES), F32),
            jax.ShapeDtypeStruct((SUBLANES, LANES), F32),
        ],
        scratch_shapes=[pltpu.VMEM((SUBLANES, LANES), F32)],
        compiler_params=pltpu.CompilerParams(dimension_semantics=("arbitrary",), vmem_limit_bytes=VMEM_LIMIT),
        name="post",
    )(x2d, ya, yb, gb, wo, gc, wq, kx, vx, wxo, gm, wr, br)


def _dispatch_kernel(dest_ref, tail_ref, n_pad_ref, n_used_ref, hm_ref, buf_ref, zeros_ref, sem, sem_fill):
    base = pl.program_id(0) * (TD * TOP_K)
    n_blocks = buf_ref.shape[0] // (BM * PACK_TILES)

    def fill(row, n_rows):
        return pltpu.make_async_copy(
            zeros_ref.at[pl.ds(0, n_rows * PACK_TILES), :],
            buf_ref.at[pl.ds(pl.multiple_of(row * PACK_TILES, PACK_TILES), n_rows * PACK_TILES), :], sem_fill)

    def zero_fills(act):
        for e in range(N_EXPERTS):
            row = tail_ref[e]
            for bit in reversed(range(BM.bit_length() - 1)):
                take = (n_pad_ref[e] >> bit) & 1

                @pl.when(take == 1)
                def _(row=row, bit=bit):
                    act(fill(row, 1 << bit))

                row = row + take * (1 << bit)

        def empty_block(j, c):
            act(fill(j * BM, BM))
            return c

        lax.fori_loop(n_used_ref[0], n_blocks, empty_block, 0)

    @pl.when(pl.program_id(0) == 0)
    def _():
        zeros_ref[...] = jnp.zeros_like(zeros_ref)
        zero_fills(lambda cp: cp.start())

    def issue(r, c):
        for k in range(TOP_K):
            _row_copy(hm_ref, r, buf_ref, dest_ref[base + TOP_K * r + k], sem).start(priority=k)
        return c

    lax.fori_loop(0, TD, issue, 0)

    @pl.when(pl.program_id(0) == 0)
    def _():
        zero_fills(lambda cp: cp.wait())

    for k in range(TOP_K):
        pltpu.make_async_copy(hm_ref, buf_ref.at[pl.ds(0, TD * PACK_TILES), :], sem).wait()


def _dispatch(dest, tail, n_pad, n_used, hm, n_slots):
    T = hm.shape[0] // PACK_TILES
    return pl.pallas_call(
        _dispatch_kernel,
        grid_spec=pltpu.PrefetchScalarGridSpec(
            num_scalar_prefetch=4,
            grid=(T // TD,),
            in_specs=[pl.BlockSpec((TD * PACK_TILES, LANES), lambda i, *_: (i, 0))],
            out_specs=pl.BlockSpec(memory_space=pl.ANY),
            scratch_shapes=[
                pltpu.VMEM((BM * PACK_TILES, LANES), U32),
                pltpu.SemaphoreType.DMA(()),
                pltpu.SemaphoreType.DMA(()),
            ],
        ),
        out_shape=jax.ShapeDtypeStruct((n_slots * PACK_TILES, LANES), U32),
        compiler_params=pltpu.CompilerParams(dimension_semantics=("arbitrary",), vmem_limit_bytes=VMEM_LIMIT),
        name="dispatch",
    )(dest, tail, n_pad, n_used, hm)


def _experts_kernel(blk_e_ref, first_ref, seg_ref, next_e_ref, n_used_ref, buf_ref, w1_hbm, w3_hbm, w2_hbm, y_ref,
                    w1s_ref, w3s_ref, w2s_ref, w1b_ref, w3b_ref, w2b_ref, sems, *, layer):
    i = pl.program_id(0)

    def fetch(expert, slot):
        return [pltpu.make_async_copy(w_hbm.at[layer, expert], stage.at[slot], sems.at[slot])
                for w_hbm, stage in ((w1_hbm, w1s_ref), (w3_hbm, w3s_ref), (w2_hbm, w2s_ref))]

    @pl.when(first_ref[i] > 0)
    def _():
        slot = seg_ref[i] % 2

        @pl.when(i == 0)
        def _():
            for cp in fetch(blk_e_ref[0], 0):
                cp.start()

        for cp in fetch(blk_e_ref[i], slot):
            cp.wait()
        w1b_ref[...] = w1s_ref[slot].astype(BF16)
        w3b_ref[...] = w3s_ref[slot].astype(BF16)
        w2b_ref[...] = w2s_ref[slot].astype(BF16)

        @pl.when(next_e_ref[i] >= 0)
        def _():
            for cp in fetch(next_e_ref[i], 1 - slot):
                cp.start()

    @pl.when(i < n_used_ref[0])
    def _():
        xb = jnp.concatenate(_load_packed_chunks(buf_ref, 0, BM), axis=-1).astype(BF16)
        h1 = _dot(xb, w1b_ref[...])
        h3 = _dot(xb, w3b_ref[...])
        a = (h1 * jax.nn.sigmoid(h1) * h3).astype(BF16)
        _store_packed(_dot(a, w2b_ref[...]), y_ref)

    @pl.when(i >= n_used_ref[0])
    def _():
        y_ref[...] = jnp.zeros_like(y_ref)


def _experts(blk_e, first, seg, next_e, n_used, buf, w1, w3, w2, l):
    n_blocks = buf.shape[0] // (BM * PACK_TILES)
    rows = pl.BlockSpec((BM * PACK_TILES, LANES), lambda i, *_: (i, 0))
    hbm = pl.BlockSpec(memory_space=pl.ANY)
    return pl.pallas_call(
        functools.partial(_experts_kernel, layer=l),
        grid_spec=pltpu.PrefetchScalarGridSpec(
            num_scalar_prefetch=5,
            grid=(n_blocks,),
            in_specs=[rows, hbm, hbm, hbm],
            out_specs=rows,
            scratch_shapes=[
                pltpu.VMEM((2, D_MODEL, D_EXPERT), F32),
                pltpu.VMEM((2, D_MODEL, D_EXPERT), F32),
                pltpu.VMEM((2, D_EXPERT, D_MODEL), F32),
                pltpu.VMEM((D_MODEL, D_EXPERT), BF16),
                pltpu.VMEM((D_MODEL, D_EXPERT), BF16),
                pltpu.VMEM((D_EXPERT, D_MODEL), BF16),
                pltpu.SemaphoreType.DMA((2,)),
            ],
        ),
        out_shape=jax.ShapeDtypeStruct(buf.shape, U32),
        compiler_params=pltpu.CompilerParams(dimension_semantics=("arbitrary",), vmem_limit_bytes=VMEM_LIMIT),
        name="experts",
    )(blk_e, first, seg, next_e, n_used, buf, w1, w3, w2)


def _combine_final_kernel(dest_ref, x2_ref, info_ref, gf_ref, ybuf_ref, out_ref, rows_ref, sems):
    chunks = _moe_output_chunks(dest_ref, x2_ref, info_ref, ybuf_ref, rows_ref, sems)
    ssq = jnp.zeros((x2_ref.shape[0], 1), F32)
    for xc in chunks:
        ssq = ssq + jnp.sum(xc * xc, axis=-1, keepdims=True)
    inv = lax.rsqrt(ssq * (1.0 / D_MODEL) + EPS)
    for c, xc in enumerate(chunks):
        lanes = slice(c * LANES, (c + 1) * LANES)
        out_ref[:, lanes] = xc * inv * gf_ref[:, lanes]


def _combine_final(dest, x2, info, gf, ybuf):
    T = x2.shape[0]
    row = lambda i, d: (i, 0)
    return pl.pallas_call(
        _combine_final_kernel,
        grid_spec=pltpu.PrefetchScalarGridSpec(
            num_scalar_prefetch=1,
            grid=(T // TE,),
            in_specs=[
                pl.BlockSpec((TE, D_MODEL), row),
                pl.BlockSpec((TE, LANES), row),
                pl.BlockSpec((1, D_MODEL), lambda i, d: (0, 0)),
                pl.BlockSpec(memory_space=pl.ANY),
            ],
            out_specs=pl.BlockSpec((TE, D_MODEL), row),
            scratch_shapes=_moe_gather_scratch(TE),
        ),
        out_shape=jax.ShapeDtypeStruct((T, D_MODEL), F32),
        compiler_params=pltpu.CompilerParams(dimension_semantics=("arbitrary",), vmem_limit_bytes=VMEM_LIMIT),
        name="combine_final",
    )(dest, x2, info, gf, ybuf)


def kernel(x, mem, norm_mix, w_in, v_norm, w_spatial, b_spatial, out_norm_a, out_norm_b, w_out, norm_cross, norm_mem, w_xq, w_xk, w_xv, w_xo, norm_moe, w_group, b_group, w_router, b_router, w1, w3, w2, norm_final):
    B, S, D = x.shape
    L = w_in.shape[0]
    T = B * S
    assert D == D_MODEL and S % TQ == 0 and T % TM_IN == 0 and S % TM_POST == 0 and T % TD == 0 and T % TE == 0

    row3 = lambda a: a.reshape(L, 1, -1)
    w_in_b, w_out_b = w_in.astype(BF16), w_out.astype(BF16)
    w_xq_b, w_xk_b, w_xv_b, w_xo_b = (w.astype(BF16) for w in (w_xq, w_xk, w_xv, w_xo))
    pad = jnp.zeros((L, D, LANES - N_GROUPS - N_EXPERTS), F32)
    w_route = jnp.concatenate([w_group, w_router, pad], axis=-1)
    w_route_hi = w_route.astype(BF16)
    w_route_lo = (w_route - w_route_hi.astype(F32)).astype(BF16)
    w_route_b = jnp.concatenate([w_route_hi, w_route_lo], axis=-1)
    b_route = jnp.concatenate([b_group, b_router, pad[:, 0, :]], axis=-1).reshape(L, 1, LANES)
    bs_t = jnp.swapaxes(b_spatial, 1, 2)

    kx, vx = _mem_kv(mem, row3(norm_mem), w_xk_b, w_xv_b)

    n_slots = T * TOP_K + N_EXPERTS * BM
    n_blocks = n_slots // BM
    xs = x.reshape(T, D)
    moe = None
    for l in range(L):
        mix_args = (row3(norm_mix), w_in_b, row3(v_norm), w_spatial, bs_t, row3(out_norm_a), l)
        if moe is None:
            q, k, v, ya = _mix_in(xs, *mix_args)
        else:
            xs, q, k, v, ya = _moe_mix_in(*moe, *mix_args)
        yb = _stick_break(q, k, v, B, S).reshape(T, SB_WIDTH)
        x2, hm, info, cnt = _post(xs, ya, yb, row3(out_norm_b), w_out_b, row3(norm_cross), w_xq_b, kx, vx, w_xo_b,
                                  row3(norm_moe), w_route_b, b_route, l, S)
        eid = info[:, 0:TOP_K].astype(jnp.int32)
        rank = info[:, 4:4 + TOP_K].astype(jnp.int32)
        counts = cnt[0, ROUTE_BASE:ROUTE_BASE + N_EXPERTS].astype(jnp.int32)
        padded = ((counts + BM - 1) // BM) * BM
        seg_end = jnp.cumsum(padded)
        seg_start = seg_end - padded
        expert_ids = jnp.arange(N_EXPERTS, dtype=jnp.int32)
        dest = (rank + jnp.sum(jnp.where(eid[..., None] == expert_ids, seg_start, 0), axis=-1)).reshape(-1)
        blk_row = jnp.arange(n_blocks, dtype=jnp.int32) * BM
        blk_e = jnp.minimum(
            jnp.sum((seg_end[None, :] <= blk_row[:, None]).astype(jnp.int32), axis=1), N_EXPERTS - 1)
        n_used = (seg_end[-1:] // BM).astype(jnp.int32)
        nonempty = counts > 0
        first = jnp.logical_and(blk_row == seg_start[blk_e], blk_row < seg_end[-1]).astype(jnp.int32)
        seg = (jnp.cumsum(nonempty.astype(jnp.int32)) - 1)[blk_e]
        later_id = jnp.where(nonempty, expert_ids, N_EXPERTS)
        after = jnp.concatenate([lax.cummin(later_id[::-1])[::-1][1:], jnp.full((1,), N_EXPERTS, jnp.int32)])
        next_e = jnp.where(after < N_EXPERTS, after, -1)[blk_e]
        buf = _dispatch(dest, seg_start + counts, padded - counts, n_used, hm, n_slots)
        ybuf = _experts(blk_e, first, seg, next_e, n_used, buf, w1, w3, w2, l)
        moe = (dest, x2, info, ybuf)
    dest, x2, info, ybuf = moe
    return _combine_final(dest, x2, info, norm_final.reshape(1, D), ybuf).reshape(B, S, D)
```

```python
import functools

import jax
import jax.numpy as jnp
from jax import lax
from jax.experimental import pallas as pl
from jax.experimental.pallas import tpu as pltpu

F32 = jnp.float32
BF16 = jnp.bfloat16
U32 = jnp.uint32

D_MODEL = 1024
A_WIDTH = 512
A_GROUPS = 4
A_CH = 128
CHUNK = 128
SB_WIDTH = 512
SB_HEADS = 8
SB_HEAD_DIM = 64
SB_PAIRS = SB_HEADS // 2
IN_WIDTH = 2 * A_WIDTH + 3 * SB_WIDTH
X_HEADS = 4
X_HEAD_DIM = 256
N_GROUPS = 4
EXPERTS_PER_GROUP = 8
N_EXPERTS = 32
TOP_K = 2
D_EXPERT = 512
EPS = 1e-6

LANES = 128
SUBLANES = 8
ROW_TILES = D_MODEL // LANES
PACK_TILES = ROW_TILES // 2

TM_IN = 512
TQ = 256
TK = 256
HALF = TQ // 2
TM_POST = 512
TD = 512
BM = 512
TE = 512
ROUTE_BASE = N_GROUPS
NEG = -1e30
LOG2E = 1.4426950408889634

VMEM_LIMIT = 52 * 1024 * 1024


def _rms(x, g):
    return x * lax.rsqrt(jnp.mean(x * x, axis=-1, keepdims=True) + EPS) * g


def _dot(a, b):
    return jnp.dot(a, b, preferred_element_type=F32)


def _dot_nt(a, b):
    return lax.dot_general(a, b, (((1,), (1,)), ((), ())), preferred_element_type=F32)


def _split_bf16(x):
    hi = x.astype(BF16)
    lo = (x - hi.astype(F32)).astype(BF16)
    return hi, lo


def _row_copy(src, src_row, dst, dst_row, sem):
    return pltpu.make_async_copy(
        src.at[pl.ds(pl.multiple_of(src_row * PACK_TILES, PACK_TILES), PACK_TILES), :],
        dst.at[pl.ds(pl.multiple_of(dst_row * PACK_TILES, PACK_TILES), PACK_TILES), :],
        sem)


def _store_packed(x, ref):
    half = D_MODEL // 2
    hi = pltpu.bitcast(x[:, 0:half].astype(BF16).astype(F32), U32)
    lo = pltpu.bitcast(x[:, half:D_MODEL].astype(BF16).astype(F32), U32)
    words = hi | (lo >> 16)
    for c in range(PACK_TILES):
        ref[pl.ds(c, x.shape[0], stride=PACK_TILES), :] = words[:, c * LANES:(c + 1) * LANES]


def _load_packed_chunks(ref, row0, rows):
    words = [ref[pl.ds(row0 * PACK_TILES + c, rows, stride=PACK_TILES), :] for c in range(PACK_TILES)]
    return ([pltpu.bitcast(w & jnp.uint32(0xFFFF0000), F32) for w in words]
            + [pltpu.bitcast(w << 16, F32) for w in words])


def _moe_output_chunks(dest_ref, x2_ref, info_ref, ybuf_ref, rows_ref, sems):
    tm = x2_ref.shape[0]
    i = pl.program_id(0)

    def start_gather(step, slot):
        base = step * (tm * TOP_K)

        def issue(r, c):
            for k in range(TOP_K):
                _row_copy(ybuf_ref, dest_ref[base + TOP_K * r + k], rows_ref.at[slot], k * tm + r,
                          sems.at[slot]).start(priority=k)
            return c

        lax.fori_loop(0, tm, issue, 0)

    @pl.when(i == 0)
    def _():
        start_gather(0, 0)

    @pl.when(i + 1 < pl.num_programs(0))
    def _():
        start_gather(i + 1, (i + 1) % 2)

    slot = i % 2
    rows = rows_ref.at[slot]
    pltpu.make_async_copy(ybuf_ref.at[pl.ds(0, TOP_K * tm * PACK_TILES), :], rows, sems.at[slot]).wait()

    info = info_ref[...]
    gate1 = info[:, 2:3]
    gate2 = info[:, 3:4]
    y1 = _load_packed_chunks(rows, 0, tm)
    y2 = _load_packed_chunks(rows, tm, tm)
    return [x2_ref[:, c * LANES:(c + 1) * LANES] + (gate1 * y1[c] + gate2 * y2[c]) for c in range(ROW_TILES)]


def _moe_gather_scratch(tm):
    return [pltpu.VMEM((2, TOP_K * tm * PACK_TILES, LANES), U32), pltpu.SemaphoreType.DMA((2,))]


def _mix_in_body(x, g_ref, w_ref, vg_ref, ws_ref, bs_ref, ga_ref, q_ref, k_ref, v_ref, ya_ref):
    tm = x.shape[0]
    h = _rms(x, g_ref[...]).astype(BF16)
    z = _dot(h, w_ref[...])

    t_idx = lax.broadcasted_iota(jnp.int32, (CHUNK, CHUNK), 0)
    s_idx = lax.broadcasted_iota(jnp.int32, (CHUNK, CHUNK), 1)
    causal = s_idx <= t_idx
    parts = []
    ssq = jnp.zeros((tm, 1), F32)
    for g in range(A_GROUPS):
        lanes = slice(g * A_CH, (g + 1) * A_CH)
        u = jax.nn.gelu(z[:, g * A_CH:(g + 1) * A_CH])
        vg = jax.nn.gelu(z[:, A_WIDTH + g * A_CH:A_WIDTH + (g + 1) * A_CH])
        vn = _rms(vg, vg_ref[:, lanes]).astype(BF16)
        ws = jnp.where(causal, ws_ref[g], 0.0).astype(BF16)
        bias = bs_ref[:, g:g + 1]
        mixed = jnp.concatenate(
            [_dot(ws, vn[c * CHUNK:(c + 1) * CHUNK, :]) + bias for c in range(tm // CHUNK)], axis=0)
        ya = u * mixed
        parts.append(ya)
        ssq = ssq + jnp.sum(ya * ya, axis=-1, keepdims=True)
    inv = lax.rsqrt(ssq * (1.0 / A_WIDTH) + EPS)
    for g in range(A_GROUPS):
        lanes = slice(g * A_CH, (g + 1) * A_CH)
        ya_ref[:, lanes] = (parts[g] * inv * ga_ref[:, lanes]).astype(BF16)

    lane = lax.broadcasted_iota(jnp.int32, (1, LANES), 1)
    q0 = 2 * A_WIDTH
    k0 = q0 + SB_WIDTH
    v0 = k0 + SB_WIDTH
    scale = 1.0 / (SB_HEAD_DIM ** 0.5)
    for hd in range(SB_HEADS):
        pair = hd // 2
        keep = (lane < SB_HEAD_DIM) if hd % 2 == 0 else (lane >= SB_HEAD_DIM)
        qs = z[:, q0 + pair * LANES:q0 + (pair + 1) * LANES] * scale
        vs = z[:, v0 + pair * LANES:v0 + (pair + 1) * LANES]
        q_ref[:, hd * LANES:(hd + 1) * LANES] = jnp.where(keep, qs, 0.0).astype(BF16)
        v_ref[:, hd * LANES:(hd + 1) * LANES] = jnp.where(keep, vs, 0.0).astype(BF16)
    k_ref[...] = z[:, k0:v0].astype(BF16)


def _mix_in_kernel(x_ref, *refs):
    _mix_in_body(x_ref[...], *refs)


def _moe_mix_in_kernel(dest_ref, x2_ref, info_ref, ybuf_ref, g_ref, w_ref, vg_ref, ws_ref, bs_ref, ga_ref,
                       x_ref, q_ref, k_ref, v_ref, ya_ref, rows_ref, sems):
    x = jnp.concatenate(_moe_output_chunks(dest_ref, x2_ref, info_ref, ybuf_ref, rows_ref, sems), axis=-1)
    x_ref[...] = x
    _mix_in_body(x, g_ref, w_ref, vg_ref, ws_ref, bs_ref, ga_ref, q_ref, k_ref, v_ref, ya_ref)


def _mix_in_specs(l, index):
    lay = lambda *a: (l, 0, 0)
    in_specs = [
        pl.BlockSpec((None, 1, D_MODEL), lay),
        pl.BlockSpec((None, D_MODEL, IN_WIDTH), lay),
        pl.BlockSpec((None, 1, A_WIDTH), lay),
        pl.BlockSpec((None, A_GROUPS, CHUNK, CHUNK), lambda *a: (l, 0, 0, 0)),
        pl.BlockSpec((None, CHUNK, A_GROUPS), lay),
        pl.BlockSpec((None, 1, A_WIDTH), lay),
    ]
    out_specs = [
        pl.BlockSpec((TM_IN, SB_HEADS * LANES), index),
        pl.BlockSpec((TM_IN, SB_WIDTH), index),
        pl.BlockSpec((TM_IN, SB_HEADS * LANES), index),
        pl.BlockSpec((TM_IN, A_WIDTH), index),
    ]
    return in_specs, out_specs


def _mix_in_out_shapes(T):
    return [
        jax.ShapeDtypeStruct((T, SB_HEADS * LANES), BF16),
        jax.ShapeDtypeStruct((T, SB_WIDTH), BF16),
        jax.ShapeDtypeStruct((T, SB_HEADS * LANES), BF16),
        jax.ShapeDtypeStruct((T, A_WIDTH), BF16),
    ]


def _mix_in(x2d, g, w_in, vg, ws, bs_t, ga, l):
    T = x2d.shape[0]
    row = lambda i: (i, 0)
    in_specs, out_specs = _mix_in_specs(l, row)
    return pl.pallas_call(
        _mix_in_kernel,
        grid=(T // TM_IN,),
        in_specs=[pl.BlockSpec((TM_IN, D_MODEL), row)] + in_specs,
        out_specs=out_specs,
        out_shape=_mix_in_out_shapes(T),
        compiler_params=pltpu.CompilerParams(dimension_semantics=("arbitrary",), vmem_limit_bytes=VMEM_LIMIT),
        name="mix_in",
    )(x2d, g, w_in, vg, ws, bs_t, ga)


def _moe_mix_in(dest, x2, info, ybuf, g, w_in, vg, ws, bs_t, ga, l):
    T = x2.shape[0]
    row = lambda i, d: (i, 0)
    in_specs, out_specs = _mix_in_specs(l, row)
    return pl.pallas_call(
        _moe_mix_in_kernel,
        grid_spec=pltpu.PrefetchScalarGridSpec(
            num_scalar_prefetch=1,
            grid=(T // TM_IN,),
            in_specs=[
                pl.BlockSpec((TM_IN, D_MODEL), row),
                pl.BlockSpec((TM_IN, LANES), row),
                pl.BlockSpec(memory_space=pl.ANY),
            ] + in_specs,
            out_specs=[pl.BlockSpec((TM_IN, D_MODEL), row)] + out_specs,
            scratch_shapes=_moe_gather_scratch(TM_IN),
        ),
        out_shape=[jax.ShapeDtypeStruct((T, D_MODEL), F32)] + _mix_in_out_shapes(T),
        compiler_params=pltpu.CompilerParams(dimension_semantics=("arbitrary",), vmem_limit_bytes=VMEM_LIMIT),
        name="moe_mix_in",
    )(dest, x2, info, ybuf, g, w_in, vg, ws, bs_t, ga)


def _stick_break_kernel(q_ref, k_ref, v_ref, o_ref, c_ref, live_ref):
    qi = pl.program_id(1)
    r_idx = lax.broadcasted_iota(jnp.int32, (TK, TK), 0)
    c_idx = lax.broadcasted_iota(jnp.int32, (TK, TK), 1)
    later = jnp.where(c_idx < r_idx, -1.0, 0.0).astype(BF16)

    def sweep(half, key0, n_keys, later_m, visible):
        rows = slice(half * HALF, (half + 1) * HALF)
        first = visible is not None
        sps, lszs = [], []
        for pr in range(SB_PAIRS):
            kb = k_ref[pl.ds(key0, n_keys), pr * LANES:(pr + 1) * LANES]
            qq = jnp.concatenate([q_ref[rows, hd * LANES:(hd + 1) * LANES] for hd in (2 * pr, 2 * pr + 1)], axis=0)
            zz = _dot_nt(qq, kb)
            for z in (zz[0:HALF, :], zz[HALF:2 * HALF, :]):
                sp = jnp.maximum(z, 0.0) + jnp.log(1.0 + jnp.exp2(jnp.abs(z) * (-LOG2E)))
                lszs.append(z - sp)
                sps.append(jnp.where(visible, sp, 0.0) if first else sp)
        after = _dot(jnp.concatenate([sp.astype(BF16) for sp in sps], axis=0), later_m)
        top = None
        for pr in range(SB_PAIRS):
            acc = None
            for hd in (2 * pr, 2 * pr + 1):
                p = jnp.exp(lszs[hd] + after[hd * HALF:(hd + 1) * HALF, :])
                if first:
                    p = jnp.where(visible, p, 0.0)
                term = _dot(p.astype(BF16), v_ref[pl.ds(key0, n_keys), hd * LANES:(hd + 1) * LANES])
                c = -jnp.sum(sps[hd], axis=-1, keepdims=True)
                if not first:
                    c_old = c_ref[hd, rows, :]
                    term = term * jnp.exp(c_old)
                    c = c_old + c
                c_ref[hd, rows, :] = c
                acc = term if acc is None else acc + term
                top = c if top is None else jnp.maximum(top, c)
            lanes = slice(pr * LANES, (pr + 1) * LANES)
            o_ref[rows, lanes] = acc if first else o_ref[rows, lanes] + acc
        edge = jnp.exp(jnp.max(top, axis=0, keepdims=True) + 1.0)
        live_ref[half] = (jnp.max(edge) > 0.0).astype(jnp.int32)

    d0 = pl.multiple_of(qi * TK, TK)
    rows_a = lax.broadcasted_iota(jnp.int32, (HALF, HALF), 0)
    cols_a = lax.broadcasted_iota(jnp.int32, (HALF, HALF), 1)
    sweep(0, d0, HALF, later[0:HALF, 0:HALF], cols_a < rows_a)
    rows_b = lax.broadcasted_iota(jnp.int32, (HALF, TK), 0)
    cols_b = lax.broadcasted_iota(jnp.int32, (HALF, TK), 1)
    sweep(1, d0, TK, later, cols_b < rows_b + HALF)

    def cond(jj):
        return jnp.logical_and(jj < qi, live_ref[0] + live_ref[1] > 0)

    def body(jj):
        key0 = pl.multiple_of((qi - 1 - jj) * TK, TK)
        for half in range(2):
            @pl.when(live_ref[half] > 0)
            def _():
                sweep(half, key0, TK, later, None)
        return jj + 1

    lax.while_loop(cond, body, jnp.int32(0))


def _stick_break(q, k, v, B, S):
    return pl.pallas_call(
        _stick_break_kernel,
        grid=(B, S // TQ),
        in_specs=[
            pl.BlockSpec((None, TQ, SB_HEADS * LANES), lambda b, i: (b, i, 0)),
            pl.BlockSpec((None, S, SB_WIDTH), lambda b, i: (b, 0, 0)),
            pl.BlockSpec((None, S, SB_HEADS * LANES), lambda b, i: (b, 0, 0)),
        ],
        out_specs=pl.BlockSpec((None, TQ, SB_WIDTH), lambda b, i: (b, i, 0)),
        out_shape=jax.ShapeDtypeStruct((B, S, SB_WIDTH), F32),
        scratch_shapes=[pltpu.VMEM((SB_HEADS, TQ, 1), F32), pltpu.SMEM((2,), jnp.int32)],
        compiler_params=pltpu.CompilerParams(
            dimension_semantics=("arbitrary", "arbitrary"), vmem_limit_bytes=VMEM_LIMIT),
        name="stick_break",
    )(q.reshape(B, S, SB_HEADS * LANES), k.reshape(B, S, SB_WIDTH), v.reshape(B, S, SB_HEADS * LANES))


def _mem_kv_kernel(m_ref, g_ref, wk_ref, wv_ref, k_ref, v_ref):
    m = _rms(m_ref[...], g_ref[...]).astype(BF16)
    k_ref[...] = _dot(m, wk_ref[...]).astype(BF16)
    v_ref[...] = _dot(m, wv_ref[...]).astype(BF16)


def _mem_kv(mem, g, wk, wv):
    B, M, _ = mem.shape
    L = wk.shape[0]
    kv_spec = pl.BlockSpec((None, None, M, D_MODEL), lambda l, b: (l, b, 0, 0))
    w_spec = pl.BlockSpec((None, D_MODEL, D_MODEL), lambda l, b: (l, 0, 0))
    return pl.pallas_call(
        _mem_kv_kernel,
        grid=(L, B),
        in_specs=[
            pl.BlockSpec((None, M, D_MODEL), lambda l, b: (b, 0, 0)),
            pl.BlockSpec((None, 1, D_MODEL), lambda l, b: (l, 0, 0)),
            w_spec, w_spec,
        ],
        out_specs=[kv_spec, kv_spec],
        out_shape=[jax.ShapeDtypeStruct((L, B, M, D_MODEL), BF16)] * 2,
        compiler_params=pltpu.CompilerParams(
            dimension_semantics=("arbitrary", "arbitrary"), vmem_limit_bytes=VMEM_LIMIT),
        name="mem_kv",
    )(mem, g, wk, wv)


def _post_kernel(x_ref, ya_ref, yb_ref, gb_ref, wo_ref, gc_ref, wq_ref, kx_ref, vx_ref, wxo_ref, gm_ref,
                 wr_ref, br_ref, x2_ref, hm_ref, info_ref, cnt_ref, run_ref):
    tm = x_ref.shape[0]

    @pl.when(pl.program_id(0) == 0)
    def _():
        run_ref[...] = jnp.zeros_like(run_ref)

    ybn = _rms(yb_ref[...], gb_ref[...]).astype(BF16)
    x1 = x_ref[...] + _dot(jnp.concatenate([ya_ref[...], ybn], axis=-1), wo_ref[...])

    h = _rms(x1, gc_ref[...]).astype(BF16)
    q = (_dot(h, wq_ref[...]) * (1.0 / (X_HEAD_DIM ** 0.5))).astype(BF16)
    heads = []
    for hd in range(X_HEADS):
        cols = slice(hd * X_HEAD_DIM, (hd + 1) * X_HEAD_DIM)
        s = _dot_nt(q[:, cols], kx_ref[:, cols])
        p = jnp.exp(s - jnp.max(s, axis=-1, keepdims=True))
        p = p / jnp.sum(p, axis=-1, keepdims=True)
        heads.append(_dot(p.astype(BF16), vx_ref[:, cols]).astype(BF16))
    x2 = x1 + _dot(jnp.concatenate(heads, axis=-1), wxo_ref[...])
    x2_ref[...] = x2

    hm = _rms(x2, gm_ref[...])
    _store_packed(hm, hm_ref)

    hm_hi, hm_lo = _split_bf16(hm)
    both = _dot(hm_hi, wr_ref[...])
    logits = both[:, 0:LANES] + both[:, LANES:2 * LANES] + _dot(hm_lo, wr_ref[:, 0:LANES]) + br_ref[...]

    lane = lax.broadcasted_iota(jnp.int32, (tm, LANES), 1)
    lanef = lane.astype(F32)
    big = float(LANES)
    is_group = lane < N_GROUPS
    gl = jnp.where(is_group, logits, NEG)
    gmax = jnp.max(gl, axis=-1, keepdims=True)
    gsel = jnp.min(jnp.where(gl == gmax, lanef, big), axis=-1, keepdims=True)
    gden = jnp.sum(jnp.where(is_group, jnp.exp(gl - gmax), 0.0), axis=-1, keepdims=True)
    g_gate = 1.0 / gden
    lo = ROUTE_BASE + EXPERTS_PER_GROUP * gsel
    in_group = (lanef >= lo) & (lanef < lo + EXPERTS_PER_GROUP)
    el = jnp.where(in_group, logits, NEG)
    v1 = jnp.max(el, axis=-1, keepdims=True)
    i1 = jnp.min(jnp.where(el == v1, lanef, big), axis=-1, keepdims=True)
    el2 = jnp.where(lanef == i1, NEG, el)
    v2 = jnp.max(el2, axis=-1, keepdims=True)
    i2 = jnp.min(jnp.where(el2 == v2, lanef, big), axis=-1, keepdims=True)
    t = jnp.exp(v2 - v1)
    den = 1.0 + t
    gate1 = g_gate * (1.0 / den)
    gate2 = g_gate * (t / den)

    hit1 = lanef == i1
    hit2 = lanef == i2
    multi = jnp.where(hit1 | hit2, 1.0, 0.0)
    r_idx = lax.broadcasted_iota(jnp.int32, (tm, tm), 0)
    c_idx = lax.broadcasted_iota(jnp.int32, (tm, tm), 1)
    earlier = jnp.where(c_idx < r_idx, 1.0, 0.0).astype(BF16)
    before = _dot(earlier, multi.astype(BF16)) + run_ref[0:1, :]
    rank1 = jnp.sum(jnp.where(hit1, before, 0.0), axis=-1, keepdims=True)
    rank2 = jnp.sum(jnp.where(hit2, before, 0.0), axis=-1, keepdims=True)
    run_ref[...] = run_ref[...] + jnp.sum(multi, axis=0, keepdims=True)
    cnt_ref[...] = run_ref[...]

    info = jnp.where(lane == 0, i1 - ROUTE_BASE, 0.0)
    info = jnp.where(lane == 1, i2 - ROUTE_BASE, info)
    info = jnp.where(lane == 2, gate1, info)
    info = jnp.where(lane == 3, gate2, info)
    info = jnp.where(lane == 4, rank1, info)
    info = jnp.where(lane == 5, rank2, info)
    info_ref[...] = info


def _post(x2d, ya, yb, gb, wo, gc, wq, kx, vx, wxo, gm, wr, br, l, S):
    T = x2d.shape[0]
    M = kx.shape[2]
    grid = (T // TM_POST,)
    row = lambda i: (i, 0)
    lay = lambda i: (l, 0, 0)
    batch = lambda i: (l, (i * TM_POST) // S, 0, 0)
    wspec = pl.BlockSpec((None, D_MODEL, D_MODEL), lay)
    gspec = pl.BlockSpec((None, 1, D_MODEL), lay)
    return pl.pallas_call(
        _post_kernel,
        grid=grid,
        in_specs=[
            pl.BlockSpec((TM_POST, D_MODEL), row),
            pl.BlockSpec((TM_POST, A_WIDTH), row),
            pl.BlockSpec((TM_POST, SB_WIDTH), row),
            pl.BlockSpec((None, 1, SB_WIDTH), lay),
            wspec, gspec, wspec,
            pl.BlockSpec((None, None, M, D_MODEL), batch),
            pl.BlockSpec((None, None, M, D_MODEL), batch),
            wspec, gspec,
            pl.BlockSpec((None, D_MODEL, 2 * LANES), lay),
            pl.BlockSpec((None, 1, LANES), lay),
        ],
        out_specs=[
            pl.BlockSpec((TM_POST, D_MODEL), row),
            pl.BlockSpec((TM_POST * PACK_TILES, LANES), row),
            pl.BlockSpec((TM_POST, LANES), row),
            pl.BlockSpec((SUBLANES, LANES), lambda i: (0, 0)),
        ],
        out_shape=[
            jax.ShapeDtypeStruct((T, D_MODEL), F32),
            jax.ShapeDtypeStruct((T * PACK_TILES, LANES), U32),
            jax.ShapeDtypeStruct((T, LANES), F32),
            jax.ShapeDtypeStruct((SUBLANES, LANES), F32),
        ],
        scratch_shapes=[pltpu.VMEM((SUBLANES, LANES), F32)],
        compiler_params=pltpu.CompilerParams(dimension_semantics=("arbitrary",), vmem_limit_bytes=VMEM_LIMIT),
        name="post",
    )(x2d, ya, yb, gb, wo, gc, wq, kx, vx, wxo, gm, wr, br)


def _dispatch_kernel(dest_ref, hm_ref, buf_in_ref, buf_ref, sem):
    del buf_in_ref
    base = pl.program_id(0) * (TD * TOP_K)

    def issue(r, c):
        for k in range(TOP_K):
            _row_copy(hm_ref, r, buf_ref, dest_ref[base + TOP_K * r + k], sem).start(priority=k)
        return c

    lax.fori_loop(0, TD, issue, 0)
    for k in range(TOP_K):
        pltpu.make_async_copy(hm_ref, buf_ref.at[pl.ds(0, TD * PACK_TILES), :], sem).wait()


def _dispatch(dest, hm, buf0):
    T = hm.shape[0] // PACK_TILES
    return pl.pallas_call(
        _dispatch_kernel,
        grid_spec=pltpu.PrefetchScalarGridSpec(
            num_scalar_prefetch=1,
            grid=(T // TD,),
            in_specs=[
                pl.BlockSpec((TD * PACK_TILES, LANES), lambda i, d: (i, 0)),
                pl.BlockSpec(memory_space=pl.ANY),
            ],
            out_specs=pl.BlockSpec(memory_space=pl.ANY),
            scratch_shapes=[pltpu.SemaphoreType.DMA(())],
        ),
        out_shape=jax.ShapeDtypeStruct(buf0.shape, U32),
        input_output_aliases={2: 0},
        compiler_params=pltpu.CompilerParams(dimension_semantics=("arbitrary",), vmem_limit_bytes=VMEM_LIMIT),
        name="dispatch",
    )(dest, hm, buf0)


def _experts_kernel(blk_e_ref, first_ref, seg_ref, next_e_ref, n_used_ref, buf_ref, w1_hbm, w3_hbm, w2_hbm, y_ref,
                    w1s_ref, w3s_ref, w2s_ref, w1b_ref, w3b_ref, w2b_ref, sems, *, layer):
    i = pl.program_id(0)

    def fetch(expert, slot):
        return [pltpu.make_async_copy(w_hbm.at[layer, expert], stage.at[slot], sems.at[slot])
                for w_hbm, stage in ((w1_hbm, w1s_ref), (w3_hbm, w3s_ref), (w2_hbm, w2s_ref))]

    @pl.when(first_ref[i] > 0)
    def _():
        slot = seg_ref[i] % 2

        @pl.when(i == 0)
        def _():
            for cp in fetch(blk_e_ref[0], 0):
                cp.start()

        for cp in fetch(blk_e_ref[i], slot):
            cp.wait()
        w1b_ref[...] = w1s_ref[slot].astype(BF16)
        w3b_ref[...] = w3s_ref[slot].astype(BF16)
        w2b_ref[...] = w2s_ref[slot].astype(BF16)

        @pl.when(next_e_ref[i] >= 0)
        def _():
            for cp in fetch(next_e_ref[i], 1 - slot):
                cp.start()

    @pl.when(i < n_used_ref[0])
    def _():
        xb = jnp.concatenate(_load_packed_chunks(buf_ref, 0, BM), axis=-1).astype(BF16)
        h1 = _dot(xb, w1b_ref[...])
        h3 = _dot(xb, w3b_ref[...])
        a = (h1 * jax.nn.sigmoid(h1) * h3).astype(BF16)
        _store_packed(_dot(a, w2b_ref[...]), y_ref)

    @pl.when(i >= n_used_ref[0])
    def _():
        y_ref[...] = jnp.zeros_like(y_ref)


def _experts(blk_e, first, seg, next_e, n_used, buf, w1, w3, w2, l):
    n_blocks = buf.shape[0] // (BM * PACK_TILES)
    rows = pl.BlockSpec((BM * PACK_TILES, LANES), lambda i, *_: (i, 0))
    rows_in = pl.BlockSpec((BM * PACK_TILES, LANES), lambda i, e, f, s, nx, n: (jnp.minimum(i, n[0] - 1), 0))
    hbm = pl.BlockSpec(memory_space=pl.ANY)
    return pl.pallas_call(
        functools.partial(_experts_kernel, layer=l),
        grid_spec=pltpu.PrefetchScalarGridSpec(
            num_scalar_prefetch=5,
            grid=(n_blocks,),
            in_specs=[rows_in, hbm, hbm, hbm],
            out_specs=rows,
            scratch_shapes=[
                pltpu.VMEM((2, D_MODEL, D_EXPERT), F32),
                pltpu.VMEM((2, D_MODEL, D_EXPERT), F32),
                pltpu.VMEM((2, D_EXPERT, D_MODEL), F32),
                pltpu.VMEM((D_MODEL, D_EXPERT), BF16),
                pltpu.VMEM((D_MODEL, D_EXPERT), BF16),
                pltpu.VMEM((D_EXPERT, D_MODEL), BF16),
                pltpu.SemaphoreType.DMA((2,)),
            ],
        ),
        out_shape=jax.ShapeDtypeStruct(buf.shape, U32),
        compiler_params=pltpu.CompilerParams(dimension_semantics=("arbitrary",), vmem_limit_bytes=VMEM_LIMIT),
        name="experts",
    )(blk_e, first, seg, next_e, n_used, buf, w1, w3, w2)


def _combine_final_kernel(dest_ref, x2_ref, info_ref, gf_ref, ybuf_ref, out_ref, rows_ref, sems):
    chunks = _moe_output_chunks(dest_ref, x2_ref, info_ref, ybuf_ref, rows_ref, sems)
    ssq = jnp.zeros((x2_ref.shape[0], 1), F32)
    for xc in chunks:
        ssq = ssq + jnp.sum(xc * xc, axis=-1, keepdims=True)
    inv = lax.rsqrt(ssq * (1.0 / D_MODEL) + EPS)
    for c, xc in enumerate(chunks):
        lanes = slice(c * LANES, (c + 1) * LANES)
        out_ref[:, lanes] = xc * inv * gf_ref[:, lanes]


def _combine_final(dest, x2, info, gf, ybuf):
    T = x2.shape[0]
    row = lambda i, d: (i, 0)
    return pl.pallas_call(
        _combine_final_kernel,
        grid_spec=pltpu.PrefetchScalarGridSpec(
            num_scalar_prefetch=1,
            grid=(T // TE,),
            in_specs=[
                pl.BlockSpec((TE, D_MODEL), row),
                pl.BlockSpec((TE, LANES), row),
                pl.BlockSpec((1, D_MODEL), lambda i, d: (0, 0)),
                pl.BlockSpec(memory_space=pl.ANY),
            ],
            out_specs=pl.BlockSpec((TE, D_MODEL), row),
            scratch_shapes=_moe_gather_scratch(TE),
        ),
        out_shape=jax.ShapeDtypeStruct((T, D_MODEL), F32),
        compiler_params=pltpu.CompilerParams(dimension_semantics=("arbitrary",), vmem_limit_bytes=VMEM_LIMIT),
        name="combine_final",
    )(dest, x2, info, gf, ybuf)


def kernel(x, mem, norm_mix, w_in, v_norm, w_spatial, b_spatial, out_norm_a, out_norm_b, w_out, norm_cross, norm_mem, w_xq, w_xk, w_xv, w_xo, norm_moe, w_group, b_group, w_router, b_router, w1, w3, w2, norm_final):
    B, S, D = x.shape
    L = w_in.shape[0]
    T = B * S
    assert D == D_MODEL and S % TQ == 0 and T % TM_IN == 0 and S % TM_POST == 0 and T % TD == 0 and T % TE == 0

    row3 = lambda a: a.reshape(L, 1, -1)
    w_in_b, w_out_b = w_in.astype(BF16), w_out.astype(BF16)
    w_xq_b, w_xk_b, w_xv_b, w_xo_b = (w.astype(BF16) for w in (w_xq, w_xk, w_xv, w_xo))
    pad = jnp.zeros((L, D, LANES - N_GROUPS - N_EXPERTS), F32)
    w_route = jnp.concatenate([w_group, w_router, pad], axis=-1)
    w_route_hi = w_route.astype(BF16)
    w_route_lo = (w_route - w_route_hi.astype(F32)).astype(BF16)
    w_route_b = jnp.concatenate([w_route_hi, w_route_lo], axis=-1)
    b_route = jnp.concatenate([b_group, b_router, pad[:, 0, :]], axis=-1).reshape(L, 1, LANES)
    bs_t = jnp.swapaxes(b_spatial, 1, 2)

    kx, vx = _mem_kv(mem, row3(norm_mem), w_xk_b, w_xv_b)

    n_slots = T * TOP_K + N_EXPERTS * BM
    n_blocks = n_slots // BM
    xs = x.reshape(T, D)
    moe = None
    for l in range(L):
        mix_args = (row3(norm_mix), w_in_b, row3(v_norm), w_spatial, bs_t, row3(out_norm_a), l)
        if moe is None:
            q, k, v, ya = _mix_in(xs, *mix_args)
        else:
            xs, q, k, v, ya = _moe_mix_in(*moe, *mix_args)
        yb = _stick_break(q, k, v, B, S).reshape(T, SB_WIDTH)
        x2, hm, info, cnt = _post(xs, ya, yb, row3(out_norm_b), w_out_b, row3(norm_cross), w_xq_b, kx, vx, w_xo_b,
                                  row3(norm_moe), w_route_b, b_route, l, S)
        eid = info[:, 0:TOP_K].astype(jnp.int32)
        rank = info[:, 4:4 + TOP_K].astype(jnp.int32)
        counts = cnt[0, ROUTE_BASE:ROUTE_BASE + N_EXPERTS].astype(jnp.int32)
        padded = ((counts + BM - 1) // BM) * BM
        seg_end = jnp.cumsum(padded)
        seg_start = seg_end - padded
        expert_ids = jnp.arange(N_EXPERTS, dtype=jnp.int32)
        dest = (rank + jnp.sum(jnp.where(eid[..., None] == expert_ids, seg_start, 0), axis=-1)).reshape(-1)
        blk_row = jnp.arange(n_blocks, dtype=jnp.int32) * BM
        blk_e = jnp.minimum(
            jnp.sum((seg_end[None, :] <= blk_row[:, None]).astype(jnp.int32), axis=1), N_EXPERTS - 1)
        n_used = (seg_end[-1:] // BM).astype(jnp.int32)
        nonempty = counts > 0
        first = jnp.logical_and(blk_row == seg_start[blk_e], blk_row < seg_end[-1]).astype(jnp.int32)
        seg = (jnp.cumsum(nonempty.astype(jnp.int32)) - 1)[blk_e]
        later_id = jnp.where(nonempty, expert_ids, N_EXPERTS)
        after = jnp.concatenate([lax.cummin(later_id[::-1])[::-1][1:], jnp.full((1,), N_EXPERTS, jnp.int32)])
        next_e = jnp.where(after < N_EXPERTS, after, -1)[blk_e]
        buf = _dispatch(dest, hm, jnp.zeros((n_slots * PACK_TILES, LANES), U32))
        ybuf = _experts(blk_e, first, seg, next_e, n_used, buf, w1, w3, w2, l)
        moe = (dest, x2, info, ybuf)
    dest, x2, info, ybuf = moe
    return _combine_final(dest, x2, info, norm_final.reshape(1, D), ybuf).reshape(B, S, D)
```

```python
import functools

import jax
import jax.numpy as jnp
from jax import lax
from jax.experimental import pallas as pl
from jax.experimental.pallas import tpu as pltpu

F32 = jnp.float32
BF16 = jnp.bfloat16
U32 = jnp.uint32

D_MODEL = 1024
A_WIDTH = 512
A_GROUPS = 4
A_CH = 128
CHUNK = 128
SB_WIDTH = 512
SB_HEADS = 8
SB_HEAD_DIM = 64
SB_PAIRS = SB_HEADS // 2
IN_WIDTH = 2 * A_WIDTH + 3 * SB_WIDTH
X_HEADS = 4
X_HEAD_DIM = 256
N_GROUPS = 4
EXPERTS_PER_GROUP = 8
N_EXPERTS = 32
TOP_K = 2
D_EXPERT = 512
EPS = 1e-6

LANES = 128
SUBLANES = 8
ROW_TILES = D_MODEL // LANES
PACK_TILES = ROW_TILES // 2

TM_IN = 512
TQ = 256
TK = 256
HALF = TQ // 2
TM_POST = 512
TD = 512
BM = 512
TE = 512
ROUTE_BASE = N_GROUPS
NEG = -1e30
LOG2E = 1.4426950408889634

VMEM_LIMIT = 52 * 1024 * 1024


def _rms(x, g):
    return x * lax.rsqrt(jnp.mean(x * x, axis=-1, keepdims=True) + EPS) * g


def _dot(a, b):
    return jnp.dot(a, b, preferred_element_type=F32)


def _dot_nt(a, b):
    return lax.dot_general(a, b, (((1,), (1,)), ((), ())), preferred_element_type=F32)


def _split_bf16(x):
    hi = x.astype(BF16)
    lo = (x - hi.astype(F32)).astype(BF16)
    return hi, lo


def _row_copy(src, src_row, dst, dst_row, sem):
    return pltpu.make_async_copy(
        src.at[pl.ds(pl.multiple_of(src_row * PACK_TILES, PACK_TILES), PACK_TILES), :],
        dst.at[pl.ds(pl.multiple_of(dst_row * PACK_TILES, PACK_TILES), PACK_TILES), :],
        sem)


def _store_packed(x, ref):
    half = D_MODEL // 2
    hi = pltpu.bitcast(x[:, 0:half].astype(BF16).astype(F32), U32)
    lo = pltpu.bitcast(x[:, half:D_MODEL].astype(BF16).astype(F32), U32)
    words = hi | (lo >> 16)
    for c in range(PACK_TILES):
        ref[pl.ds(c, x.shape[0], stride=PACK_TILES), :] = words[:, c * LANES:(c + 1) * LANES]


def _load_packed_chunks(ref, row0, rows):
    words = [ref[pl.ds(row0 * PACK_TILES + c, rows, stride=PACK_TILES), :] for c in range(PACK_TILES)]
    return ([pltpu.bitcast(w & jnp.uint32(0xFFFF0000), F32) for w in words]
            + [pltpu.bitcast(w << 16, F32) for w in words])


def _moe_output_chunks(dest_ref, x2_ref, info_ref, ybuf_ref, rows_ref, sems):
    tm = x2_ref.shape[0]
    i = pl.program_id(0)

    def start_gather(step, slot):
        base = step * (tm * TOP_K)

        def issue(r, c):
            for k in range(TOP_K):
                _row_copy(ybuf_ref, dest_ref[base + TOP_K * r + k], rows_ref.at[slot], k * tm + r,
                          sems.at[slot]).start(priority=k)
            return c

        lax.fori_loop(0, tm, issue, 0)

    @pl.when(i == 0)
    def _():
        start_gather(0, 0)

    @pl.when(i + 1 < pl.num_programs(0))
    def _():
        start_gather(i + 1, (i + 1) % 2)

    slot = i % 2
    rows = rows_ref.at[slot]
    pltpu.make_async_copy(ybuf_ref.at[pl.ds(0, TOP_K * tm * PACK_TILES), :], rows, sems.at[slot]).wait()

    info = info_ref[...]
    gate1 = info[:, 2:3]
    gate2 = info[:, 3:4]
    y1 = _load_packed_chunks(rows, 0, tm)
    y2 = _load_packed_chunks(rows, tm, tm)
    return [x2_ref[:, c * LANES:(c + 1) * LANES] + (gate1 * y1[c] + gate2 * y2[c]) for c in range(ROW_TILES)]


def _moe_gather_scratch(tm):
    return [pltpu.VMEM((2, TOP_K * tm * PACK_TILES, LANES), U32), pltpu.SemaphoreType.DMA((2,))]


def _mix_in_body(x, g_ref, w_ref, vg_ref, ws_ref, bs_ref, ga_ref, q_ref, k_ref, v_ref, ya_ref):
    tm = x.shape[0]
    h = _rms(x, g_ref[...]).astype(BF16)
    z = _dot(h, w_ref[...])

    t_idx = lax.broadcasted_iota(jnp.int32, (CHUNK, CHUNK), 0)
    s_idx = lax.broadcasted_iota(jnp.int32, (CHUNK, CHUNK), 1)
    causal = s_idx <= t_idx
    parts = []
    ssq = jnp.zeros((tm, 1), F32)
    for g in range(A_GROUPS):
        lanes = slice(g * A_CH, (g + 1) * A_CH)
        u = jax.nn.gelu(z[:, g * A_CH:(g + 1) * A_CH])
        vg = jax.nn.gelu(z[:, A_WIDTH + g * A_CH:A_WIDTH + (g + 1) * A_CH])
        vn = _rms(vg, vg_ref[:, lanes]).astype(BF16)
        ws = jnp.where(causal, ws_ref[g], 0.0).astype(BF16)
        bias = bs_ref[:, g:g + 1]
        mixed = jnp.concatenate(
            [_dot(ws, vn[c * CHUNK:(c + 1) * CHUNK, :]) + bias for c in range(tm // CHUNK)], axis=0)
        ya = u * mixed
        parts.append(ya)
        ssq = ssq + jnp.sum(ya * ya, axis=-1, keepdims=True)
    inv = lax.rsqrt(ssq * (1.0 / A_WIDTH) + EPS)
    for g in range(A_GROUPS):
        lanes = slice(g * A_CH, (g + 1) * A_CH)
        ya_ref[:, lanes] = (parts[g] * inv * ga_ref[:, lanes]).astype(BF16)

    lane = lax.broadcasted_iota(jnp.int32, (1, LANES), 1)
    q0 = 2 * A_WIDTH
    k0 = q0 + SB_WIDTH
    v0 = k0 + SB_WIDTH
    scale = 1.0 / (SB_HEAD_DIM ** 0.5)
    for hd in range(SB_HEADS):
        pair = hd // 2
        keep = (lane < SB_HEAD_DIM) if hd % 2 == 0 else (lane >= SB_HEAD_DIM)
        qs = z[:, q0 + pair * LANES:q0 + (pair + 1) * LANES] * scale
        vs = z[:, v0 + pair * LANES:v0 + (pair + 1) * LANES]
        q_ref[:, hd * LANES:(hd + 1) * LANES] = jnp.where(keep, qs, 0.0).astype(BF16)
        v_ref[:, hd * LANES:(hd + 1) * LANES] = jnp.where(keep, vs, 0.0).astype(BF16)
    k_ref[...] = z[:, k0:v0].astype(BF16)


def _cast_once(w_ref, wb_ref):
    @pl.when(pl.program_id(0) == 0)
    def _():
        wb_ref[...] = w_ref[...].astype(BF16)


def _mix_in_kernel(x_ref, g_ref, w_ref, vg_ref, ws_ref, bs_ref, ga_ref, q_ref, k_ref, v_ref, ya_ref, wb_ref):
    _cast_once(w_ref, wb_ref)
    _mix_in_body(x_ref[...], g_ref, wb_ref, vg_ref, ws_ref, bs_ref, ga_ref, q_ref, k_ref, v_ref, ya_ref)


def _moe_mix_in_kernel(dest_ref, x2_ref, info_ref, ybuf_ref, g_ref, w_ref, vg_ref, ws_ref, bs_ref, ga_ref,
                       x_ref, q_ref, k_ref, v_ref, ya_ref, rows_ref, sems, wb_ref):
    _cast_once(w_ref, wb_ref)
    x = jnp.concatenate(_moe_output_chunks(dest_ref, x2_ref, info_ref, ybuf_ref, rows_ref, sems), axis=-1)
    x_ref[...] = x
    _mix_in_body(x, g_ref, wb_ref, vg_ref, ws_ref, bs_ref, ga_ref, q_ref, k_ref, v_ref, ya_ref)


def _mix_in_specs(l, index):
    lay = lambda *a: (l, 0, 0)
    in_specs = [
        pl.BlockSpec((None, 1, D_MODEL), lay),
        pl.BlockSpec((None, D_MODEL, IN_WIDTH), lay),
        pl.BlockSpec((None, 1, A_WIDTH), lay),
        pl.BlockSpec((None, A_GROUPS, CHUNK, CHUNK), lambda *a: (l, 0, 0, 0)),
        pl.BlockSpec((None, CHUNK, A_GROUPS), lay),
        pl.BlockSpec((None, 1, A_WIDTH), lay),
    ]
    out_specs = [
        pl.BlockSpec((TM_IN, SB_HEADS * LANES), index),
        pl.BlockSpec((TM_IN, SB_WIDTH), index),
        pl.BlockSpec((TM_IN, SB_HEADS * LANES), index),
        pl.BlockSpec((TM_IN, A_WIDTH), index),
    ]
    return in_specs, out_specs


def _mix_in_out_shapes(T):
    return [
        jax.ShapeDtypeStruct((T, SB_HEADS * LANES), BF16),
        jax.ShapeDtypeStruct((T, SB_WIDTH), BF16),
        jax.ShapeDtypeStruct((T, SB_HEADS * LANES), BF16),
        jax.ShapeDtypeStruct((T, A_WIDTH), BF16),
    ]


def _mix_in(x2d, g, w_in, vg, ws, bs_t, ga, l):
    T = x2d.shape[0]
    row = lambda i: (i, 0)
    in_specs, out_specs = _mix_in_specs(l, row)
    return pl.pallas_call(
        _mix_in_kernel,
        grid=(T // TM_IN,),
        in_specs=[pl.BlockSpec((TM_IN, D_MODEL), row)] + in_specs,
        out_specs=out_specs,
        out_shape=_mix_in_out_shapes(T),
        scratch_shapes=[pltpu.VMEM((D_MODEL, IN_WIDTH), BF16)],
        compiler_params=pltpu.CompilerParams(dimension_semantics=("arbitrary",), vmem_limit_bytes=VMEM_LIMIT),
        name="mix_in",
    )(x2d, g, w_in, vg, ws, bs_t, ga)


def _moe_mix_in(dest, x2, info, ybuf, g, w_in, vg, ws, bs_t, ga, l):
    T = x2.shape[0]
    row = lambda i, d: (i, 0)
    in_specs, out_specs = _mix_in_specs(l, row)
    return pl.pallas_call(
        _moe_mix_in_kernel,
        grid_spec=pltpu.PrefetchScalarGridSpec(
            num_scalar_prefetch=1,
            grid=(T // TM_IN,),
            in_specs=[
                pl.BlockSpec((TM_IN, D_MODEL), row),
                pl.BlockSpec((TM_IN, LANES), row),
                pl.BlockSpec(memory_space=pl.ANY),
            ] + in_specs,
            out_specs=[pl.BlockSpec((TM_IN, D_MODEL), row)] + out_specs,
            scratch_shapes=_moe_gather_scratch(TM_IN) + [pltpu.VMEM((D_MODEL, IN_WIDTH), BF16)],
        ),
        out_shape=[jax.ShapeDtypeStruct((T, D_MODEL), F32)] + _mix_in_out_shapes(T),
        compiler_params=pltpu.CompilerParams(dimension_semantics=("arbitrary",), vmem_limit_bytes=VMEM_LIMIT),
        name="moe_mix_in",
    )(dest, x2, info, ybuf, g, w_in, vg, ws, bs_t, ga)


def _stick_break_kernel(q_ref, k_ref, v_ref, o_ref, c_ref, live_ref):
    qi = pl.program_id(1)
    r_idx = lax.broadcasted_iota(jnp.int32, (TK, TK), 0)
    c_idx = lax.broadcasted_iota(jnp.int32, (TK, TK), 1)
    later = jnp.where(c_idx < r_idx, -1.0, 0.0).astype(BF16)

    def sweep(half, key0, n_keys, later_m, visible):
        rows = slice(half * HALF, (half + 1) * HALF)
        first = visible is not None
        sps, lszs = [], []
        for pr in range(SB_PAIRS):
            kb = k_ref[pl.ds(key0, n_keys), pr * LANES:(pr + 1) * LANES]
            qq = jnp.concatenate([q_ref[rows, hd * LANES:(hd + 1) * LANES] for hd in (2 * pr, 2 * pr + 1)], axis=0)
            zz = _dot_nt(qq, kb)
            for z in (zz[0:HALF, :], zz[HALF:2 * HALF, :]):
                sp = jnp.maximum(z, 0.0) + jnp.log(1.0 + jnp.exp2(jnp.abs(z) * (-LOG2E)))
                lszs.append(z - sp)
                sps.append(jnp.where(visible, sp, 0.0) if first else sp)
        after = _dot(jnp.concatenate([sp.astype(BF16) for sp in sps], axis=0), later_m)
        top = None
        for pr in range(SB_PAIRS):
            acc = None
            for hd in (2 * pr, 2 * pr + 1):
                p = jnp.exp(lszs[hd] + after[hd * HALF:(hd + 1) * HALF, :])
                if first:
                    p = jnp.where(visible, p, 0.0)
                term = _dot(p.astype(BF16), v_ref[pl.ds(key0, n_keys), hd * LANES:(hd + 1) * LANES])
                c = -jnp.sum(sps[hd], axis=-1, keepdims=True)
                if not first:
                    c_old = c_ref[hd, rows, :]
                    term = term * jnp.exp(c_old)
                    c = c_old + c
                c_ref[hd, rows, :] = c
                acc = term if acc is None else acc + term
                top = c if top is None else jnp.maximum(top, c)
            lanes = slice(pr * LANES, (pr + 1) * LANES)
            o_ref[rows, lanes] = acc if first else o_ref[rows, lanes] + acc
        edge = jnp.exp(jnp.max(top, axis=0, keepdims=True) + 1.0)
        live_ref[half] = (jnp.max(edge) > 0.0).astype(jnp.int32)

    d0 = pl.multiple_of(qi * TK, TK)
    rows_a = lax.broadcasted_iota(jnp.int32, (HALF, HALF), 0)
    cols_a = lax.broadcasted_iota(jnp.int32, (HALF, HALF), 1)
    sweep(0, d0, HALF, later[0:HALF, 0:HALF], cols_a < rows_a)
    rows_b = lax.broadcasted_iota(jnp.int32, (HALF, TK), 0)
    cols_b = lax.broadcasted_iota(jnp.int32, (HALF, TK), 1)
    sweep(1, d0, TK, later, cols_b < rows_b + HALF)

    def cond(jj):
        return jnp.logical_and(jj < qi, live_ref[0] + live_ref[1] > 0)

    def body(jj):
        key0 = pl.multiple_of((qi - 1 - jj) * TK, TK)
        for half in range(2):
            @pl.when(live_ref[half] > 0)
            def _():
                sweep(half, key0, TK, later, None)
        return jj + 1

    lax.while_loop(cond, body, jnp.int32(0))


def _stick_break(q, k, v, B, S):
    return pl.pallas_call(
        _stick_break_kernel,
        grid=(B, S // TQ),
        in_specs=[
            pl.BlockSpec((None, TQ, SB_HEADS * LANES), lambda b, i: (b, i, 0)),
            pl.BlockSpec((None, S, SB_WIDTH), lambda b, i: (b, 0, 0)),
            pl.BlockSpec((None, S, SB_HEADS * LANES), lambda b, i: (b, 0, 0)),
        ],
        out_specs=pl.BlockSpec((None, TQ, SB_WIDTH), lambda b, i: (b, i, 0)),
        out_shape=jax.ShapeDtypeStruct((B, S, SB_WIDTH), F32),
        scratch_shapes=[pltpu.VMEM((SB_HEADS, TQ, 1), F32), pltpu.SMEM((2,), jnp.int32)],
        compiler_params=pltpu.CompilerParams(
            dimension_semantics=("arbitrary", "arbitrary"), vmem_limit_bytes=VMEM_LIMIT),
        name="stick_break",
    )(q.reshape(B, S, SB_HEADS * LANES), k.reshape(B, S, SB_WIDTH), v.reshape(B, S, SB_HEADS * LANES))


def _mem_kv_kernel(m_ref, g_ref, wk_ref, wv_ref, k_ref, v_ref, wkb_ref, wvb_ref):
    @pl.when(pl.program_id(1) == 0)
    def _():
        wkb_ref[...] = wk_ref[...].astype(BF16)
        wvb_ref[...] = wv_ref[...].astype(BF16)

    m = _rms(m_ref[...], g_ref[...]).astype(BF16)
    k_ref[...] = _dot(m, wkb_ref[...]).astype(BF16)
    v_ref[...] = _dot(m, wvb_ref[...]).astype(BF16)


def _mem_kv(mem, g, wk, wv):
    B, M, _ = mem.shape
    L = wk.shape[0]
    kv_spec = pl.BlockSpec((None, None, M, D_MODEL), lambda l, b: (l, b, 0, 0))
    w_spec = pl.BlockSpec((None, D_MODEL, D_MODEL), lambda l, b: (l, 0, 0))
    return pl.pallas_call(
        _mem_kv_kernel,
        grid=(L, B),
        in_specs=[
            pl.BlockSpec((None, M, D_MODEL), lambda l, b: (b, 0, 0)),
            pl.BlockSpec((None, 1, D_MODEL), lambda l, b: (l, 0, 0)),
            w_spec, w_spec,
        ],
        out_specs=[kv_spec, kv_spec],
        out_shape=[jax.ShapeDtypeStruct((L, B, M, D_MODEL), BF16)] * 2,
        scratch_shapes=[pltpu.VMEM((D_MODEL, D_MODEL), BF16)] * 2,
        compiler_params=pltpu.CompilerParams(
            dimension_semantics=("arbitrary", "arbitrary"), vmem_limit_bytes=VMEM_LIMIT),
        name="mem_kv",
    )(mem, g, wk, wv)


def _post_kernel(x_ref, ya_ref, yb_ref, gb_ref, wo_ref, gc_ref, wq_ref, kx_ref, vx_ref, wxo_ref, gm_ref,
                 wr_ref, br_ref, x2_ref, hm_ref, info_ref, cnt_ref, run_ref, wob_ref, wqb_ref, wxob_ref):
    tm = x_ref.shape[0]

    @pl.when(pl.program_id(0) == 0)
    def _():
        run_ref[...] = jnp.zeros_like(run_ref)

    _cast_once(wo_ref, wob_ref)
    _cast_once(wq_ref, wqb_ref)
    _cast_once(wxo_ref, wxob_ref)

    ybn = _rms(yb_ref[...], gb_ref[...]).astype(BF16)
    x1 = x_ref[...] + _dot(jnp.concatenate([ya_ref[...], ybn], axis=-1), wob_ref[...])

    h = _rms(x1, gc_ref[...]).astype(BF16)
    q = (_dot(h, wqb_ref[...]) * (1.0 / (X_HEAD_DIM ** 0.5))).astype(BF16)
    heads = []
    for hd in range(X_HEADS):
        cols = slice(hd * X_HEAD_DIM, (hd + 1) * X_HEAD_DIM)
        s = _dot_nt(q[:, cols], kx_ref[:, cols])
        p = jnp.exp(s - jnp.max(s, axis=-1, keepdims=True))
        p = p / jnp.sum(p, axis=-1, keepdims=True)
        heads.append(_dot(p.astype(BF16), vx_ref[:, cols]).astype(BF16))
    x2 = x1 + _dot(jnp.concatenate(heads, axis=-1), wxob_ref[...])
    x2_ref[...] = x2

    hm = _rms(x2, gm_ref[...])
    _store_packed(hm, hm_ref)

    hm_hi, hm_lo = _split_bf16(hm)
    both = _dot(hm_hi, wr_ref[...])
    logits = both[:, 0:LANES] + both[:, LANES:2 * LANES] + _dot(hm_lo, wr_ref[:, 0:LANES]) + br_ref[...]

    lane = lax.broadcasted_iota(jnp.int32, (tm, LANES), 1)
    lanef = lane.astype(F32)
    big = float(LANES)
    is_group = lane < N_GROUPS
    gl = jnp.where(is_group, logits, NEG)
    gmax = jnp.max(gl, axis=-1, keepdims=True)
    gsel = jnp.min(jnp.where(gl == gmax, lanef, big), axis=-1, keepdims=True)
    gden = jnp.sum(jnp.where(is_group, jnp.exp(gl - gmax), 0.0), axis=-1, keepdims=True)
    g_gate = 1.0 / gden
    lo = ROUTE_BASE + EXPERTS_PER_GROUP * gsel
    in_group = (lanef >= lo) & (lanef < lo + EXPERTS_PER_GROUP)
    el = jnp.where(in_group, logits, NEG)
    v1 = jnp.max(el, axis=-1, keepdims=True)
    i1 = jnp.min(jnp.where(el == v1, lanef, big), axis=-1, keepdims=True)
    el2 = jnp.where(lanef == i1, NEG, el)
    v2 = jnp.max(el2, axis=-1, keepdims=True)
    i2 = jnp.min(jnp.where(el2 == v2, lanef, big), axis=-1, keepdims=True)
    t = jnp.exp(v2 - v1)
    den = 1.0 + t
    gate1 = g_gate * (1.0 / den)
    gate2 = g_gate * (t / den)

    hit1 = lanef == i1
    hit2 = lanef == i2
    multi = jnp.where(hit1 | hit2, 1.0, 0.0)
    r_idx = lax.broadcasted_iota(jnp.int32, (tm, tm), 0)
    c_idx = lax.broadcasted_iota(jnp.int32, (tm, tm), 1)
    earlier = jnp.where(c_idx < r_idx, 1.0, 0.0).astype(BF16)
    before = _dot(earlier, multi.astype(BF16)) + run_ref[0:1, :]
    rank1 = jnp.sum(jnp.where(hit1, before, 0.0), axis=-1, keepdims=True)
    rank2 = jnp.sum(jnp.where(hit2, before, 0.0), axis=-1, keepdims=True)
    run_ref[...] = run_ref[...] + jnp.sum(multi, axis=0, keepdims=True)
    cnt_ref[...] = run_ref[...]

    info = jnp.where(lane == 0, i1 - ROUTE_BASE, 0.0)
    info = jnp.where(lane == 1, i2 - ROUTE_BASE, info)
    info = jnp.where(lane == 2, gate1, info)
    info = jnp.where(lane == 3, gate2, info)
    info = jnp.where(lane == 4, rank1, info)
    info = jnp.where(lane == 5, rank2, info)
    info_ref[...] = info


def _post(x2d, ya, yb, gb, wo, gc, wq, kx, vx, wxo, gm, wr, br, l, S):
    T = x2d.shape[0]
    M = kx.shape[2]
    grid = (T // TM_POST,)
    row = lambda i: (i, 0)
    lay = lambda i: (l, 0, 0)
    batch = lambda i: (l, (i * TM_POST) // S, 0, 0)
    wspec = pl.BlockSpec((None, D_MODEL, D_MODEL), lay)
    gspec = pl.BlockSpec((None, 1, D_MODEL), lay)
    return pl.pallas_call(
        _post_kernel,
        grid=grid,
        in_specs=[
            pl.BlockSpec((TM_POST, D_MODEL), row),
            pl.BlockSpec((TM_POST, A_WIDTH), row),
            pl.BlockSpec((TM_POST, SB_WIDTH), row),
            pl.BlockSpec((None, 1, SB_WIDTH), lay),
            wspec, gspec, wspec,
            pl.BlockSpec((None, None, M, D_MODEL), batch),
            pl.BlockSpec((None, None, M, D_MODEL), batch),
            wspec, gspec,
            pl.BlockSpec((None, D_MODEL, 2 * LANES), lay),
            pl.BlockSpec((None, 1, LANES), lay),
        ],
        out_specs=[
            pl.BlockSpec((TM_POST, D_MODEL), row),
            pl.BlockSpec((TM_POST * PACK_TILES, LANES), row),
            pl.BlockSpec((TM_POST, LANES), row),
            pl.BlockSpec((SUBLANES, LANES), lambda i: (0, 0)),
        ],
        out_shape=[
            jax.ShapeDtypeStruct((T, D_MODEL), F32),
            jax.ShapeDtypeStruct((T * PACK_TILES, LANES), U32),
            jax.ShapeDtypeStruct((T, LANES), F32),
            jax.ShapeDtypeStruct((SUBLANES, LANES), F32),
        ],
        scratch_shapes=[pltpu.VMEM((SUBLANES, LANES), F32)] + [pltpu.VMEM((D_MODEL, D_MODEL), BF16)] * 3,
        compiler_params=pltpu.CompilerParams(dimension_semantics=("arbitrary",), vmem_limit_bytes=VMEM_LIMIT),
        name="post",
    )(x2d, ya, yb, gb, wo, gc, wq, kx, vx, wxo, gm, wr, br)


def _dispatch_kernel(dest_ref, hm_ref, buf_in_ref, buf_ref, sem):
    del buf_in_ref
    base = pl.program_id(0) * (TD * TOP_K)

    def issue(r, c):
        for k in range(TOP_K):
            _row_copy(hm_ref, r, buf_ref, dest_ref[base + TOP_K * r + k], sem).start(priority=k)
        return c

    lax.fori_loop(0, TD, issue, 0)
    for k in range(TOP_K):
        pltpu.make_async_copy(hm_ref, buf_ref.at[pl.ds(0, TD * PACK_TILES), :], sem).wait()


def _dispatch(dest, hm, buf0):
    T = hm.shape[0] // PACK_TILES
    return pl.pallas_call(
        _dispatch_kernel,
        grid_spec=pltpu.PrefetchScalarGridSpec(
            num_scalar_prefetch=1,
            grid=(T // TD,),
            in_specs=[
                pl.BlockSpec((TD * PACK_TILES, LANES), lambda i, d: (i, 0)),
                pl.BlockSpec(memory_space=pl.ANY),
            ],
            out_specs=pl.BlockSpec(memory_space=pl.ANY),
            scratch_shapes=[pltpu.SemaphoreType.DMA(())],
        ),
        out_shape=jax.ShapeDtypeStruct(buf0.shape, U32),
        input_output_aliases={2: 0},
        compiler_params=pltpu.CompilerParams(dimension_semantics=("arbitrary",), vmem_limit_bytes=VMEM_LIMIT),
        name="dispatch",
    )(dest, hm, buf0)


def _experts_kernel(blk_e_ref, first_ref, seg_ref, next_e_ref, n_used_ref, buf_ref, w1_hbm, w3_hbm, w2_hbm, y_ref,
                    w1s_ref, w3s_ref, w2s_ref, w1b_ref, w3b_ref, w2b_ref, sems, *, layer):
    i = pl.program_id(0)

    def fetch(expert, slot):
        return [pltpu.make_async_copy(w_hbm.at[layer, expert], stage.at[slot], sems.at[slot])
                for w_hbm, stage in ((w1_hbm, w1s_ref), (w3_hbm, w3s_ref), (w2_hbm, w2s_ref))]

    @pl.when(first_ref[i] > 0)
    def _():
        slot = seg_ref[i] % 2

        @pl.when(i == 0)
        def _():
            for cp in fetch(blk_e_ref[0], 0):
                cp.start()

        for cp in fetch(blk_e_ref[i], slot):
            cp.wait()
        w1b_ref[...] = w1s_ref[slot].astype(BF16)
        w3b_ref[...] = w3s_ref[slot].astype(BF16)
        w2b_ref[...] = w2s_ref[slot].astype(BF16)

        @pl.when(next_e_ref[i] >= 0)
        def _():
            for cp in fetch(next_e_ref[i], 1 - slot):
                cp.start()

    @pl.when(i < n_used_ref[0])
    def _():
        xb = jnp.concatenate(_load_packed_chunks(buf_ref, 0, BM), axis=-1).astype(BF16)
        h1 = _dot(xb, w1b_ref[...])
        h3 = _dot(xb, w3b_ref[...])
        a = (h1 * jax.nn.sigmoid(h1) * h3).astype(BF16)
        _store_packed(_dot(a, w2b_ref[...]), y_ref)

    @pl.when(i >= n_used_ref[0])
    def _():
        y_ref[...] = jnp.zeros_like(y_ref)


def _experts(blk_e, first, seg, next_e, n_used, buf, w1, w3, w2, l):
    n_blocks = buf.shape[0] // (BM * PACK_TILES)
    rows = pl.BlockSpec((BM * PACK_TILES, LANES), lambda i, *_: (i, 0))
    rows_in = pl.BlockSpec((BM * PACK_TILES, LANES), lambda i, e, f, s, nx, n: (jnp.minimum(i, n[0] - 1), 0))
    hbm = pl.BlockSpec(memory_space=pl.ANY)
    return pl.pallas_call(
        functools.partial(_experts_kernel, layer=l),
        grid_spec=pltpu.PrefetchScalarGridSpec(
            num_scalar_prefetch=5,
            grid=(n_blocks,),
            in_specs=[rows_in, hbm, hbm, hbm],
            out_specs=rows,
            scratch_shapes=[
                pltpu.VMEM((2, D_MODEL, D_EXPERT), F32),
                pltpu.VMEM((2, D_MODEL, D_EXPERT), F32),
                pltpu.VMEM((2, D_EXPERT, D_MODEL), F32),
                pltpu.VMEM((D_MODEL, D_EXPERT), BF16),
                pltpu.VMEM((D_MODEL, D_EXPERT), BF16),
                pltpu.VMEM((D_EXPERT, D_MODEL), BF16),
                pltpu.SemaphoreType.DMA((2,)),
            ],
        ),
        out_shape=jax.ShapeDtypeStruct(buf.shape, U32),
        compiler_params=pltpu.CompilerParams(dimension_semantics=("arbitrary",), vmem_limit_bytes=VMEM_LIMIT),
        name="experts",
    )(blk_e, first, seg, next_e, n_used, buf, w1, w3, w2)


def _combine_final_kernel(dest_ref, x2_ref, info_ref, gf_ref, ybuf_ref, out_ref, rows_ref, sems):
    chunks = _moe_output_chunks(dest_ref, x2_ref, info_ref, ybuf_ref, rows_ref, sems)
    ssq = jnp.zeros((x2_ref.shape[0], 1), F32)
    for xc in chunks:
        ssq = ssq + jnp.sum(xc * xc, axis=-1, keepdims=True)
    inv = lax.rsqrt(ssq * (1.0 / D_MODEL) + EPS)
    for c, xc in enumerate(chunks):
        lanes = slice(c * LANES, (c + 1) * LANES)
        out_ref[:, lanes] = xc * inv * gf_ref[:, lanes]


def _combine_final(dest, x2, info, gf, ybuf):
    T = x2.shape[0]
    row = lambda i, d: (i, 0)
    return pl.pallas_call(
        _combine_final_kernel,
        grid_spec=pltpu.PrefetchScalarGridSpec(
            num_scalar_prefetch=1,
            grid=(T // TE,),
            in_specs=[
                pl.BlockSpec((TE, D_MODEL), row),
                pl.BlockSpec((TE, LANES), row),
                pl.BlockSpec((1, D_MODEL), lambda i, d: (0, 0)),
                pl.BlockSpec(memory_space=pl.ANY),
            ],
            out_specs=pl.BlockSpec((TE, D_MODEL), row),
            scratch_shapes=_moe_gather_scratch(TE),
        ),
        out_shape=jax.ShapeDtypeStruct((T, D_MODEL), F32),
        compiler_params=pltpu.CompilerParams(dimension_semantics=("arbitrary",), vmem_limit_bytes=VMEM_LIMIT),
        name="combine_final",
    )(dest, x2, info, gf, ybuf)


def kernel(x, mem, norm_mix, w_in, v_norm, w_spatial, b_spatial, out_norm_a, out_norm_b, w_out, norm_cross, norm_mem, w_xq, w_xk, w_xv, w_xo, norm_moe, w_group, b_group, w_router, b_router, w1, w3, w2, norm_final):
    B, S, D = x.shape
    L = w_in.shape[0]
    T = B * S
    assert D == D_MODEL and S % TQ == 0 and T % TM_IN == 0 and S % TM_POST == 0 and T % TD == 0 and T % TE == 0

    row3 = lambda a: a.reshape(L, 1, -1)
    pad = jnp.zeros((L, D, LANES - N_GROUPS - N_EXPERTS), F32)
    w_route = jnp.concatenate([w_group, w_router, pad], axis=-1)
    w_route_hi = w_route.astype(BF16)
    w_route_lo = (w_route - w_route_hi.astype(F32)).astype(BF16)
    w_route_b = jnp.concatenate([w_route_hi, w_route_lo], axis=-1)
    b_route = jnp.concatenate([b_group, b_router, pad[:, 0, :]], axis=-1).reshape(L, 1, LANES)
    bs_t = jnp.swapaxes(b_spatial, 1, 2)

    kx, vx = _mem_kv(mem, row3(norm_mem), w_xk, w_xv)

    n_slots = T * TOP_K + N_EXPERTS * BM
    n_blocks = n_slots // BM
    xs = x.reshape(T, D)
    moe = None
    for l in range(L):
        mix_args = (row3(norm_mix), w_in, row3(v_norm), w_spatial, bs_t, row3(out_norm_a), l)
        if moe is None:
            q, k, v, ya = _mix_in(xs, *mix_args)
        else:
            xs, q, k, v, ya = _moe_mix_in(*moe, *mix_args)
        yb = _stick_break(q, k, v, B, S).reshape(T, SB_WIDTH)
        x2, hm, info, cnt = _post(xs, ya, yb, row3(out_norm_b), w_out, row3(norm_cross), w_xq, kx, vx, w_xo,
                                  row3(norm_moe), w_route_b, b_route, l, S)
        eid = info[:, 0:TOP_K].astype(jnp.int32)
        rank = info[:, 4:4 + TOP_K].astype(jnp.int32)
        counts = cnt[0, ROUTE_BASE:ROUTE_BASE + N_EXPERTS].astype(jnp.int32)
        padded = ((counts + BM - 1) // BM) * BM
        seg_end = jnp.cumsum(padded)
        seg_start = seg_end - padded
        expert_ids = jnp.arange(N_EXPERTS, dtype=jnp.int32)
        dest = (rank + jnp.sum(jnp.where(eid[..., None] == expert_ids, seg_start, 0), axis=-1)).reshape(-1)
        blk_row = jnp.arange(n_blocks, dtype=jnp.int32) * BM
        blk_e = jnp.minimum(
            jnp.sum((seg_end[None, :] <= blk_row[:, None]).astype(jnp.int32), axis=1), N_EXPERTS - 1)
        n_used = (seg_end[-1:] // BM).astype(jnp.int32)
        nonempty = counts > 0
        first = jnp.logical_and(blk_row == seg_start[blk_e], blk_row < seg_end[-1]).astype(jnp.int32)
        seg = (jnp.cumsum(nonempty.astype(jnp.int32)) - 1)[blk_e]
        later_id = jnp.where(nonempty, expert_ids, N_EXPERTS)
        after = jnp.concatenate([lax.cummin(later_id[::-1])[::-1][1:], jnp.full((1,), N_EXPERTS, jnp.int32)])
        next_e = jnp.where(after < N_EXPERTS, after, -1)[blk_e]
        buf = _dispatch(dest, hm, jnp.zeros((n_slots * PACK_TILES, LANES), U32))
        ybuf = _experts(blk_e, first, seg, next_e, n_used, buf, w1, w3, w2, l)
        moe = (dest, x2, info, ybuf)
    dest, x2, info, ybuf = moe
    return _combine_final(dest, x2, info, norm_final.reshape(1, D), ybuf).reshape(B, S, D)
```

```python
import functools

import jax
import jax.numpy as jnp
from jax import lax
from jax.experimental import pallas as pl
from jax.experimental.pallas import tpu as pltpu

F32 = jnp.float32
BF16 = jnp.bfloat16
U32 = jnp.uint32

D_MODEL = 1024
A_WIDTH = 512
A_GROUPS = 4
A_CH = 128
CHUNK = 128
SB_WIDTH = 512
SB_HEADS = 8
SB_HEAD_DIM = 64
SB_PAIRS = SB_HEADS // 2
IN_WIDTH = 2 * A_WIDTH + 3 * SB_WIDTH
X_HEADS = 4
X_HEAD_DIM = 256
N_GROUPS = 4
EXPERTS_PER_GROUP = 8
N_EXPERTS = 32
TOP_K = 2
D_EXPERT = 512
EPS = 1e-6

LANES = 128
SUBLANES = 8
ROW_TILES = D_MODEL // LANES
PACK_TILES = ROW_TILES // 2

TM_IN = 512
TQ = 256
TK = 256
HALF = TQ // 2
TM_POST = 512
TD = 512
BM = 512
TE = 512
ROUTE_BASE = N_GROUPS
NEG = -1e30
LOG2E = 1.4426950408889634

VMEM_LIMIT = 52 * 1024 * 1024


def _rms(x, g):
    return x * lax.rsqrt(jnp.mean(x * x, axis=-1, keepdims=True) + EPS) * g


def _dot(a, b):
    return jnp.dot(a, b, preferred_element_type=F32)


def _dot_nt(a, b):
    return lax.dot_general(a, b, (((1,), (1,)), ((), ())), preferred_element_type=F32)


def _split_bf16(x):
    hi = x.astype(BF16)
    lo = (x - hi.astype(F32)).astype(BF16)
    return hi, lo


def _row_copy(src, src_row, dst, dst_row, sem):
    return pltpu.make_async_copy(
        src.at[pl.ds(pl.multiple_of(src_row * PACK_TILES, PACK_TILES), PACK_TILES), :],
        dst.at[pl.ds(pl.multiple_of(dst_row * PACK_TILES, PACK_TILES), PACK_TILES), :],
        sem)


def _store_packed(x, ref):
    half = D_MODEL // 2
    hi = pltpu.bitcast(x[:, 0:half].astype(BF16).astype(F32), U32)
    lo = pltpu.bitcast(x[:, half:D_MODEL].astype(BF16).astype(F32), U32)
    words = hi | (lo >> 16)
    for c in range(PACK_TILES):
        ref[pl.ds(c, x.shape[0], stride=PACK_TILES), :] = words[:, c * LANES:(c + 1) * LANES]


def _load_packed_chunks(ref, row0, rows):
    words = [ref[pl.ds(row0 * PACK_TILES + c, rows, stride=PACK_TILES), :] for c in range(PACK_TILES)]
    return ([pltpu.bitcast(w & jnp.uint32(0xFFFF0000), F32) for w in words]
            + [pltpu.bitcast(w << 16, F32) for w in words])


def _moe_output_chunks(dest_ref, x2_ref, info_ref, ybuf_ref, rows_ref, sems):
    tm = x2_ref.shape[0]
    i = pl.program_id(0)

    def start_gather(step, slot):
        base = step * (tm * TOP_K)

        def issue(r, c):
            for k in range(TOP_K):
                _row_copy(ybuf_ref, dest_ref[base + TOP_K * r + k], rows_ref.at[slot], k * tm + r,
                          sems.at[slot]).start(priority=k)
            return c

        lax.fori_loop(0, tm, issue, 0)

    @pl.when(i == 0)
    def _():
        start_gather(0, 0)

    @pl.when(i + 1 < pl.num_programs(0))
    def _():
        start_gather(i + 1, (i + 1) % 2)

    slot = i % 2
    rows = rows_ref.at[slot]
    pltpu.make_async_copy(ybuf_ref.at[pl.ds(0, TOP_K * tm * PACK_TILES), :], rows, sems.at[slot]).wait()

    info = info_ref[...]
    gate1 = info[:, 2:3]
    gate2 = info[:, 3:4]
    y1 = _load_packed_chunks(rows, 0, tm)
    y2 = _load_packed_chunks(rows, tm, tm)
    return [x2_ref[:, c * LANES:(c + 1) * LANES] + (gate1 * y1[c] + gate2 * y2[c]) for c in range(ROW_TILES)]


def _moe_gather_scratch(tm):
    return [pltpu.VMEM((2, TOP_K * tm * PACK_TILES, LANES), U32), pltpu.SemaphoreType.DMA((2,))]


def _mix_in_body(x, g_ref, w_ref, vg_ref, ws_ref, bs_ref, ga_ref, q_ref, k_ref, v_ref, ya_ref):
    tm = x.shape[0]
    h = _rms(x, g_ref[...]).astype(BF16)
    z = _dot(h, w_ref[...])

    t_idx = lax.broadcasted_iota(jnp.int32, (CHUNK, CHUNK), 0)
    s_idx = lax.broadcasted_iota(jnp.int32, (CHUNK, CHUNK), 1)
    causal = s_idx <= t_idx
    parts = []
    ssq = jnp.zeros((tm, 1), F32)
    for g in range(A_GROUPS):
        lanes = slice(g * A_CH, (g + 1) * A_CH)
        u = jax.nn.gelu(z[:, g * A_CH:(g + 1) * A_CH])
        vg = jax.nn.gelu(z[:, A_WIDTH + g * A_CH:A_WIDTH + (g + 1) * A_CH])
        vn = _rms(vg, vg_ref[:, lanes]).astype(BF16)
        ws = jnp.where(causal, ws_ref[g], 0.0).astype(BF16)
        bias = bs_ref[:, g:g + 1]
        mixed = jnp.concatenate(
            [_dot(ws, vn[c * CHUNK:(c + 1) * CHUNK, :]) + bias for c in range(tm // CHUNK)], axis=0)
        ya = u * mixed
        parts.append(ya)
        ssq = ssq + jnp.sum(ya * ya, axis=-1, keepdims=True)
    inv = lax.rsqrt(ssq * (1.0 / A_WIDTH) + EPS)
    for g in range(A_GROUPS):
        lanes = slice(g * A_CH, (g + 1) * A_CH)
        ya_ref[:, lanes] = (parts[g] * inv * ga_ref[:, lanes]).astype(BF16)

    lane = lax.broadcasted_iota(jnp.int32, (1, LANES), 1)
    q0 = 2 * A_WIDTH
    k0 = q0 + SB_WIDTH
    v0 = k0 + SB_WIDTH
    scale = 1.0 / (SB_HEAD_DIM ** 0.5)
    for hd in range(SB_HEADS):
        pair = hd // 2
        keep = (lane < SB_HEAD_DIM) if hd % 2 == 0 else (lane >= SB_HEAD_DIM)
        qs = z[:, q0 + pair * LANES:q0 + (pair + 1) * LANES] * scale
        vs = z[:, v0 + pair * LANES:v0 + (pair + 1) * LANES]
        q_ref[:, hd * LANES:(hd + 1) * LANES] = jnp.where(keep, qs, 0.0).astype(BF16)
        v_ref[:, hd * LANES:(hd + 1) * LANES] = jnp.where(keep, vs, 0.0).astype(BF16)
    k_ref[...] = z[:, k0:v0].astype(BF16)


def _cast_once(w_ref, wb_ref):
    @pl.when(pl.program_id(0) == 0)
    def _():
        wb_ref[...] = w_ref[...].astype(BF16)


def _mix_in_kernel(x_ref, g_ref, w_ref, vg_ref, ws_ref, bs_ref, ga_ref, q_ref, k_ref, v_ref, ya_ref, wb_ref):
    _cast_once(w_ref, wb_ref)
    _mix_in_body(x_ref[...], g_ref, wb_ref, vg_ref, ws_ref, bs_ref, ga_ref, q_ref, k_ref, v_ref, ya_ref)


def _moe_mix_in_kernel(dest_ref, x2_ref, info_ref, ybuf_ref, g_ref, w_ref, vg_ref, ws_ref, bs_ref, ga_ref,
                       x_ref, q_ref, k_ref, v_ref, ya_ref, rows_ref, sems, wb_ref):
    _cast_once(w_ref, wb_ref)
    x = jnp.concatenate(_moe_output_chunks(dest_ref, x2_ref, info_ref, ybuf_ref, rows_ref, sems), axis=-1)
    x_ref[...] = x
    _mix_in_body(x, g_ref, wb_ref, vg_ref, ws_ref, bs_ref, ga_ref, q_ref, k_ref, v_ref, ya_ref)


def _mix_in_specs(l, index):
    lay = lambda *a: (l, 0, 0)
    in_specs = [
        pl.BlockSpec((None, 1, D_MODEL), lay),
        pl.BlockSpec((None, D_MODEL, IN_WIDTH), lay),
        pl.BlockSpec((None, 1, A_WIDTH), lay),
        pl.BlockSpec((None, A_GROUPS, CHUNK, CHUNK), lambda *a: (l, 0, 0, 0)),
        pl.BlockSpec((None, CHUNK, A_GROUPS), lay),
        pl.BlockSpec((None, 1, A_WIDTH), lay),
    ]
    out_specs = [
        pl.BlockSpec((TM_IN, SB_HEADS * LANES), index),
        pl.BlockSpec((TM_IN, SB_WIDTH), index),
        pl.BlockSpec((TM_IN, SB_HEADS * LANES), index),
        pl.BlockSpec((TM_IN, A_WIDTH), index),
    ]
    return in_specs, out_specs


def _mix_in_out_shapes(T):
    return [
        jax.ShapeDtypeStruct((T, SB_HEADS * LANES), BF16),
        jax.ShapeDtypeStruct((T, SB_WIDTH), BF16),
        jax.ShapeDtypeStruct((T, SB_HEADS * LANES), BF16),
        jax.ShapeDtypeStruct((T, A_WIDTH), BF16),
    ]


def _mix_in(x2d, g, w_in, vg, ws, bs_t, ga, l):
    T = x2d.shape[0]
    row = lambda i: (i, 0)
    in_specs, out_specs = _mix_in_specs(l, row)
    return pl.pallas_call(
        _mix_in_kernel,
        grid=(T // TM_IN,),
        in_specs=[pl.BlockSpec((TM_IN, D_MODEL), row)] + in_specs,
        out_specs=out_specs,
        out_shape=_mix_in_out_shapes(T),
        scratch_shapes=[pltpu.VMEM((D_MODEL, IN_WIDTH), BF16)],
        compiler_params=pltpu.CompilerParams(dimension_semantics=("arbitrary",), vmem_limit_bytes=VMEM_LIMIT),
        name="mix_in",
    )(x2d, g, w_in, vg, ws, bs_t, ga)


def _moe_mix_in(dest, x2, info, ybuf, g, w_in, vg, ws, bs_t, ga, l):
    T = x2.shape[0]
    row = lambda i, d: (i, 0)
    in_specs, out_specs = _mix_in_specs(l, row)
    return pl.pallas_call(
        _moe_mix_in_kernel,
        grid_spec=pltpu.PrefetchScalarGridSpec(
            num_scalar_prefetch=1,
            grid=(T // TM_IN,),
            in_specs=[
                pl.BlockSpec((TM_IN, D_MODEL), row),
                pl.BlockSpec((TM_IN, LANES), row),
                pl.BlockSpec(memory_space=pl.ANY),
            ] + in_specs,
            out_specs=[pl.BlockSpec((TM_IN, D_MODEL), row)] + out_specs,
            scratch_shapes=_moe_gather_scratch(TM_IN) + [pltpu.VMEM((D_MODEL, IN_WIDTH), BF16)],
        ),
        out_shape=[jax.ShapeDtypeStruct((T, D_MODEL), F32)] + _mix_in_out_shapes(T),
        compiler_params=pltpu.CompilerParams(dimension_semantics=("arbitrary",), vmem_limit_bytes=VMEM_LIMIT),
        name="moe_mix_in",
    )(dest, x2, info, ybuf, g, w_in, vg, ws, bs_t, ga)


def _stick_break_kernel(q_ref, k_ref, v_ref, o_ref, c_ref, live_ref):
    qi = pl.program_id(1)
    r_idx = lax.broadcasted_iota(jnp.int32, (TK, TK), 0)
    c_idx = lax.broadcasted_iota(jnp.int32, (TK, TK), 1)
    later = jnp.where(c_idx < r_idx, -1.0, 0.0).astype(BF16)

    def sweep(half, key0, n_keys, later_m, visible):
        rows = slice(half * HALF, (half + 1) * HALF)
        first = visible is not None
        sps, lszs = [], []
        for pr in range(SB_PAIRS):
            kb = k_ref[pl.ds(key0, n_keys), pr * LANES:(pr + 1) * LANES]
            qq = jnp.concatenate([q_ref[rows, hd * LANES:(hd + 1) * LANES] for hd in (2 * pr, 2 * pr + 1)], axis=0)
            zz = _dot_nt(qq, kb)
            for z in (zz[0:HALF, :], zz[HALF:2 * HALF, :]):
                sp = jnp.maximum(z, 0.0) + jnp.log(1.0 + jnp.exp2(jnp.abs(z) * (-LOG2E)))
                lszs.append(z - sp)
                sps.append(jnp.where(visible, sp, 0.0) if first else sp)
        after = _dot(jnp.concatenate([sp.astype(BF16) for sp in sps], axis=0), later_m)
        top = None
        for pr in range(SB_PAIRS):
            acc = None
            for hd in (2 * pr, 2 * pr + 1):
                p = jnp.exp(lszs[hd] + after[hd * HALF:(hd + 1) * HALF, :])
                if first:
                    p = jnp.where(visible, p, 0.0)
                term = _dot(p.astype(BF16), v_ref[pl.ds(key0, n_keys), hd * LANES:(hd + 1) * LANES])
                c = -jnp.sum(sps[hd], axis=-1, keepdims=True)
                if not first:
                    c_old = c_ref[hd, rows, :]
                    term = term * jnp.exp(c_old)
                    c = c_old + c
                c_ref[hd, rows, :] = c
                acc = term if acc is None else acc + term
                top = c if top is None else jnp.maximum(top, c)
            lanes = slice(pr * LANES, (pr + 1) * LANES)
            o_ref[rows, lanes] = acc if first else o_ref[rows, lanes] + acc
        edge = jnp.exp(jnp.max(top, axis=0, keepdims=True) + 1.0)
        live_ref[half] = (jnp.max(edge) > 0.0).astype(jnp.int32)

    d0 = pl.multiple_of(qi * TK, TK)
    rows_a = lax.broadcasted_iota(jnp.int32, (HALF, HALF), 0)
    cols_a = lax.broadcasted_iota(jnp.int32, (HALF, HALF), 1)
    sweep(0, d0, HALF, later[0:HALF, 0:HALF], cols_a < rows_a)
    rows_b = lax.broadcasted_iota(jnp.int32, (HALF, TK), 0)
    cols_b = lax.broadcasted_iota(jnp.int32, (HALF, TK), 1)
    sweep(1, d0, TK, later, cols_b < rows_b + HALF)

    def cond(jj):
        return jnp.logical_and(jj < qi, live_ref[0] + live_ref[1] > 0)

    def body(jj):
        key0 = pl.multiple_of((qi - 1 - jj) * TK, TK)
        for half in range(2):
            @pl.when(live_ref[half] > 0)
            def _():
                sweep(half, key0, TK, later, None)
        return jj + 1

    lax.while_loop(cond, body, jnp.int32(0))


def _stick_break(q, k, v, B, S):
    return pl.pallas_call(
        _stick_break_kernel,
        grid=(B, S // TQ),
        in_specs=[
            pl.BlockSpec((None, TQ, SB_HEADS * LANES), lambda b, i: (b, i, 0)),
            pl.BlockSpec((None, S, SB_WIDTH), lambda b, i: (b, 0, 0)),
            pl.BlockSpec((None, S, SB_HEADS * LANES), lambda b, i: (b, 0, 0)),
        ],
        out_specs=pl.BlockSpec((None, TQ, SB_WIDTH), lambda b, i: (b, i, 0)),
        out_shape=jax.ShapeDtypeStruct((B, S, SB_WIDTH), F32),
        scratch_shapes=[pltpu.VMEM((SB_HEADS, TQ, 1), F32), pltpu.SMEM((2,), jnp.int32)],
        compiler_params=pltpu.CompilerParams(
            dimension_semantics=("arbitrary", "arbitrary"), vmem_limit_bytes=VMEM_LIMIT),
        name="stick_break",
    )(q.reshape(B, S, SB_HEADS * LANES), k.reshape(B, S, SB_WIDTH), v.reshape(B, S, SB_HEADS * LANES))


def _mem_kv_kernel(m_ref, g_ref, wk_ref, wv_ref, k_ref, v_ref, wkb_ref, wvb_ref):
    @pl.when(pl.program_id(1) == 0)
    def _():
        wkb_ref[...] = wk_ref[...].astype(BF16)
        wvb_ref[...] = wv_ref[...].astype(BF16)

    m = _rms(m_ref[...], g_ref[...]).astype(BF16)
    k_ref[...] = _dot(m, wkb_ref[...]).astype(BF16)
    v_ref[...] = _dot(m, wvb_ref[...]).astype(BF16)


def _mem_kv(mem, g, wk, wv):
    B, M, _ = mem.shape
    L = wk.shape[0]
    kv_spec = pl.BlockSpec((None, None, M, D_MODEL), lambda l, b: (l, b, 0, 0))
    w_spec = pl.BlockSpec((None, D_MODEL, D_MODEL), lambda l, b: (l, 0, 0))
    return pl.pallas_call(
        _mem_kv_kernel,
        grid=(L, B),
        in_specs=[
            pl.BlockSpec((None, M, D_MODEL), lambda l, b: (b, 0, 0)),
            pl.BlockSpec((None, 1, D_MODEL), lambda l, b: (l, 0, 0)),
            w_spec, w_spec,
        ],
        out_specs=[kv_spec, kv_spec],
        out_shape=[jax.ShapeDtypeStruct((L, B, M, D_MODEL), BF16)] * 2,
        scratch_shapes=[pltpu.VMEM((D_MODEL, D_MODEL), BF16)] * 2,
        compiler_params=pltpu.CompilerParams(
            dimension_semantics=("arbitrary", "arbitrary"), vmem_limit_bytes=VMEM_LIMIT),
        name="mem_kv",
    )(mem, g, wk, wv)


def _post_kernel(x_ref, ya_ref, yb_ref, gb_ref, wo_ref, gc_ref, wq_ref, kx_ref, vx_ref, wxo_ref, gm_ref,
                 wr_ref, br_ref, x2_ref, hm_ref, info_ref, cnt_ref, run_ref, wob_ref, wqb_ref, wxob_ref):
    tm = x_ref.shape[0]

    @pl.when(pl.program_id(0) == 0)
    def _():
        run_ref[...] = jnp.zeros_like(run_ref)

    _cast_once(wo_ref, wob_ref)
    _cast_once(wq_ref, wqb_ref)
    _cast_once(wxo_ref, wxob_ref)

    ybn = _rms(yb_ref[...], gb_ref[...]).astype(BF16)
    x1 = x_ref[...] + _dot(jnp.concatenate([ya_ref[...], ybn], axis=-1), wob_ref[...])

    h = _rms(x1, gc_ref[...]).astype(BF16)
    q = (_dot(h, wqb_ref[...]) * (1.0 / (X_HEAD_DIM ** 0.5))).astype(BF16)
    heads = []
    for hd in range(X_HEADS):
        cols = slice(hd * X_HEAD_DIM, (hd + 1) * X_HEAD_DIM)
        s = _dot_nt(q[:, cols], kx_ref[:, cols])
        p = jnp.exp(s - jnp.max(s, axis=-1, keepdims=True))
        p = p / jnp.sum(p, axis=-1, keepdims=True)
        heads.append(_dot(p.astype(BF16), vx_ref[:, cols]).astype(BF16))
    x2 = x1 + _dot(jnp.concatenate(heads, axis=-1), wxob_ref[...])
    x2_ref[...] = x2

    hm = _rms(x2, gm_ref[...])
    _store_packed(hm, hm_ref)

    hm_hi, hm_lo = _split_bf16(hm)
    both = _dot(hm_hi, wr_ref[...])
    logits = both[:, 0:LANES] + both[:, LANES:2 * LANES] + _dot(hm_lo, wr_ref[:, 0:LANES]) + br_ref[...]

    lane = lax.broadcasted_iota(jnp.int32, (tm, LANES), 1)
    lanef = lane.astype(F32)
    big = float(LANES)
    is_group = lane < N_GROUPS
    gl = jnp.where(is_group, logits, NEG)
    gmax = jnp.max(gl, axis=-1, keepdims=True)
    gsel = jnp.min(jnp.where(gl == gmax, lanef, big), axis=-1, keepdims=True)
    gden = jnp.sum(jnp.where(is_group, jnp.exp(gl - gmax), 0.0), axis=-1, keepdims=True)
    g_gate = 1.0 / gden
    lo = ROUTE_BASE + EXPERTS_PER_GROUP * gsel
    in_group = (lanef >= lo) & (lanef < lo + EXPERTS_PER_GROUP)
    el = jnp.where(in_group, logits, NEG)
    v1 = jnp.max(el, axis=-1, keepdims=True)
    i1 = jnp.min(jnp.where(el == v1, lanef, big), axis=-1, keepdims=True)
    el2 = jnp.where(lanef == i1, NEG, el)
    v2 = jnp.max(el2, axis=-1, keepdims=True)
    i2 = jnp.min(jnp.where(el2 == v2, lanef, big), axis=-1, keepdims=True)
    t = jnp.exp(v2 - v1)
    den = 1.0 + t
    gate1 = g_gate * (1.0 / den)
    gate2 = g_gate * (t / den)

    hit1 = lanef == i1
    hit2 = lanef == i2
    multi = jnp.where(hit1 | hit2, 1.0, 0.0)
    r_idx = lax.broadcasted_iota(jnp.int32, (tm, tm), 0)
    c_idx = lax.broadcasted_iota(jnp.int32, (tm, tm), 1)
    earlier = jnp.where(c_idx < r_idx, 1.0, 0.0).astype(BF16)
    before = _dot(earlier, multi.astype(BF16)) + run_ref[0:1, :]
    rank1 = jnp.sum(jnp.where(hit1, before, 0.0), axis=-1, keepdims=True)
    rank2 = jnp.sum(jnp.where(hit2, before, 0.0), axis=-1, keepdims=True)
    run_ref[...] = run_ref[...] + jnp.sum(multi, axis=0, keepdims=True)
    cnt_ref[...] = run_ref[...]

    info = jnp.where(lane == 0, i1 - ROUTE_BASE, 0.0)
    info = jnp.where(lane == 1, i2 - ROUTE_BASE, info)
    info = jnp.where(lane == 2, gate1, info)
    info = jnp.where(lane == 3, gate2, info)
    info = jnp.where(lane == 4, rank1, info)
    info = jnp.where(lane == 5, rank2, info)
    info_ref[...] = info


def _post(x2d, ya, yb, gb, wo, gc, wq, kx, vx, wxo, gm, wr, br, l, S):
    T = x2d.shape[0]
    M = kx.shape[2]
    grid = (T // TM_POST,)
    row = lambda i: (i, 0)
    lay = lambda i: (l, 0, 0)
    batch = lambda i: (l, (i * TM_POST) // S, 0, 0)
    wspec = pl.BlockSpec((None, D_MODEL, D_MODEL), lay)
    gspec = pl.BlockSpec((None, 1, D_MODEL), lay)
    return pl.pallas_call(
        _post_kernel,
        grid=grid,
        in_specs=[
            pl.BlockSpec((TM_POST, D_MODEL), row),
            pl.BlockSpec((TM_POST, A_WIDTH), row),
            pl.BlockSpec((TM_POST, SB_WIDTH), row),
            pl.BlockSpec((None, 1, SB_WIDTH), lay),
            wspec, gspec, wspec,
            pl.BlockSpec((None, None, M, D_MODEL), batch),
            pl.BlockSpec((None, None, M, D_MODEL), batch),
            wspec, gspec,
            pl.BlockSpec((None, D_MODEL, 2 * LANES), lay),
            pl.BlockSpec((None, 1, LANES), lay),
        ],
        out_specs=[
            pl.BlockSpec((TM_POST, D_MODEL), row),
            pl.BlockSpec((TM_POST * PACK_TILES, LANES), row),
            pl.BlockSpec((TM_POST, LANES), row),
            pl.BlockSpec((SUBLANES, LANES), lambda i: (0, 0)),
        ],
        out_shape=[
            jax.ShapeDtypeStruct((T, D_MODEL), F32),
            jax.ShapeDtypeStruct((T * PACK_TILES, LANES), U32),
            jax.ShapeDtypeStruct((T, LANES), F32),
            jax.ShapeDtypeStruct((SUBLANES, LANES), F32),
        ],
        scratch_shapes=[pltpu.VMEM((SUBLANES, LANES), F32)] + [pltpu.VMEM((D_MODEL, D_MODEL), BF16)] * 3,
        compiler_params=pltpu.CompilerParams(dimension_semantics=("arbitrary",), vmem_limit_bytes=VMEM_LIMIT),
        name="post",
    )(x2d, ya, yb, gb, wo, gc, wq, kx, vx, wxo, gm, wr, br)


def _dispatch_kernel(dest_ref, hm_ref, buf_in_ref, buf_ref, sem):
    del buf_in_ref
    base = pl.program_id(0) * (TD * TOP_K)

    def issue(r, c):
        for k in range(TOP_K):
            _row_copy(hm_ref, r, buf_ref, dest_ref[base + TOP_K * r + k], sem).start(priority=k)
        return c

    lax.fori_loop(0, TD, issue, 0)
    for k in range(TOP_K):
        pltpu.make_async_copy(hm_ref, buf_ref.at[pl.ds(0, TD * PACK_TILES), :], sem).wait()


def _dispatch(dest, hm, buf0):
    T = hm.shape[0] // PACK_TILES
    return pl.pallas_call(
        _dispatch_kernel,
        grid_spec=pltpu.PrefetchScalarGridSpec(
            num_scalar_prefetch=1,
            grid=(T // TD,),
            in_specs=[
                pl.BlockSpec((TD * PACK_TILES, LANES), lambda i, d: (i, 0)),
                pl.BlockSpec(memory_space=pl.ANY),
            ],
            out_specs=pl.BlockSpec(memory_space=pl.ANY),
            scratch_shapes=[pltpu.SemaphoreType.DMA(())],
        ),
        out_shape=jax.ShapeDtypeStruct(buf0.shape, U32),
        input_output_aliases={2: 0},
        compiler_params=pltpu.CompilerParams(dimension_semantics=("arbitrary",), vmem_limit_bytes=VMEM_LIMIT),
        name="dispatch",
    )(dest, hm, buf0)


def _experts_kernel(blk_e_ref, first_ref, seg_ref, next_e_ref, n_used_ref, buf_ref, w1_hbm, w3_hbm, w2_hbm, y_ref,
                    w1s_ref, w3s_ref, w2s_ref, w1b_ref, w3b_ref, w2b_ref, sems, *, layer):
    i = pl.program_id(0)

    def fetch(expert, slot):
        return [pltpu.make_async_copy(w_hbm.at[layer, expert], stage.at[slot], sems.at[slot])
                for w_hbm, stage in ((w1_hbm, w1s_ref), (w3_hbm, w3s_ref), (w2_hbm, w2s_ref))]

    @pl.when(first_ref[i] > 0)
    def _():
        slot = seg_ref[i] % 2

        @pl.when(i == 0)
        def _():
            for cp in fetch(blk_e_ref[0], 0):
                cp.start()

        for cp in fetch(blk_e_ref[i], slot):
            cp.wait()
        w1b_ref[...] = w1s_ref[slot].astype(BF16)
        w3b_ref[...] = w3s_ref[slot].astype(BF16)
        w2b_ref[...] = w2s_ref[slot].astype(BF16)

        @pl.when(next_e_ref[i] >= 0)
        def _():
            for cp in fetch(next_e_ref[i], 1 - slot):
                cp.start()

    @pl.when(i < n_used_ref[0])
    def _():
        xb = jnp.concatenate(_load_packed_chunks(buf_ref, 0, BM), axis=-1).astype(BF16)
        h1 = _dot(xb, w1b_ref[...])
        h3 = _dot(xb, w3b_ref[...])
        a = (h1 * jax.nn.sigmoid(h1) * h3).astype(BF16)
        _store_packed(_dot(a, w2b_ref[...]), y_ref)

    @pl.when(i >= n_used_ref[0])
    def _():
        y_ref[...] = jnp.zeros_like(y_ref)


def _experts(blk_e, first, seg, next_e, n_used, buf, w1, w3, w2, l):
    n_blocks = buf.shape[0] // (BM * PACK_TILES)
    rows = pl.BlockSpec((BM * PACK_TILES, LANES), lambda i, *_: (i, 0))
    rows_in = pl.BlockSpec((BM * PACK_TILES, LANES), lambda i, e, f, s, nx, n: (jnp.minimum(i, n[0] - 1), 0))
    hbm = pl.BlockSpec(memory_space=pl.ANY)
    return pl.pallas_call(
        functools.partial(_experts_kernel, layer=l),
        grid_spec=pltpu.PrefetchScalarGridSpec(
            num_scalar_prefetch=5,
            grid=(n_blocks,),
            in_specs=[rows_in, hbm, hbm, hbm],
            out_specs=rows,
            scratch_shapes=[
                pltpu.VMEM((2, D_MODEL, D_EXPERT), F32),
                pltpu.VMEM((2, D_MODEL, D_EXPERT), F32),
                pltpu.VMEM((2, D_EXPERT, D_MODEL), F32),
                pltpu.VMEM((D_MODEL, D_EXPERT), BF16),
                pltpu.VMEM((D_MODEL, D_EXPERT), BF16),
                pltpu.VMEM((D_EXPERT, D_MODEL), BF16),
                pltpu.SemaphoreType.DMA((2,)),
            ],
        ),
        out_shape=jax.ShapeDtypeStruct(buf.shape, U32),
        compiler_params=pltpu.CompilerParams(dimension_semantics=("arbitrary",), vmem_limit_bytes=VMEM_LIMIT),
        name="experts",
    )(blk_e, first, seg, next_e, n_used, buf, w1, w3, w2)


def _combine_final_kernel(dest_ref, x2_ref, info_ref, gf_ref, ybuf_ref, out_ref, rows_ref, sems):
    chunks = _moe_output_chunks(dest_ref, x2_ref, info_ref, ybuf_ref, rows_ref, sems)
    ssq = jnp.zeros((x2_ref.shape[0], 1), F32)
    for xc in chunks:
        ssq = ssq + jnp.sum(xc * xc, axis=-1, keepdims=True)
    inv = lax.rsqrt(ssq * (1.0 / D_MODEL) + EPS)
    for c, xc in enumerate(chunks):
        lanes = slice(c * LANES, (c + 1) * LANES)
        out_ref[:, lanes] = xc * inv * gf_ref[:, lanes]


def _combine_final(dest, x2, info, gf, ybuf):
    T = x2.shape[0]
    row = lambda i, d: (i, 0)
    return pl.pallas_call(
        _combine_final_kernel,
        grid_spec=pltpu.PrefetchScalarGridSpec(
            num_scalar_prefetch=1,
            grid=(T // TE,),
            in_specs=[
                pl.BlockSpec((TE, D_MODEL), row),
                pl.BlockSpec((TE, LANES), row),
                pl.BlockSpec((1, D_MODEL), lambda i, d: (0, 0)),
                pl.BlockSpec(memory_space=pl.ANY),
            ],
            out_specs=pl.BlockSpec((TE, D_MODEL), row),
            scratch_shapes=_moe_gather_scratch(TE),
        ),
        out_shape=jax.ShapeDtypeStruct((T, D_MODEL), F32),
        compiler_params=pltpu.CompilerParams(dimension_semantics=("arbitrary",), vmem_limit_bytes=VMEM_LIMIT),
        name="combine_final",
    )(dest, x2, info, gf, ybuf)


def kernel(x, mem, norm_mix, w_in, v_norm, w_spatial, b_spatial, out_norm_a, out_norm_b, w_out, norm_cross, norm_mem, w_xq, w_xk, w_xv, w_xo, norm_moe, w_group, b_group, w_router, b_router, w1, w3, w2, norm_final):
    B, S, D = x.shape
    L = w_in.shape[0]
    T = B * S
    assert D == D_MODEL and S % TQ == 0 and T % TM_IN == 0 and S % TM_POST == 0 and T % TD == 0 and T % TE == 0

    row3 = lambda a: a.reshape(L, 1, -1)
    pad = jnp.zeros((L, D, LANES - N_GROUPS - N_EXPERTS), F32)
    w_route = jnp.concatenate([w_group, w_router, pad], axis=-1)
    w_route_hi = w_route.astype(BF16)
    w_route_lo = (w_route - w_route_hi.astype(F32)).astype(BF16)
    w_route_b = jnp.concatenate([w_route_hi, w_route_lo], axis=-1)
    b_route = jnp.concatenate([b_group, b_router, pad[:, 0, :]], axis=-1).reshape(L, 1, LANES)
    bs_t = jnp.swapaxes(b_spatial, 1, 2)

    kx, vx = _mem_kv(mem, row3(norm_mem), w_xk, w_xv)

    n_slots = T * TOP_K + N_EXPERTS * BM
    n_blocks = n_slots // BM
    xs = x.reshape(T, D)
    buf = jnp.zeros((n_slots * PACK_TILES, LANES), U32)
    moe = None
    for l in range(L):
        mix_args = (row3(norm_mix), w_in, row3(v_norm), w_spatial, bs_t, row3(out_norm_a), l)
        if moe is None:
            q, k, v, ya = _mix_in(xs, *mix_args)
        else:
            xs, q, k, v, ya = _moe_mix_in(*moe, *mix_args)
        yb = _stick_break(q, k, v, B, S).reshape(T, SB_WIDTH)
        x2, hm, info, cnt = _post(xs, ya, yb, row3(out_norm_b), w_out, row3(norm_cross), w_xq, kx, vx, w_xo,
                                  row3(norm_moe), w_route_b, b_route, l, S)
        eid = info[:, 0:TOP_K].astype(jnp.int32)
        rank = info[:, 4:4 + TOP_K].astype(jnp.int32)
        counts = cnt[0, ROUTE_BASE:ROUTE_BASE + N_EXPERTS].astype(jnp.int32)
        padded = ((counts + BM - 1) // BM) * BM
        seg_end = jnp.cumsum(padded)
        seg_start = seg_end - padded
        expert_ids = jnp.arange(N_EXPERTS, dtype=jnp.int32)
        dest = (rank + jnp.sum(jnp.where(eid[..., None] == expert_ids, seg_start, 0), axis=-1)).reshape(-1)
        blk_row = jnp.arange(n_blocks, dtype=jnp.int32) * BM
        blk_e = jnp.minimum(
            jnp.sum((seg_end[None, :] <= blk_row[:, None]).astype(jnp.int32), axis=1), N_EXPERTS - 1)
        n_used = (seg_end[-1:] // BM).astype(jnp.int32)
        nonempty = counts > 0
        first = jnp.logical_and(blk_row == seg_start[blk_e], blk_row < seg_end[-1]).astype(jnp.int32)
        seg = (jnp.cumsum(nonempty.astype(jnp.int32)) - 1)[blk_e]
        later_id = jnp.where(nonempty, expert_ids, N_EXPERTS)
        after = jnp.concatenate([lax.cummin(later_id[::-1])[::-1][1:], jnp.full((1,), N_EXPERTS, jnp.int32)])
        next_e = jnp.where(after < N_EXPERTS, after, -1)[blk_e]
        buf = _dispatch(dest, hm, buf)
        ybuf = _experts(blk_e, first, seg, next_e, n_used, buf, w1, w3, w2, l)
        moe = (dest, x2, info, ybuf)
    dest, x2, info, ybuf = moe
    return _combine_final(dest, x2, info, norm_final.reshape(1, D), ybuf).reshape(B, S, D)
```
